```python
import jax, jax.numpy as jnp
from jax import lax
import numpy as np

D_MODEL = 1024
BATCH = 8
SEQ = 2048
DEPTH = 2

HEAD_DIM = 64
FOX_HEADS = 4
RWKV_HEADS = 8
MOBA_HEADS = 4
FOX_W = FOX_HEADS * HEAD_DIM
RWKV_W = RWKV_HEADS * HEAD_DIM
MOBA_W = MOBA_HEADS * HEAD_DIM
D_MIX = FOX_W + RWKV_W + MOBA_W
DECAY_LORA = 64
AAA_LORA = 64
GATE_LORA = 128
MOBA_BLOCK = 256
MOBA_TOPK = 3
FOX_Q_BLOCK = 128
MOBA_Q_BLOCK = 64
D_FF = 2816
CONV_W = 3
NORM_EPS = 1e-6
GN_EPS = 64e-5

FOX_COLS = 3 * FOX_W + FOX_HEADS
RWKV_COLS = 3 * RWKV_W + DECAY_LORA + AAA_LORA + GATE_LORA
MOBA_COLS = 3 * MOBA_W
D_IN = FOX_COLS + RWKV_COLS + MOBA_COLS
RWKV_SPLITS = [RWKV_W, 2 * RWKV_W, 3 * RWKV_W, 3 * RWKV_W + DECAY_LORA, 3 * RWKV_W + DECAY_LORA + AAA_LORA]

kernel_name = "hymba_style_fox_rwkv7_moba_hybrid"


def _rmsnorm(x, g):
    xf = x.astype(jnp.float32)
    y = xf * lax.rsqrt(jnp.mean(xf * xf, axis=-1, keepdims=True) + NORM_EPS)
    return (y * g).astype(x.dtype)


def _split_heads(t, n_heads):
    b, s, _ = t.shape
    return t.reshape(b, s, n_heads, HEAD_DIM).transpose(0, 2, 1, 3)


def _merge_heads(t):
    b, h, s, d = t.shape
    return t.transpose(0, 2, 1, 3).reshape(b, s, h * d)


def _shift_right(t, n):
    return jnp.pad(t, ((0, 0), (n, 0), (0, 0)))[:, : t.shape[1]]


def fox_attention(q, k, v, f_logit, f_bias):
    b, s, _ = q.shape
    q, k, v = (_split_heads(t, FOX_HEADS) for t in (q, k, v))
    log_f = jax.nn.log_sigmoid(f_logit.astype(jnp.float32) + f_bias.astype(jnp.float32))
    cum = jnp.cumsum(log_f, axis=1).transpose(0, 2, 1)
    n_blk = s // FOX_Q_BLOCK
    q_blocks = q.reshape(b, FOX_HEADS, n_blk, FOX_Q_BLOCK, HEAD_DIM).transpose(2, 0, 1, 3, 4)
    cum_blocks = cum.reshape(b, FOX_HEADS, n_blk, FOX_Q_BLOCK).transpose(2, 0, 1, 3)
    k_pos = jnp.arange(s)
    scale = HEAD_DIM ** -0.5

    def one_block(args):
        q_i, cum_i, i = args
        q_pos = i * FOX_Q_BLOCK + jnp.arange(FOX_Q_BLOCK)
        logits = jnp.einsum('bhqd,bhkd->bhqk', q_i, k).astype(jnp.float32) * scale
        logits = logits + cum_i[..., :, None] - cum[..., None, :]
        logits = jnp.where(k_pos[None, :] <= q_pos[:, None], logits, -jnp.inf)
        probs = jax.nn.softmax(logits, axis=-1).astype(v.dtype)
        return jnp.einsum('bhqk,bhkd->bhqd', probs, v)

    out = lax.map(one_block, (q_blocks, cum_blocks, jnp.arange(n_blk)))
    return _merge_heads(out.transpose(1, 2, 0, 3, 4).reshape(b, FOX_HEADS, s, HEAD_DIM))


def rwkv7_time_mix(feat, mu, w0, w2, a0, a2, g2, k_k, k_a, r_k, ln_w, ln_b):
    b, s, _ = feat.shape
    feat = feat + (_shift_right(feat, 1) - feat) * mu
    r, k, v, w_lo, a_lo, g_lo = jnp.split(feat, RWKV_SPLITS, axis=-1)
    log_w = -jax.nn.softplus(-(w0 + jnp.tanh(w_lo) @ w2)) - 0.5
    decay = jnp.exp(-jnp.exp(log_w.astype(jnp.float32)))
    a = jax.nn.sigmoid(a0 + a_lo @ a2)
    g = jax.nn.sigmoid(g_lo) @ g2
    kk = (k * k_k).astype(jnp.float32).reshape(b, s, RWKV_HEADS, HEAD_DIM)
    kk = kk / jnp.maximum(jnp.sqrt(jnp.sum(kk * kk, axis=-1, keepdims=True)), 1e-12)
    k = k * (1.0 + (a - 1.0) * k_a)

    def heads(t):
        return t.astype(jnp.float32).reshape(b, s, RWKV_HEADS, HEAD_DIM)

    rh, kh, vh, ah, wh = heads(r), heads(k), heads(v), heads(a), heads(decay)
    xs = tuple(t.transpose(1, 0, 2, 3) for t in (rh, wh, kh, vh, kk, ah))

    def step(state, inp):
        r_t, w_t, k_t, v_t, kk_t, a_t = inp
        removed = jnp.einsum('bhvk,bhk->bhv', state, kk_t)
        state = (state * w_t[:, :, None, :]
                 - removed[..., None] * (kk_t * a_t)[:, :, None, :]
                 + v_t[..., None] * k_t[:, :, None, :])
        return state, jnp.einsum('bhvk,bhk->bhv', state, r_t)

    state0 = jnp.zeros((b, RWKV_HEADS, HEAD_DIM, HEAD_DIM), jnp.float32)
    _, y = lax.scan(step, state0, xs)
    y = y.transpose(1, 0, 2, 3)
    mean = jnp.mean(y, axis=-1, keepdims=True)
    var = jnp.mean(jnp.square(y - mean), axis=-1, keepdims=True)
    y = ((y - mean) * lax.rsqrt(var + GN_EPS)).reshape(b, s, RWKV_W) * ln_w + ln_b
    bonus = jnp.sum(rh * kh * r_k, axis=-1, keepdims=True) * vh
    y = (y + bonus.reshape(b, s, RWKV_W)) * g
    return y.astype(feat.dtype)


def moba_attention(q, k, v):
    b, s, _ = q.shape
    q, k, v = (_split_heads(t, MOBA_HEADS) for t in (q, k, v))
    n_kb = -(-s // MOBA_BLOCK)
    s_pad = n_kb * MOBA_BLOCK
    pad = ((0, 0), (0, 0), (0, s_pad - s), (0, 0))
    k_p, v_p = jnp.pad(k, pad), jnp.pad(v, pad)
    k_blocks = k_p.reshape(b, MOBA_HEADS, n_kb, MOBA_BLOCK, HEAD_DIM)
    v_blocks = v_p.reshape(b, MOBA_HEADS, n_kb, MOBA_BLOCK, HEAD_DIM)
    k_mean = jnp.mean(k_blocks.astype(jnp.float32), axis=3)
    top_k = min(MOBA_TOPK, n_kb)
    n_qc = s // MOBA_Q_BLOCK
    q_chunks = q.reshape(b, MOBA_HEADS, n_qc, MOBA_Q_BLOCK, HEAD_DIM).transpose(2, 0, 1, 3, 4)
    gather_blocks = jax.vmap(jax.vmap(lambda blocks, idx: blocks[idx]))
    blk_ids = jnp.arange(n_kb)
    own_offsets = jnp.arange(MOBA_BLOCK)
    scale = HEAD_DIM ** -0.5

    def one_chunk(args):
        q_i, ci = args
        q_pos = ci * MOBA_Q_BLOCK + jnp.arange(MOBA_Q_BLOCK)
        own = (ci * MOBA_Q_BLOCK) // MOBA_BLOCK
        gate = jnp.einsum('bhqd,bhnd->bhqn', q_i.astype(jnp.float32), k_mean)
        gate = jnp.where(blk_ids < own, gate, -jnp.inf)
        gate_val, sel = lax.top_k(gate, top_k)
        valid = jnp.isfinite(gate_val)
        k_sel = gather_blocks(k_blocks, sel)
        v_sel = gather_blocks(v_blocks, sel)
        s_sel = jnp.einsum('bhqd,bhqnkd->bhqnk', q_i, k_sel).astype(jnp.float32) * scale
        s_sel = jnp.where(valid[..., None], s_sel, -jnp.inf)
        k_own = lax.dynamic_slice_in_dim(k_p, own * MOBA_BLOCK, MOBA_BLOCK, axis=2)
        v_own = lax.dynamic_slice_in_dim(v_p, own * MOBA_BLOCK, MOBA_BLOCK, axis=2)
        s_own = jnp.einsum('bhqd,bhkd->bhqk', q_i, k_own).astype(jnp.float32) * scale
        s_own = jnp.where(own * MOBA_BLOCK + own_offsets[None, :] <= q_pos[:, None], s_own, -jnp.inf)
        logits = jnp.concatenate([s_sel.reshape(b, MOBA_HEADS, MOBA_Q_BLOCK, top_k * MOBA_BLOCK), s_own], axis=-1)
        probs = jax.nn.softmax(logits, axis=-1).astype(v.dtype)
        p_sel = probs[..., : top_k * MOBA_BLOCK].reshape(b, MOBA_HEADS, MOBA_Q_BLOCK, top_k, MOBA_BLOCK)
        p_own = probs[..., top_k * MOBA_BLOCK:]
        return (jnp.einsum('bhqnk,bhqnkd->bhqd', p_sel, v_sel)
                + jnp.einsum('bhqk,bhkd->bhqd', p_own, v_own))

    out = lax.map(one_chunk, (q_chunks, jnp.arange(n_qc)))
    return _merge_heads(out.transpose(1, 2, 0, 3, 4).reshape(b, MOBA_HEADS, s, HEAD_DIM))


def _causal_dwconv(u, w, bias):
    out = bias
    for j in range(CONV_W):
        out = out + _shift_right(u, CONV_W - 1 - j) * w[j]
    return out


def setup_inputs(seed: int = 0) -> dict:
    key = jax.random.key(seed)
    ks = iter(jax.random.split(key, 32))

    def nrm(shape, scale):
        return scale * jax.random.normal(next(ks), shape, jnp.float32)

    L, D = DEPTH, D_MODEL
    return {
        "x": nrm((BATCH, SEQ, D), 1.0),
        "c": nrm((BATCH, D), 1.0),
        "w_mod": nrm((L, D, 6 * D), 0.5 * D ** -0.5),
        "b_mod": nrm((L, 6 * D), 0.02),
        "norm_mix": 1.0 + nrm((L, D), 0.02),
        "w_in": nrm((L, D, D_IN), D ** -0.5),
        "fox_f_bias": 2.0 + nrm((L, FOX_HEADS), 0.1),
        "rwkv_mu": jax.random.uniform(next(ks), (L, RWKV_COLS), jnp.float32),
        "rwkv_w0": -1.0 + nrm((L, RWKV_W), 0.5),
        "rwkv_w2": nrm((L, DECAY_LORA, RWKV_W), 0.5 * DECAY_LORA ** -0.5),
        "rwkv_a0": nrm((L, RWKV_W), 0.1),
        "rwkv_a2": nrm((L, AAA_LORA, RWKV_W), AAA_LORA ** -0.5),
        "rwkv_g2": nrm((L, GATE_LORA, RWKV_W), GATE_LORA ** -0.5),
        "rwkv_k_k": 0.85 + nrm((L, RWKV_W), 0.05),
        "rwkv_k_a": 1.0 + nrm((L, RWKV_W), 0.05),
        "rwkv_r_k": nrm((L, RWKV_HEADS, HEAD_DIM), 0.1),
        "rwkv_ln_w": 1.0 + nrm((L, RWKV_W), 0.02),
        "rwkv_ln_b": nrm((L, RWKV_W), 0.02),
        "w_out": nrm((L, D_MIX, D), D_MIX ** -0.5),
        "norm_ffn": 1.0 + nrm((L, D), 0.02),
        "w_up": nrm((L, D, 2 * D_FF), D ** -0.5),
        "conv_w": nrm((L, CONV_W, 2 * D_FF), CONV_W ** -0.5),
        "conv_b": nrm((L, 2 * D_FF), 0.02),
        "w_down": nrm((L, D_FF, D), D_FF ** -0.5),
        "norm_final": 1.0 + nrm((D,), 0.02),
    }


def reference(x, c, w_mod, b_mod, norm_mix, w_in, fox_f_bias, rwkv_mu, rwkv_w0, rwkv_w2, rwkv_a0, rwkv_a2,
              rwkv_g2, rwkv_k_k, rwkv_k_a, rwkv_r_k, rwkv_ln_w, rwkv_ln_b, w_out, norm_ffn, w_up, conv_w,
              conv_b, w_down, norm_final):
    c_act = jax.nn.silu(c)
    for l in range(DEPTH):
        mod = c_act @ w_mod[l] + b_mod[l]
        sh1, sc1, g1, sh2, sc2, g2 = (m[:, None, :] for m in jnp.split(mod, 6, axis=-1))
        h = _rmsnorm(x, norm_mix[l]) * (1.0 + sc1) + sh1
        p = h @ w_in[l]
        p_fox, p_rwkv, p_moba = jnp.split(p, [FOX_COLS, FOX_COLS + RWKV_COLS], axis=-1)
        fq, fk, fv, ff = jnp.split(p_fox, [FOX_W, 2 * FOX_W, 3 * FOX_W], axis=-1)
        y_fox = fox_attention(fq, fk, fv, ff, fox_f_bias[l])
        y_rwkv = rwkv7_time_mix(p_rwkv, rwkv_mu[l], rwkv_w0[l], rwkv_w2[l], rwkv_a0[l], rwkv_a2[l], rwkv_g2[l],
                                rwkv_k_k[l], rwkv_k_a[l], rwkv_r_k[l], rwkv_ln_w[l], rwkv_ln_b[l])
        mq, mk, mv = jnp.split(p_moba, [MOBA_W, 2 * MOBA_W], axis=-1)
        y_moba = moba_attention(mq, mk, mv)
        y = jnp.concatenate([y_fox.astype(x.dtype), y_rwkv.astype(x.dtype), y_moba.astype(x.dtype)], axis=-1)
        x = x + g1 * (y @ w_out[l])
        h = _rmsnorm(x, norm_ffn[l]) * (1.0 + sc2) + sh2
        u = _causal_dwconv(h @ w_up[l], conv_w[l], conv_b[l])
        u_gate, u_val = jnp.split(u, 2, axis=-1)
        x = x + g2 * ((jax.nn.silu(u_gate) * u_val) @ w_down[l])
    return _rmsnorm(x, norm_final)
```

```python
import functools

import jax
import jax.numpy as jnp
from jax import lax
from jax.experimental import pallas as pl
from jax.experimental.pallas import tpu as pltpu

F32 = jnp.float32
BF16 = jnp.bfloat16
HI = lax.Precision.HIGHEST

HEAD_DIM = 64
FOX_HEADS = 4
RWKV_HEADS = 8
MOBA_HEADS = 4
FOX_W = FOX_HEADS * HEAD_DIM
RWKV_W = RWKV_HEADS * HEAD_DIM
MOBA_W = MOBA_HEADS * HEAD_DIM
DECAY_LORA = 64
AAA_LORA = 64
GATE_LORA = 128
LORA_W = DECAY_LORA + AAA_LORA + GATE_LORA
MOBA_BLOCK = 256
MOBA_TOPK = 3
CONV_W = 3
NORM_EPS = 1e-6
GN_EPS = 64e-5
ATTN_SCALE = HEAD_DIM ** -0.5

C_FQ, C_FK, C_FV = 0, 256, 512
C_MQ, C_MK, C_MV = 768, 1024, 1280
C_RR, C_RK, C_RV = 1536, 2048, 2560
C_LORA = 3072
C_FF = 3328
FF_PAD = 128
NP_COLS = C_FF + FF_PAD

RWKV_CHUNK = 64
LANES = 128
VMEM_LIMIT = 56 * 1024 * 1024


def _cparams(sem):
    return pltpu.CompilerParams(dimension_semantics=sem, vmem_limit_bytes=VMEM_LIMIT)


def _dot(a, b):
    return jnp.dot(a, b, preferred_element_type=F32)


def _dot_hi(a, b):
    return jnp.dot(a, b, precision=HI, preferred_element_type=F32)


def _dot_nt(a, b):
    return lax.dot_general(a, b, (((1,), (1,)), ((), ())), preferred_element_type=F32)


def _dot_nt_hi(a, b):
    return lax.dot_general(a, b, (((1,), (1,)), ((), ())), precision=HI, preferred_element_type=F32)


def _sigmoid(x):
    return 1.0 / (1.0 + jnp.exp(-x))


def _log_sigmoid(x):
    return jnp.minimum(x, 0.0) - jnp.log1p(jnp.exp(-jnp.abs(x)))


def _rmsnorm(x, w):
    ms = jnp.mean(x * x, axis=-1, keepdims=True)
    return x * lax.rsqrt(ms + NORM_EPS) * w


def _seg_cumsum_rows(x, seg):
    row = lax.broadcasted_iota(jnp.int32, x.shape, 0) & (seg - 1)
    s = 1
    while s < seg:
        x = x + jnp.where(row >= s, pltpu.roll(x, s, 0), 0.0)
        s *= 2
    return x


def _mod_kernel(c_ref, w_ref, b_ref, o_ref):
    c = c_ref[...]
    o_ref[0] = _dot_hi(c * _sigmoid(c), w_ref[0]) + b_ref[0]


def _mod_call(c, w_mod, b_mod):
    n_layers, d, n = w_mod.shape
    b = c.shape[0]
    tn = 1536
    return pl.pallas_call(
        _mod_kernel,
        grid=(n_layers, n // tn),
        in_specs=[
            pl.BlockSpec((b, d), lambda l, j: (0, 0)),
            pl.BlockSpec((1, d, tn), lambda l, j: (l, 0, j)),
            pl.BlockSpec((1, 1, tn), lambda l, j: (l, 0, j)),
        ],
        out_specs=pl.BlockSpec((1, b, tn), lambda l, j: (l, 0, j)),
        out_shape=jax.ShapeDtypeStruct((n_layers, b, n), F32),
        compiler_params=_cparams(("arbitrary", "arbitrary")),
        name="adaln_mod",
    )(c, w_mod, b_mod.reshape(n_layers, 1, n))


def _inproj_kernel(x_ref, mod_ref, nw_ref, w_ref, o_ref):
    m = mod_ref[0]
    h = _rmsnorm(x_ref[...], nw_ref[...]) * (1.0 + m[1:2, :]) + m[0:1, :]
    o_ref[...] = _dot(h.astype(BF16), w_ref[...])


def _inproj_call(xf, mod_l, norm_w, w_in_p, seq):
    rows, d = xf.shape
    tm = 512
    nt = seq // tm
    return pl.pallas_call(
        _inproj_kernel,
        grid=(rows // tm,),
        in_specs=[
            pl.BlockSpec((tm, d), lambda i: (i, 0)),
            pl.BlockSpec((1, 6, d), lambda i: (i // nt, 0, 0)),
            pl.BlockSpec((1, d), lambda i: (0, 0)),
            pl.BlockSpec((d, NP_COLS), lambda i: (0, 0)),
        ],
        out_specs=pl.BlockSpec((tm, NP_COLS), lambda i: (i, 0)),
        out_shape=jax.ShapeDtypeStruct((rows, NP_COLS), F32),
        compiler_params=_cparams(("arbitrary",)),
        name="in_proj",
    )(xf, mod_l, norm_w, w_in_p)


def _softmax_tile(s, m, l, acc, v):
    m_new = jnp.maximum(m, jnp.max(s, axis=1, keepdims=True))
    alpha = jnp.exp(m - m_new)
    p = jnp.exp(s - m_new)
    l_new = alpha * l + jnp.sum(p, axis=1, keepdims=True)
    acc_new = alpha * acc + _dot(p.astype(BF16), v)
    return m_new, l_new, acc_new


def _first_tile(s, v):
    m = jnp.max(s, axis=1, keepdims=True)
    p = jnp.exp(s - m)
    return m, jnp.sum(p, axis=1, keepdims=True), _dot(p.astype(BF16), v)


def _fox_kernel(q_ref, k_ref, v_ref, ff_ref, fb_ref, o_ref, kb, vb, fcol, frow, *, tq, n_kv):
    qi = pl.program_id(1)

    @pl.when(qi == 0)
    def _():
        kb[...] = k_ref[...].astype(BF16)
        vb[...] = v_ref[...].astype(BF16)
        f = _seg_cumsum_rows(_log_sigmoid(ff_ref[...] + fb_ref[...]), n_kv * tq)
        fcol[...] = f
        for j in range(n_kv):
            frow[j] = f[j * tq:(j + 1) * tq, :].T[0:8, :]

    q = (q_ref[...] * ATTN_SCALE).astype(BF16)
    q0 = pl.multiple_of(qi * tq, tq)
    fq = fcol[pl.ds(q0, tq), :]
    row = lax.broadcasted_iota(jnp.int32, (tq, tq), 0)
    col = lax.broadcasted_iota(jnp.int32, (tq, tq), 1)
    causal = col <= row

    def logits(j, h):
        k0 = pl.multiple_of(j * tq, tq)
        hs = slice(h * HEAD_DIM, (h + 1) * HEAD_DIM)
        s = _dot_nt(q[:, hs], kb[pl.ds(k0, tq), hs])
        s = s + (fq[:, h:h + 1] - frow[j][h:h + 1, :])
        return s, vb[pl.ds(k0, tq), hs]

    carry = []
    for h in range(FOX_HEADS):
        s, v = logits(qi, h)
        carry.extend(_first_tile(jnp.where(causal, s, -jnp.inf), v))

    def body(j, carry):
        out = []
        for h in range(FOX_HEADS):
            s, v = logits(j, h)
            out.extend(_softmax_tile(s, *carry[3 * h:3 * h + 3], v))
        return tuple(out)

    carry = lax.fori_loop(0, qi, body, tuple(carry))
    for h in range(FOX_HEADS):
        o_ref[:, h * HEAD_DIM:(h + 1) * HEAD_DIM] = carry[3 * h + 2] / carry[3 * h + 1]


def _fox_call(p, f_bias_row, batch, seq):
    tq = 256
    nq = seq // tq
    kern = functools.partial(_fox_kernel, tq=tq, n_kv=nq)
    return pl.pallas_call(
        kern,
        grid=(batch, nq),
        in_specs=[
            pl.BlockSpec((tq, FOX_W), lambda b, i: (b * nq + i, C_FQ // FOX_W)),
            pl.BlockSpec((seq, FOX_W), lambda b, i: (b, C_FK // FOX_W)),
            pl.BlockSpec((seq, FOX_W), lambda b, i: (b, C_FV // FOX_W)),
            pl.BlockSpec((seq, FF_PAD), lambda b, i: (b, C_FF // FF_PAD)),
            pl.BlockSpec((1, FF_PAD), lambda b, i: (0, 0)),
        ],
        out_specs=pl.BlockSpec((tq, FOX_W), lambda b, i: (b * nq + i, 0)),
        out_shape=jax.ShapeDtypeStruct((batch * seq, FOX_W), F32),
        scratch_shapes=[
            pltpu.VMEM((seq, FOX_W), BF16),
            pltpu.VMEM((seq, FOX_W), BF16),
            pltpu.VMEM((seq, FF_PAD), F32),
            pltpu.VMEM((nq, 8, tq), F32),
        ],
        compiler_params=_cparams(("arbitrary", "arbitrary")),
        name="fox_attention",
    )(p, p, p, p, f_bias_row)


def _moba_kernel(q_ref, k_ref, v_ref, o_ref, kb, vb, kmean, *, blk, n_kb):
    own = pl.program_id(1)

    @pl.when(own == 0)
    def _():
        kf = k_ref[...]
        kb[...] = kf.astype(BF16)
        vb[...] = v_ref[...].astype(BF16)
        kmean[...] = jnp.zeros(kmean.shape, F32)
        for n in range(n_kb):
            kmean[n:n + 1, :] = jnp.mean(kf[n * blk:(n + 1) * blk, :], axis=0, keepdims=True)

    qf = q_ref[...]
    q = (qf * ATTN_SCALE).astype(BF16)
    lane = lax.broadcasted_iota(jnp.int32, (blk, LANES), 1)
    past = lane < own
    row = lax.broadcasted_iota(jnp.int32, (blk, blk), 0)
    col = lax.broadcasted_iota(jnp.int32, (blk, blk), 1)
    causal = col <= row
    own0 = pl.multiple_of(own * blk, blk)

    sel = []
    carry = []
    for h in range(MOBA_HEADS):
        hs = slice(h * HEAD_DIM, (h + 1) * HEAD_DIM)
        gate = _dot_nt_hi(qf[:, hs], kmean[:, hs])
        selm = jnp.zeros((blk, LANES), F32)
        for n in range(n_kb):
            gn = gate[:, n:n + 1]
            beats = past & ((gate > gn) | ((gate == gn) & (lane < n)))
            rank = jnp.sum(beats.astype(F32), axis=1, keepdims=True)
            selm = jnp.where(lane == n, (rank < MOBA_TOPK).astype(F32), selm)
        sel.append(jnp.where(past, selm, 0.0))
        s = _dot_nt(q[:, hs], kb[pl.ds(own0, blk), hs])
        carry.extend(_first_tile(jnp.where(causal, s, -jnp.inf), vb[pl.ds(own0, blk), hs]))

    def body(j, carry):
        k0 = pl.multiple_of(j * blk, blk)
        out = []
        for h in range(MOBA_HEADS):
            hs = slice(h * HEAD_DIM, (h + 1) * HEAD_DIM)
            picked = jnp.sum(jnp.where(lane == j, sel[h], 0.0), axis=1, keepdims=True) > 0.5
            s = _dot_nt(q[:, hs], kb[pl.ds(k0, blk), hs])
            s = jnp.where(picked, s, -jnp.inf)
            out.extend(_softmax_tile(s, *carry[3 * h:3 * h + 3], vb[pl.ds(k0, blk), hs]))
        return tuple(out)

    carry = lax.fori_loop(0, own, body, tuple(carry))
    for h in range(MOBA_HEADS):
        o_ref[:, h * HEAD_DIM:(h + 1) * HEAD_DIM] = carry[3 * h + 2] / carry[3 * h + 1]


def _moba_call(p, batch, seq):
    blk = MOBA_BLOCK
    n_kb = seq // blk
    kern = functools.partial(_moba_kernel, blk=blk, n_kb=n_kb)
    return pl.pallas_call(
        kern,
        grid=(batch, n_kb),
        in_specs=[
            pl.BlockSpec((blk, MOBA_W), lambda b, i: (b * n_kb + i, C_MQ // MOBA_W)),
            pl.BlockSpec((seq, MOBA_W), lambda b, i: (b, C_MK // MOBA_W)),
            pl.BlockSpec((seq, MOBA_W), lambda b, i: (b, C_MV // MOBA_W)),
        ],
        out_specs=pl.BlockSpec((blk, MOBA_W), lambda b, i: (b * n_kb + i, 0)),
        out_shape=jax.ShapeDtypeStruct((batch * seq, MOBA_W), F32),
        scratch_shapes=[
            pltpu.VMEM((seq, MOBA_W), BF16),
            pltpu.VMEM((seq, MOBA_W), BF16),
            pltpu.VMEM((LANES, MOBA_W), F32),
        ],
        compiler_params=_cparams(("arbitrary", "arbitrary")),
        name="moba_attention",
    )(p, p, p)


def _rwkv_kernel(r_ref, k_ref, v_ref, lo_ref, mu_r, mu_k, mu_v, mu_l, w0, w2, a0, a2, g2, kkw, kaw, rkw,
                 lnw, lnb, o_ref, prev_r, prev_k, prev_v, prev_l, state, a_t, r_t, k_t, b_t, k_b, b_b, v_s,
                 w_c, y_s, *, tm):
    i = pl.program_id(1)
    ch = RWKV_CHUNK
    n_ch = tm // ch

    @pl.when(i == 0)
    def _():
        prev_r[...] = jnp.zeros(prev_r.shape, F32)
        prev_k[...] = jnp.zeros(prev_k.shape, F32)
        prev_v[...] = jnp.zeros(prev_v.shape, F32)
        prev_l[...] = jnp.zeros(prev_l.shape, F32)
        state[...] = jnp.zeros(state.shape, F32)

    def token_shift(x_ref, prev, mu):
        x = x_ref[...]
        first = lax.broadcasted_iota(jnp.int32, x.shape, 0) == 0
        shifted = jnp.where(first, prev[...], pltpu.roll(x, 1, 0))
        prev[...] = x[tm - 1:tm, :]
        return x + (shifted - x) * mu[...]

    r = token_shift(r_ref, prev_r, mu_r)
    k = token_shift(k_ref, prev_k, mu_k)
    v = token_shift(v_ref, prev_v, mu_v)
    lo = token_shift(lo_ref, prev_l, mu_l)
    w_lo = lo[:, 0:DECAY_LORA]
    a_lo = lo[:, DECAY_LORA:DECAY_LORA + AAA_LORA]
    g_lo = lo[:, DECAY_LORA + AAA_LORA:LORA_W]

    log_w = _log_sigmoid(w0[...] + _dot_hi(jnp.tanh(w_lo), w2[...])) - 0.5
    lw = -jnp.exp(log_w)
    eta = _sigmoid(a0[...] + _dot_hi(a_lo, a2[...]))
    gate = _dot(_sigmoid(g_lo).astype(BF16), g2[...])

    gi = lax.broadcasted_iota(jnp.int32, (RWKV_W, RWKV_W), 0) // HEAD_DIM
    gj = lax.broadcasted_iota(jnp.int32, (RWKV_W, RWKV_W), 1) // HEAD_DIM
    group = (gi == gj).astype(F32)

    kk = k * kkw[...]
    kk = kk / jnp.maximum(jnp.sqrt(_dot_hi(kk * kk, group)), 1e-12)
    kp = k * (1.0 + (eta - 1.0) * kaw[...])
    bb = kk * eta

    lc = _seg_cumsum_rows(lw, ch)
    a_t[...] = -kk * jnp.exp(lc - lw)
    r_t[...] = r * jnp.exp(lc)
    einv = jnp.exp(-lc)
    k_t[...] = kp * einv
    b_t[...] = bb * einv
    v_s[...] = v
    for c in range(n_ch):
        rows = slice(c * ch, (c + 1) * ch)
        last = lc[(c + 1) * ch - 1:(c + 1) * ch, :]
        e = jnp.exp(last - lc[rows, :])
        k_b[rows, :] = kp[rows, :] * e
        b_b[rows, :] = bb[rows, :] * e
        w_c[c] = jnp.broadcast_to(jnp.exp(last), (8, RWKV_W))

    ti = lax.broadcasted_iota(jnp.int32, (ch, ch), 0)
    tj = lax.broadcasted_iota(jnp.int32, (ch, ch), 1)
    strict = tj < ti
    incl = tj <= ti
    eye = (ti == tj).astype(F32)

    def chunk(c, _):
        c0 = pl.multiple_of(c * ch, ch)
        rows = pl.ds(c0, ch)
        for h in range(RWKV_HEADS):
            hs = slice(h * HEAD_DIM, (h + 1) * HEAD_DIM)
            at = a_t[rows, hs]
            rt = r_t[rows, hs]
            vv = v_s[rows, hs]
            kbar = k_b[rows, hs]
            bbar = b_b[rows, hs]
            m4 = _dot_nt_hi(jnp.concatenate([at, rt], axis=0),
                            jnp.concatenate([b_t[rows, hs], k_t[rows, hs]], axis=0))
            a_ab = jnp.where(strict, m4[0:ch, 0:ch], 0.0)
            a_ak = jnp.where(strict, m4[0:ch, ch:2 * ch], 0.0)
            a_rb = jnp.where(incl, m4[ch:2 * ch, 0:ch], 0.0)
            a_rk = jnp.where(incl, m4[ch:2 * ch, ch:2 * ch], 0.0)
            tinv = eye + a_ab
            pw = _dot_hi(a_ab, a_ab)
            span = 2
            while 2 * span < ch:
                x = _dot_hi(jnp.concatenate([pw, tinv], axis=0), pw)
                pw = x[0:ch, :]
                tinv = tinv + x[ch:2 * ch, :]
                span *= 2
            tinv = tinv + _dot_hi(tinv, pw)
            tx = _dot_hi(tinv, jnp.concatenate([at, _dot_hi(a_ak, vv)], axis=1))
            ry = _dot_hi(a_rb, tx)
            s0 = state[h]
            y = _dot_nt_hi(rt + ry[:, 0:ch], s0) + _dot_hi(a_rk, vv) + ry[:, ch:2 * ch]
            y_s[rows, hs] = y
            z = _dot_hi(tx[:, 0:ch].T, bbar)
            vt = jnp.concatenate([vv, tx[:, ch:2 * ch]], axis=0).T
            state[h] = (s0 * w_c[c][0:1, hs] + _dot_hi(s0, z)
                        + _dot_hi(vt, jnp.concatenate([kbar, bbar], axis=0)))
        return 0

    lax.fori_loop(0, n_ch, chunk, 0)

    y = y_s[...]
    inv_d = 1.0 / HEAD_DIM
    mean = _dot_hi(y, group) * inv_d
    d = y - mean
    var = _dot_hi(d * d, group) * inv_d
    yn = d * lax.rsqrt(var + GN_EPS) * lnw[...] + lnb[...]
    bonus = _dot_hi(r * kp * rkw[...], group) * v
    o_ref[...] = (yn + bonus) * gate


def _rwkv_call(p, prm, batch, seq):
    tm = 256
    nt = seq // tm
    kern = functools.partial(_rwkv_kernel, tm=tm)

    def rows(width, cstart):
        return pl.BlockSpec((tm, width), lambda b, i: (b * nt + i, cstart // width))

    def full(a):
        return pl.BlockSpec(a.shape, lambda b, i: (0,) * a.ndim)

    params = [prm[n] for n in ("mu_r", "mu_k", "mu_v", "mu_l", "w0", "w2", "a0", "a2", "g2", "k_k", "k_a",
                               "r_k", "ln_w", "ln_b")]
    big = pltpu.VMEM((tm, RWKV_W), F32)
    return pl.pallas_call(
        kern,
        grid=(batch, nt),
        in_specs=[rows(RWKV_W, C_RR), rows(RWKV_W, C_RK), rows(RWKV_W, C_RV), rows(LORA_W, C_LORA)]
        + [full(a) for a in params],
        out_specs=pl.BlockSpec((tm, RWKV_W), lambda b, i: (b * nt + i, 0)),
        out_shape=jax.ShapeDtypeStruct((batch * seq, RWKV_W), F32),
        scratch_shapes=[
            pltpu.VMEM((1, RWKV_W), F32), pltpu.VMEM((1, RWKV_W), F32), pltpu.VMEM((1, RWKV_W), F32),
            pltpu.VMEM((1, LORA_W), F32),
            pltpu.VMEM((RWKV_HEADS, HEAD_DIM, HEAD_DIM), F32),
            big, big, big, big, big, big, big,
            pltpu.VMEM((tm // RWKV_CHUNK, 8, RWKV_W), F32),
            big,
        ],
        compiler_params=_cparams(("arbitrary", "arbitrary")),
        name="rwkv7_mix",
    )(p, p, p, p, *params)


def _outproj_kernel(yf_ref, yr_ref, ym_ref, x_ref, mod_ref, w_ref, o_ref):
    z = _dot(yf_ref[...].astype(BF16), w_ref[0:FOX_W, :])
    z = z + _dot(yr_ref[...].astype(BF16), w_ref[FOX_W:FOX_W + RWKV_W, :])
    z = z + _dot(ym_ref[...].astype(BF16), w_ref[FOX_W + RWKV_W:, :])
    o_ref[...] = x_ref[...] + mod_ref[0][2:3, :] * z


def _outproj_call(yf, yr, ym, xf, mod_l, w_out_b, seq):
    rows, d = xf.shape
    tm = 512
    nt = seq // tm
    return pl.pallas_call(
        _outproj_kernel,
        grid=(rows // tm,),
        in_specs=[
            pl.BlockSpec((tm, FOX_W), lambda i: (i, 0)),
            pl.BlockSpec((tm, RWKV_W), lambda i: (i, 0)),
            pl.BlockSpec((tm, MOBA_W), lambda i: (i, 0)),
            pl.BlockSpec((tm, d), lambda i: (i, 0)),
            pl.BlockSpec((1, 6, d), lambda i: (i // nt, 0, 0)),
            pl.BlockSpec(w_out_b.shape, lambda i: (0, 0)),
        ],
        out_specs=pl.BlockSpec((tm, d), lambda i: (i, 0)),
        out_shape=jax.ShapeDtypeStruct((rows, d), F32),
        compiler_params=_cparams(("arbitrary",)),
        name="out_proj",
    )(yf, yr, ym, xf, mod_l, w_out_b)


FFN_HALO = 16


def _ffn_kernel(x_ref, xp_ref, mod_ref, nw_ref, wg_ref, wv_ref, cwg_ref, cwv_ref, cbg_ref, cbv_ref, wd_ref,
                nf_ref, o_ref, hext, acc, *, tm, nt, n_ff, final):
    i = pl.program_id(0)
    j = pl.program_id(1)
    m = mod_ref[0]

    @pl.when(j == 0)
    def _():
        def pre(x):
            return _rmsnorm(x, nw_ref[...]) * (1.0 + m[4:5, :]) + m[3:4, :]

        keep = jnp.where(i % nt == 0, 0.0, 1.0)
        hext[0:FFN_HALO, :] = (pre(xp_ref[...]) * keep).astype(BF16)
        hext[FFN_HALO:, :] = pre(x_ref[...]).astype(BF16)
        acc[...] = jnp.zeros(acc.shape, F32)

    he = hext[...]

    def conv(u, cw, cb):
        return (cb[...] + cw[0:1, :] * pltpu.roll(u, 2, 0)[FFN_HALO:, :]
                + cw[1:2, :] * pltpu.roll(u, 1, 0)[FFN_HALO:, :] + cw[2:3, :] * u[FFN_HALO:, :])

    ug = conv(_dot(he, wg_ref[...]), cwg_ref, cbg_ref)
    uv = conv(_dot(he, wv_ref[...]), cwv_ref, cbv_ref)
    act = ug * _sigmoid(ug) * uv
    acc[...] += _dot(act.astype(BF16), wd_ref[...])

    @pl.when(j == n_ff - 1)
    def _():
        out = x_ref[...] + m[5:6, :] * acc[...]
        if final:
            out = _rmsnorm(out, nf_ref[...])
        o_ref[...] = out


def _ffn_call(xf, mod_l, norm_w, w_up_b, conv_w, conv_b, w_down_b, norm_final, seq, final):
    rows, d = xf.shape
    d_ff = w_down_b.shape[0]
    tm = 512
    tf = 256
    nt = seq // tm
    n_ff = d_ff // tf
    hb = tm // FFN_HALO
    kern = functools.partial(_ffn_kernel, tm=tm, nt=nt, n_ff=n_ff, final=final)
    return pl.pallas_call(
        kern,
        grid=(rows // tm, n_ff),
        in_specs=[
            pl.BlockSpec((tm, d), lambda i, j: (i, 0)),
            pl.BlockSpec((FFN_HALO, d), lambda i, j: (jnp.maximum(i * hb - 1, 0), 0)),
            pl.BlockSpec((1, 6, d), lambda i, j: (i // nt, 0, 0)),
            pl.BlockSpec((1, d), lambda i, j: (0, 0)),
            pl.BlockSpec((d, tf), lambda i, j: (0, j)),
            pl.BlockSpec((d, tf), lambda i, j: (0, n_ff + j)),
            pl.BlockSpec((CONV_W, tf), lambda i, j: (0, j)),
            pl.BlockSpec((CONV_W, tf), lambda i, j: (0, n_ff + j)),
            pl.BlockSpec((1, tf), lambda i, j: (0, j)),
            pl.BlockSpec((1, tf), lambda i, j: (0, n_ff + j)),
            pl.BlockSpec((tf, d), lambda i, j: (j, 0)),
            pl.BlockSpec((1, d), lambda i, j: (0, 0)),
        ],
        out_specs=pl.BlockSpec((tm, d), lambda i, j: (i, 0)),
        out_shape=jax.ShapeDtypeStruct((rows, d), F32),
        scratch_shapes=[pltpu.VMEM((tm + FFN_HALO, d), BF16), pltpu.VMEM((tm, d), F32)],
        compiler_params=_cparams(("arbitrary", "arbitrary")),
        name="conv_ffn",
    )(xf, xf, mod_l, norm_w, w_up_b, w_up_b, conv_w, conv_w, conv_b, conv_b, w_down_b, norm_final)


def kernel(x, c, w_mod, b_mod, norm_mix, w_in, fox_f_bias, rwkv_mu, rwkv_w0, rwkv_w2, rwkv_a0, rwkv_a2,
           rwkv_g2, rwkv_k_k, rwkv_k_a, rwkv_r_k, rwkv_ln_w, rwkv_ln_b, w_out, norm_ffn, w_up, conv_w,
           conv_b, w_down, norm_final):
    batch, seq, d = x.shape
    n_layers = w_mod.shape[0]
    fox_cols = 3 * FOX_W + FOX_HEADS
    rwkv_cols = 3 * RWKV_W + LORA_W

    mod = _mod_call(c, w_mod, b_mod).reshape(n_layers, batch, 6, d)

    w_in_p = jnp.concatenate(
        [w_in[:, :, :3 * FOX_W], w_in[:, :, fox_cols + rwkv_cols:], w_in[:, :, fox_cols:fox_cols + rwkv_cols],
         w_in[:, :, 3 * FOX_W:fox_cols], jnp.zeros((n_layers, d, FF_PAD - FOX_HEADS), w_in.dtype)],
        axis=-1).astype(BF16)
    f_bias = jnp.pad(fox_f_bias, ((0, 0), (0, FF_PAD - FOX_HEADS)))
    w_out_b = w_out.astype(BF16)
    w_up_b = w_up.astype(BF16)
    w_down_b = w_down.astype(BF16)
    g2_b = rwkv_g2.astype(BF16)

    xf = x.reshape(batch * seq, d)
    for l in range(n_layers):
        row = lambda a: a[l].reshape(1, -1)
        mu = rwkv_mu[l]
        prm = {
            "mu_r": mu[0:RWKV_W].reshape(1, -1), "mu_k": mu[RWKV_W:2 * RWKV_W].reshape(1, -1),
            "mu_v": mu[2 * RWKV_W:3 * RWKV_W].reshape(1, -1), "mu_l": mu[3 * RWKV_W:].reshape(1, -1),
            "w0": row(rwkv_w0), "w2": rwkv_w2[l], "a0": row(rwkv_a0), "a2": rwkv_a2[l], "g2": g2_b[l],
            "k_k": row(rwkv_k_k), "k_a": row(rwkv_k_a), "r_k": row(rwkv_r_k), "ln_w": row(rwkv_ln_w),
            "ln_b": row(rwkv_ln_b),
        }
        p = _inproj_call(xf, mod[l], row(norm_mix), w_in_p[l], seq)
        y_fox = _fox_call(p, f_bias[l:l + 1], batch, seq)
        y_moba = _moba_call(p, batch, seq)
        y_rwkv = _rwkv_call(p, prm, batch, seq)
        xf = _outproj_call(y_fox, y_rwkv, y_moba, xf, mod[l], w_out_b[l], seq)
        xf = _ffn_call(xf, mod[l], row(norm_ffn), w_up_b[l], conv_w[l], conv_b[l].reshape(1, -1), w_down_b[l],
                       norm_final.reshape(1, -1), seq, final=(l == n_layers - 1))
    return xf.reshape(batch, seq, d)
```

```python
import functools

import jax
import jax.numpy as jnp
from jax import lax
from jax.experimental import pallas as pl
from jax.experimental.pallas import tpu as pltpu

F32 = jnp.float32
BF16 = jnp.bfloat16
HI = lax.Precision.HIGHEST

HEAD_DIM = 64
FOX_HEADS = 4
RWKV_HEADS = 8
MOBA_HEADS = 4
FOX_W = FOX_HEADS * HEAD_DIM
RWKV_W = RWKV_HEADS * HEAD_DIM
MOBA_W = MOBA_HEADS * HEAD_DIM
DECAY_LORA = 64
AAA_LORA = 64
GATE_LORA = 128
LORA_W = DECAY_LORA + AAA_LORA + GATE_LORA
MOBA_BLOCK = 256
MOBA_TOPK = 3
CONV_W = 3
NORM_EPS = 1e-6
GN_EPS = 64e-5
ATTN_SCALE = HEAD_DIM ** -0.5

C_FQ, C_FK, C_FV = 0, 256, 512
C_MQ, C_MK, C_MV = 768, 1024, 1280
C_RR, C_RK, C_RV = 1536, 2048, 2560
C_LORA = 3072
C_FF = 3328
FF_PAD = 128
NP_COLS = C_FF + FF_PAD

RWKV_CHUNK = 64
LANES = 128
VMEM_LIMIT = 56 * 1024 * 1024


def _cparams(sem):
    return pltpu.CompilerParams(dimension_semantics=sem, vmem_limit_bytes=VMEM_LIMIT)


def _dot(a, b):
    return jnp.dot(a, b, preferred_element_type=F32)


def _dot_hi(a, b):
    return jnp.dot(a, b, precision=HI, preferred_element_type=F32)


def _dot_nt(a, b):
    return lax.dot_general(a, b, (((1,), (1,)), ((), ())), preferred_element_type=F32)


def _dot_nt_hi(a, b):
    return lax.dot_general(a, b, (((1,), (1,)), ((), ())), precision=HI, preferred_element_type=F32)


def _mm(a, b):
    return jnp.dot(a.astype(BF16), b.astype(BF16), preferred_element_type=F32)


def _mm_nt(a, b):
    return _dot_nt(a.astype(BF16), b.astype(BF16))


def _split2(x):
    hi = x.astype(BF16)
    return hi, (x - hi.astype(F32)).astype(BF16)


def _dot_split_lhs(x, w_bf16):
    m = x.shape[0]
    r = _dot(jnp.concatenate(_split2(x), axis=0), w_bf16)
    return r[0:m, :] + r[m:2 * m, :]


def _dot3(a, b):
    m = a.shape[0]
    a_hi, a_lo = _split2(a)
    b_hi, b_lo = _split2(b)
    r = _dot(jnp.concatenate([a_hi, a_lo], axis=0), b_hi)
    return r[0:m, :] + r[m:2 * m, :] + _dot(a_hi, b_lo)


def _sigmoid(x):
    return 1.0 / (1.0 + jnp.exp(-x))


def _log_sigmoid(x):
    return jnp.minimum(x, 0.0) - jnp.log1p(jnp.exp(-jnp.abs(x)))


def _rmsnorm(x, w):
    ms = jnp.mean(x * x, axis=-1, keepdims=True)
    return x * lax.rsqrt(ms + NORM_EPS) * w


def _seg_cumsum_rows(x, seg):
    row = lax.broadcasted_iota(jnp.int32, x.shape, 0) & (seg - 1)
    s = 1
    while s < seg:
        x = x + jnp.where(row >= s, pltpu.roll(x, s, 0), 0.0)
        s *= 2
    return x


def _mod_kernel(c_ref, w_ref, b_ref, o_ref):
    c = c_ref[...]
    o_ref[0] = _dot_hi(c * _sigmoid(c), w_ref[0]) + b_ref[0]


def _mod_call(c, w_mod, b_mod):
    n_layers, d, n = w_mod.shape
    b = c.shape[0]
    tn = 1536
    return pl.pallas_call(
        _mod_kernel,
        grid=(n_layers, n // tn),
        in_specs=[
            pl.BlockSpec((b, d), lambda l, j: (0, 0)),
            pl.BlockSpec((1, d, tn), lambda l, j: (l, 0, j)),
            pl.BlockSpec((1, 1, tn), lambda l, j: (l, 0, j)),
        ],
        out_specs=pl.BlockSpec((1, b, tn), lambda l, j: (l, 0, j)),
        out_shape=jax.ShapeDtypeStruct((n_layers, b, n), F32),
        compiler_params=_cparams(("arbitrary", "arbitrary")),
        name="adaln_mod",
    )(c, w_mod, b_mod.reshape(n_layers, 1, n))


def _inproj_kernel(x_ref, mod_ref, nw_ref, w_ref, o_ref):
    m = mod_ref[0]
    h = _rmsnorm(x_ref[...], nw_ref[...]) * (1.0 + m[1:2, :]) + m[0:1, :]
    o_ref[...] = _dot(h.astype(BF16), w_ref[...])


def _inproj_call(xf, mod_l, norm_w, w_in_p, seq):
    rows, d = xf.shape
    tm = 512
    nt = seq // tm
    return pl.pallas_call(
        _inproj_kernel,
        grid=(rows // tm,),
        in_specs=[
            pl.BlockSpec((tm, d), lambda i: (i, 0)),
            pl.BlockSpec((1, 6, d), lambda i: (i // nt, 0, 0)),
            pl.BlockSpec((1, d), lambda i: (0, 0)),
            pl.BlockSpec((d, NP_COLS), lambda i: (0, 0)),
        ],
        out_specs=pl.BlockSpec((tm, NP_COLS), lambda i: (i, 0)),
        out_shape=jax.ShapeDtypeStruct((rows, NP_COLS), F32),
        compiler_params=_cparams(("arbitrary",)),
        name="in_proj",
    )(xf, mod_l, norm_w, w_in_p)


def _softmax_tile(s, m, l, acc, v):
    m_new = jnp.maximum(m, jnp.max(s, axis=1, keepdims=True))
    alpha = jnp.exp(m - m_new)
    p = jnp.exp(s - m_new)
    l_new = alpha * l + jnp.sum(p, axis=1, keepdims=True)
    acc_new = alpha * acc + _dot(p.astype(BF16), v)
    return m_new, l_new, acc_new


def _first_tile(s, v):
    m = jnp.max(s, axis=1, keepdims=True)
    p = jnp.exp(s - m)
    return m, jnp.sum(p, axis=1, keepdims=True), _dot(p.astype(BF16), v)


def _fox_kernel(q_ref, k_ref, v_ref, ff_ref, fb_ref, o_ref, kb, vb, fcol, frow, *, tq, n_kv):
    qi = pl.program_id(1)

    @pl.when(qi == 0)
    def _():
        kb[...] = k_ref[...].astype(BF16)
        vb[...] = v_ref[...].astype(BF16)
        f = _seg_cumsum_rows(_log_sigmoid(ff_ref[...] + fb_ref[...]), n_kv * tq)
        fcol[...] = f
        for j in range(n_kv):
            frow[j] = f[j * tq:(j + 1) * tq, :].T[0:8, :]

    q = (q_ref[...] * ATTN_SCALE).astype(BF16)
    q0 = pl.multiple_of(qi * tq, tq)
    fq = fcol[pl.ds(q0, tq), :]
    row = lax.broadcasted_iota(jnp.int32, (tq, tq), 0)
    col = lax.broadcasted_iota(jnp.int32, (tq, tq), 1)
    causal = col <= row

    def logits(j, h):
        k0 = pl.multiple_of(j * tq, tq)
        hs = slice(h * HEAD_DIM, (h + 1) * HEAD_DIM)
        s = _dot_nt(q[:, hs], kb[pl.ds(k0, tq), hs])
        s = s + (fq[:, h:h + 1] - frow[j][h:h + 1, :])
        return s, vb[pl.ds(k0, tq), hs]

    carry = []
    for h in range(FOX_HEADS):
        s, v = logits(qi, h)
        carry.extend(_first_tile(jnp.where(causal, s, -jnp.inf), v))

    def body(j, carry):
        out = []
        for h in range(FOX_HEADS):
            s, v = logits(j, h)
            out.extend(_softmax_tile(s, *carry[3 * h:3 * h + 3], v))
        return tuple(out)

    carry = lax.fori_loop(0, qi, body, tuple(carry))
    for h in range(FOX_HEADS):
        o_ref[:, h * HEAD_DIM:(h + 1) * HEAD_DIM] = carry[3 * h + 2] / carry[3 * h + 1]


def _fox_call(p, f_bias_row, batch, seq):
    tq = 256
    nq = seq // tq
    kern = functools.partial(_fox_kernel, tq=tq, n_kv=nq)
    return pl.pallas_call(
        kern,
        grid=(batch, nq),
        in_specs=[
            pl.BlockSpec((tq, FOX_W), lambda b, i: (b * nq + i, C_FQ // FOX_W)),
            pl.BlockSpec((seq, FOX_W), lambda b, i: (b, C_FK // FOX_W)),
            pl.BlockSpec((seq, FOX_W), lambda b, i: (b, C_FV // FOX_W)),
            pl.BlockSpec((seq, FF_PAD), lambda b, i: (b, C_FF // FF_PAD)),
            pl.BlockSpec((1, FF_PAD), lambda b, i: (0, 0)),
        ],
        out_specs=pl.BlockSpec((tq, FOX_W), lambda b, i: (b * nq + i, 0)),
        out_shape=jax.ShapeDtypeStruct((batch * seq, FOX_W), F32),
        scratch_shapes=[
            pltpu.VMEM((seq, FOX_W), BF16),
            pltpu.VMEM((seq, FOX_W), BF16),
            pltpu.VMEM((seq, FF_PAD), F32),
            pltpu.VMEM((nq, 8, tq), F32),
        ],
        compiler_params=_cparams(("arbitrary", "arbitrary")),
        name="fox_attention",
    )(p, p, p, p, f_bias_row)


def _moba_kernel(q_ref, k_ref, v_ref, o_ref, kb, vb, kmean, *, blk, n_kb):
    own = pl.program_id(1)

    @pl.when(own == 0)
    def _():
        kf = k_ref[...]
        kb[...] = kf.astype(BF16)
        vb[...] = v_ref[...].astype(BF16)
        kmean[...] = jnp.zeros(kmean.shape, F32)
        for n in range(n_kb):
            kmean[n:n + 1, :] = jnp.mean(kf[n * blk:(n + 1) * blk, :], axis=0, keepdims=True)

    qf = q_ref[...]
    q = (qf * ATTN_SCALE).astype(BF16)
    lane = lax.broadcasted_iota(jnp.int32, (blk, LANES), 1)
    past = lane < own
    row = lax.broadcasted_iota(jnp.int32, (blk, blk), 0)
    col = lax.broadcasted_iota(jnp.int32, (blk, blk), 1)
    causal = col <= row
    own0 = pl.multiple_of(own * blk, blk)

    sel = []
    carry = []
    for h in range(MOBA_HEADS):
        hs = slice(h * HEAD_DIM, (h + 1) * HEAD_DIM)
        gate = _dot_nt_hi(qf[:, hs], kmean[:, hs])
        selm = jnp.zeros((blk, LANES), F32)
        for n in range(n_kb):
            gn = gate[:, n:n + 1]
            beats = past & ((gate > gn) | ((gate == gn) & (lane < n)))
            rank = jnp.sum(beats.astype(F32), axis=1, keepdims=True)
            selm = jnp.where(lane == n, (rank < MOBA_TOPK).astype(F32), selm)
        sel.append(jnp.where(past, selm, 0.0))
        s = _dot_nt(q[:, hs], kb[pl.ds(own0, blk), hs])
        carry.extend(_first_tile(jnp.where(causal, s, -jnp.inf), vb[pl.ds(own0, blk), hs]))

    def body(j, carry):
        k0 = pl.multiple_of(j * blk, blk)
        out = []
        for h in range(MOBA_HEADS):
            hs = slice(h * HEAD_DIM, (h + 1) * HEAD_DIM)
            picked = jnp.sum(jnp.where(lane == j, sel[h], 0.0), axis=1, keepdims=True) > 0.5
            s = _dot_nt(q[:, hs], kb[pl.ds(k0, blk), hs])
            s = jnp.where(picked, s, -jnp.inf)
            out.extend(_softmax_tile(s, *carry[3 * h:3 * h + 3], vb[pl.ds(k0, blk), hs]))
        return tuple(out)

    carry = lax.fori_loop(0, own, body, tuple(carry))
    for h in range(MOBA_HEADS):
        o_ref[:, h * HEAD_DIM:(h + 1) * HEAD_DIM] = carry[3 * h + 2] / carry[3 * h + 1]


def _moba_call(p, batch, seq):
    blk = MOBA_BLOCK
    n_kb = seq // blk
    kern = functools.partial(_moba_kernel, blk=blk, n_kb=n_kb)
    return pl.pallas_call(
        kern,
        grid=(batch, n_kb),
        in_specs=[
            pl.BlockSpec((blk, MOBA_W), lambda b, i: (b * n_kb + i, C_MQ // MOBA_W)),
            pl.BlockSpec((seq, MOBA_W), lambda b, i: (b, C_MK // MOBA_W)),
            pl.BlockSpec((seq, MOBA_W), lambda b, i: (b, C_MV // MOBA_W)),
        ],
        out_specs=pl.BlockSpec((blk, MOBA_W), lambda b, i: (b * n_kb + i, 0)),
        out_shape=jax.ShapeDtypeStruct((batch * seq, MOBA_W), F32),
        scratch_shapes=[
            pltpu.VMEM((seq, MOBA_W), BF16),
            pltpu.VMEM((seq, MOBA_W), BF16),
            pltpu.VMEM((LANES, MOBA_W), F32),
        ],
        compiler_params=_cparams(("arbitrary", "arbitrary")),
        name="moba_attention",
    )(p, p, p)


def _rwkv_kernel(r_ref, k_ref, v_ref, lo_ref, mu_r, mu_k, mu_v, mu_l, w0, w2, a0, a2, g2, kkw, kaw, rkw,
                 lnw, lnb, o_ref, prev_r, prev_k, prev_v, prev_l, state, a_t, r_t, k_t, b_t, k_b, b_b, v_s,
                 w_c, y_s, *, tm):
    i = pl.program_id(1)
    ch = RWKV_CHUNK
    n_ch = tm // ch

    @pl.when(i == 0)
    def _():
        prev_r[...] = jnp.zeros(prev_r.shape, F32)
        prev_k[...] = jnp.zeros(prev_k.shape, F32)
        prev_v[...] = jnp.zeros(prev_v.shape, F32)
        prev_l[...] = jnp.zeros(prev_l.shape, F32)
        state[...] = jnp.zeros(state.shape, F32)

    def token_shift(x_ref, prev, mu):
        x = x_ref[...]
        first = lax.broadcasted_iota(jnp.int32, x.shape, 0) == 0
        shifted = jnp.where(first, prev[...], pltpu.roll(x, 1, 0))
        prev[...] = x[tm - 1:tm, :]
        return x + (shifted - x) * mu[...]

    r = token_shift(r_ref, prev_r, mu_r)
    k = token_shift(k_ref, prev_k, mu_k)
    v = token_shift(v_ref, prev_v, mu_v)
    lo = token_shift(lo_ref, prev_l, mu_l)
    w_lo = lo[:, 0:DECAY_LORA]
    a_lo = lo[:, DECAY_LORA:DECAY_LORA + AAA_LORA]
    g_lo = lo[:, DECAY_LORA + AAA_LORA:LORA_W]

    log_w = _log_sigmoid(w0[...] + _dot3(jnp.tanh(w_lo), w2[...])) - 0.5
    lw = -jnp.exp(log_w)
    eta = _sigmoid(a0[...] + _dot3(a_lo, a2[...]))
    gate = _dot(_sigmoid(g_lo).astype(BF16), g2[...])

    gi = lax.broadcasted_iota(jnp.int32, (RWKV_W, RWKV_W), 0) // HEAD_DIM
    gj = lax.broadcasted_iota(jnp.int32, (RWKV_W, RWKV_W), 1) // HEAD_DIM
    group = (gi == gj).astype(BF16)

    kk = k * kkw[...]
    kk = kk / jnp.maximum(jnp.sqrt(_dot_split_lhs(kk * kk, group)), 1e-12)
    kp = k * (1.0 + (eta - 1.0) * kaw[...])
    bb = kk * eta

    lc = _seg_cumsum_rows(lw, ch)
    a_t[...] = -kk * jnp.exp(lc - lw)
    r_t[...] = r * jnp.exp(lc)
    einv = jnp.exp(-lc)
    k_t[...] = kp * einv
    b_t[...] = bb * einv
    v_s[...] = v
    for c in range(n_ch):
        rows = slice(c * ch, (c + 1) * ch)
        last = lc[(c + 1) * ch - 1:(c + 1) * ch, :]
        e = jnp.exp(last - lc[rows, :])
        k_b[rows, :] = kp[rows, :] * e
        b_b[rows, :] = bb[rows, :] * e
        w_c[c] = jnp.broadcast_to(jnp.exp(last), (8, RWKV_W))

    ti = lax.broadcasted_iota(jnp.int32, (ch, ch), 0)
    tj = lax.broadcasted_iota(jnp.int32, (ch, ch), 1)
    strict = tj < ti
    incl = tj <= ti
    eye = (ti == tj).astype(F32)

    def chunk(c, _):
        c0 = pl.multiple_of(c * ch, ch)
        rows = pl.ds(c0, ch)
        loads = []
        for h in range(RWKV_HEADS):
            hs = slice(h * HEAD_DIM, (h + 1) * HEAD_DIM)
            loads.append((a_t[rows, hs], r_t[rows, hs], v_s[rows, hs], k_b[rows, hs], b_b[rows, hs],
                          b_t[rows, hs], k_t[rows, hs], state[h], w_c[c][0:1, hs]))
        hh = range(RWKV_HEADS)
        at, rt, vv, kbar, bbar, bt_, kt_, s0, wc = (list(t) for t in zip(*loads))
        m4 = [_mm_nt(jnp.concatenate([at[h], rt[h]], axis=0), jnp.concatenate([bt_[h], kt_[h]], axis=0))
              for h in hh]
        a_ab = [jnp.where(strict, m4[h][0:ch, 0:ch], 0.0) for h in hh]
        a_ak = [jnp.where(strict, m4[h][0:ch, ch:2 * ch], 0.0) for h in hh]
        a_rb = [jnp.where(incl, m4[h][ch:2 * ch, 0:ch], 0.0) for h in hh]
        a_rk = [jnp.where(incl, m4[h][ch:2 * ch, ch:2 * ch], 0.0) for h in hh]
        tinv = [eye + a_ab[h] for h in hh]
        pw = [_mm(a_ab[h], a_ab[h]) for h in hh]
        span = 2
        while 2 * span < ch:
            x = [_mm(jnp.concatenate([pw[h], tinv[h]], axis=0), pw[h]) for h in hh]
            pw = [x[h][0:ch, :] for h in hh]
            tinv = [tinv[h] + x[h][ch:2 * ch, :] for h in hh]
            span *= 2
        tinv = [tinv[h] + _mm(tinv[h], pw[h]) for h in hh]
        av = [_mm(a_ak[h], vv[h]) for h in hh]
        tx = [_mm(tinv[h], jnp.concatenate([at[h], av[h]], axis=1)) for h in hh]
        ry = [_mm(a_rb[h], tx[h]) for h in hh]
        yk = [_mm(a_rk[h], vv[h]) for h in hh]
        ys = [_mm_nt(rt[h] + ry[h][:, 0:ch], s0[h]) for h in hh]
        z = [_mm(tx[h][:, 0:ch].T, bbar[h]) for h in hh]
        sz = [_mm(s0[h], z[h]) for h in hh]
        kv = [_mm(jnp.concatenate([vv[h], tx[h][:, ch:2 * ch]], axis=0).T,
                  jnp.concatenate([kbar[h], bbar[h]], axis=0)) for h in hh]
        for h in hh:
            hs = slice(h * HEAD_DIM, (h + 1) * HEAD_DIM)
            y_s[rows, hs] = ys[h] + yk[h] + ry[h][:, ch:2 * ch]
            state[h] = s0[h] * wc[h] + sz[h] + kv[h]
        return 0

    lax.fori_loop(0, n_ch, chunk, 0)

    y = y_s[...]
    inv_d = 1.0 / HEAD_DIM
    mean = _dot_split_lhs(y, group) * inv_d
    d = y - mean
    var = _dot_split_lhs(d * d, group) * inv_d
    yn = d * lax.rsqrt(var + GN_EPS) * lnw[...] + lnb[...]
    bonus = _dot_split_lhs(r * kp * rkw[...], group) * v
    o_ref[...] = (yn + bonus) * gate


def _rwkv_call(p, prm, batch, seq):
    tm = 256
    nt = seq // tm
    kern = functools.partial(_rwkv_kernel, tm=tm)

    def rows(width, cstart):
        return pl.BlockSpec((tm, width), lambda b, i: (b * nt + i, cstart // width))

    def full(a):
        return pl.BlockSpec(a.shape, lambda b, i: (0,) * a.ndim)

    params = [prm[n] for n in ("mu_r", "mu_k", "mu_v", "mu_l", "w0", "w2", "a0", "a2", "g2", "k_k", "k_a",
                               "r_k", "ln_w", "ln_b")]
    big = pltpu.VMEM((tm, RWKV_W), F32)
    return pl.pallas_call(
        kern,
        grid=(batch, nt),
        in_specs=[rows(RWKV_W, C_RR), rows(RWKV_W, C_RK), rows(RWKV_W, C_RV), rows(LORA_W, C_LORA)]
        + [full(a) for a in params],
        out_specs=pl.BlockSpec((tm, RWKV_W), lambda b, i: (b * nt + i, 0)),
        out_shape=jax.ShapeDtypeStruct((batch * seq, RWKV_W), F32),
        scratch_shapes=[
            pltpu.VMEM((1, RWKV_W), F32), pltpu.VMEM((1, RWKV_W), F32), pltpu.VMEM((1, RWKV_W), F32),
            pltpu.VMEM((1, LORA_W), F32),
            pltpu.VMEM((RWKV_HEADS, HEAD_DIM, HEAD_DIM), F32),
            big, big, big, big, big, big, big,
            pltpu.VMEM((tm // RWKV_CHUNK, 8, RWKV_W), F32),
            big,
        ],
        compiler_params=_cparams(("arbitrary", "arbitrary")),
        name="rwkv7_mix",
    )(p, p, p, p, *params)


def _outproj_kernel(yf_ref, yr_ref, ym_ref, x_ref, mod_ref, w_ref, o_ref):
    z = _dot(yf_ref[...].astype(BF16), w_ref[0:FOX_W, :])
    z = z + _dot(yr_ref[...].astype(BF16), w_ref[FOX_W:FOX_W + RWKV_W, :])
    z = z + _dot(ym_ref[...].astype(BF16), w_ref[FOX_W + RWKV_W:, :])
    o_ref[...] = x_ref[...] + mod_ref[0][2:3, :] * z


def _outproj_call(yf, yr, ym, xf, mod_l, w_out_b, seq):
    rows, d = xf.shape
    tm = 512
    nt = seq // tm
    return pl.pallas_call(
        _outproj_kernel,
        grid=(rows // tm,),
        in_specs=[
            pl.BlockSpec((tm, FOX_W), lambda i: (i, 0)),
            pl.BlockSpec((tm, RWKV_W), lambda i: (i, 0)),
            pl.BlockSpec((tm, MOBA_W), lambda i: (i, 0)),
            pl.BlockSpec((tm, d), lambda i: (i, 0)),
            pl.BlockSpec((1, 6, d), lambda i: (i // nt, 0, 0)),
            pl.BlockSpec(w_out_b.shape, lambda i: (0, 0)),
        ],
        out_specs=pl.BlockSpec((tm, d), lambda i: (i, 0)),
        out_shape=jax.ShapeDtypeStruct((rows, d), F32),
        compiler_params=_cparams(("arbitrary",)),
        name="out_proj",
    )(yf, yr, ym, xf, mod_l, w_out_b)


FFN_HALO = 16


def _ffn_kernel(x_ref, xp_ref, mod_ref, nw_ref, wg_ref, wv_ref, cwg_ref, cwv_ref, cbg_ref, cbv_ref, wd_ref,
                nf_ref, o_ref, hext, acc, *, tm, nt, n_ff, final):
    i = pl.program_id(0)
    j = pl.program_id(1)
    m = mod_ref[0]

    @pl.when(j == 0)
    def _():
        def pre(x):
            return _rmsnorm(x, nw_ref[...]) * (1.0 + m[4:5, :]) + m[3:4, :]

        keep = jnp.where(i % nt == 0, 0.0, 1.0)
        hext[0:FFN_HALO, :] = (pre(xp_ref[...]) * keep).astype(BF16)
        hext[FFN_HALO:, :] = pre(x_ref[...]).astype(BF16)
        acc[...] = jnp.zeros(acc.shape, F32)

    he = hext[...]

    def conv(u, cw, cb):
        return (cb[...] + cw[0:1, :] * pltpu.roll(u, 2, 0)[FFN_HALO:, :]
                + cw[1:2, :] * pltpu.roll(u, 1, 0)[FFN_HALO:, :] + cw[2:3, :] * u[FFN_HALO:, :])

    ug = conv(_dot(he, wg_ref[...]), cwg_ref, cbg_ref)
    uv = conv(_dot(he, wv_ref[...]), cwv_ref, cbv_ref)
    act = ug * _sigmoid(ug) * uv
    acc[...] += _dot(act.astype(BF16), wd_ref[...])

    @pl.when(j == n_ff - 1)
    def _():
        out = x_ref[...] + m[5:6, :] * acc[...]
        if final:
            out = _rmsnorm(out, nf_ref[...])
        o_ref[...] = out


def _ffn_call(xf, mod_l, norm_w, w_up_b, conv_w, conv_b, w_down_b, norm_final, seq, final):
    rows, d = xf.shape
    d_ff = w_down_b.shape[0]
    tm = 512
    tf = 256
    nt = seq // tm
    n_ff = d_ff // tf
    hb = tm // FFN_HALO
    kern = functools.partial(_ffn_kernel, tm=tm, nt=nt, n_ff=n_ff, final=final)
    return pl.pallas_call(
        kern,
        grid=(rows // tm, n_ff),
        in_specs=[
            pl.BlockSpec((tm, d), lambda i, j: (i, 0)),
            pl.BlockSpec((FFN_HALO, d), lambda i, j: (jnp.maximum(i * hb - 1, 0), 0)),
            pl.BlockSpec((1, 6, d), lambda i, j: (i // nt, 0, 0)),
            pl.BlockSpec((1, d), lambda i, j: (0, 0)),
            pl.BlockSpec((d, tf), lambda i, j: (0, j)),
            pl.BlockSpec((d, tf), lambda i, j: (0, n_ff + j)),
            pl.BlockSpec((CONV_W, tf), lambda i, j: (0, j)),
            pl.BlockSpec((CONV_W, tf), lambda i, j: (0, n_ff + j)),
            pl.BlockSpec((1, tf), lambda i, j: (0, j)),
            pl.BlockSpec((1, tf), lambda i, j: (0, n_ff + j)),
            pl.BlockSpec((tf, d), lambda i, j: (j, 0)),
            pl.BlockSpec((1, d), lambda i, j: (0, 0)),
        ],
        out_specs=pl.BlockSpec((tm, d), lambda i, j: (i, 0)),
        out_shape=jax.ShapeDtypeStruct((rows, d), F32),
        scratch_shapes=[pltpu.VMEM((tm + FFN_HALO, d), BF16), pltpu.VMEM((tm, d), F32)],
        compiler_params=_cparams(("arbitrary", "arbitrary")),
        name="conv_ffn",
    )(xf, xf, mod_l, norm_w, w_up_b, w_up_b, conv_w, conv_w, conv_b, conv_b, w_down_b, norm_final)


def kernel(x, c, w_mod, b_mod, norm_mix, w_in, fox_f_bias, rwkv_mu, rwkv_w0, rwkv_w2, rwkv_a0, rwkv_a2,
           rwkv_g2, rwkv_k_k, rwkv_k_a, rwkv_r_k, rwkv_ln_w, rwkv_ln_b, w_out, norm_ffn, w_up, conv_w,
           conv_b, w_down, norm_final):
    batch, seq, d = x.shape
    n_layers = w_mod.shape[0]
    fox_cols = 3 * FOX_W + FOX_HEADS
    rwkv_cols = 3 * RWKV_W + LORA_W

    mod = _mod_call(c, w_mod, b_mod).reshape(n_layers, batch, 6, d)

    w_in_p = jnp.concatenate(
        [w_in[:, :, :3 * FOX_W], w_in[:, :, fox_cols + rwkv_cols:], w_in[:, :, fox_cols:fox_cols + rwkv_cols],
         w_in[:, :, 3 * FOX_W:fox_cols], jnp.zeros((n_layers, d, FF_PAD - FOX_HEADS), w_in.dtype)],
        axis=-1).astype(BF16)
    f_bias = jnp.pad(fox_f_bias, ((0, 0), (0, FF_PAD - FOX_HEADS)))
    w_out_b = w_out.astype(BF16)
    w_up_b = w_up.astype(BF16)
    w_down_b = w_down.astype(BF16)
    g2_b = rwkv_g2.astype(BF16)

    xf = x.reshape(batch * seq, d)
    for l in range(n_layers):
        row = lambda a: a[l].reshape(1, -1)
        mu = rwkv_mu[l]
        prm = {
            "mu_r": mu[0:RWKV_W].reshape(1, -1), "mu_k": mu[RWKV_W:2 * RWKV_W].reshape(1, -1),
            "mu_v": mu[2 * RWKV_W:3 * RWKV_W].reshape(1, -1), "mu_l": mu[3 * RWKV_W:].reshape(1, -1),
            "w0": row(rwkv_w0), "w2": rwkv_w2[l], "a0": row(rwkv_a0), "a2": rwkv_a2[l], "g2": g2_b[l],
            "k_k": row(rwkv_k_k), "k_a": row(rwkv_k_a), "r_k": row(rwkv_r_k), "ln_w": row(rwkv_ln_w),
            "ln_b": row(rwkv_ln_b),
        }
        p = _inproj_call(xf, mod[l], row(norm_mix), w_in_p[l], seq)
        y_fox = _fox_call(p, f_bias[l:l + 1], batch, seq)
        y_moba = _moba_call(p, batch, seq)
        y_rwkv = _rwkv_call(p, prm, batch, seq)
        xf = _outproj_call(y_fox, y_rwkv, y_moba, xf, mod[l], w_out_b[l], seq)
        xf = _ffn_call(xf, mod[l], row(norm_ffn), w_up_b[l], conv_w[l], conv_b[l].reshape(1, -1), w_down_b[l],
                       norm_final.reshape(1, -1), seq, final=(l == n_layers - 1))
    return xf.reshape(batch, seq, d)
```

```python
import functools

import jax
import jax.numpy as jnp
import numpy as np
from jax import lax
from jax.experimental import pallas as pl
from jax.experimental.pallas import tpu as pltpu

F32 = jnp.float32
BF16 = jnp.bfloat16
HI = lax.Precision.HIGHEST

HEAD_DIM = 64
FOX_HEADS = 4
RWKV_HEADS = 8
MOBA_HEADS = 4
FOX_W = FOX_HEADS * HEAD_DIM
RWKV_W = RWKV_HEADS * HEAD_DIM
MOBA_W = MOBA_HEADS * HEAD_DIM
DECAY_LORA = 64
AAA_LORA = 64
GATE_LORA = 128
LORA_W = DECAY_LORA + AAA_LORA + GATE_LORA
MOBA_BLOCK = 256
MOBA_TOPK = 3
CONV_W = 3
NORM_EPS = 1e-6
GN_EPS = 64e-5
ATTN_SCALE = HEAD_DIM ** -0.5

C_FQ, C_FK, C_FV = 0, 256, 512
C_MQ, C_MK, C_MV = 768, 1024, 1280
C_RR, C_RK, C_RV = 1536, 2048, 2560
C_LORA = 3072
C_FF = 3328
FF_PAD = 128
NP_COLS = C_FF + FF_PAD

RWKV_CHUNK = 64
LANES = 128
VMEM_LIMIT = 56 * 1024 * 1024


def _cparams(sem):
    return pltpu.CompilerParams(dimension_semantics=sem, vmem_limit_bytes=VMEM_LIMIT)


def _dot(a, b):
    return jnp.dot(a, b, preferred_element_type=F32)


def _dot_hi(a, b):
    return jnp.dot(a, b, precision=HI, preferred_element_type=F32)


def _dot_nt(a, b):
    return lax.dot_general(a, b, (((1,), (1,)), ((), ())), preferred_element_type=F32)


def _dot_nt_hi(a, b):
    return lax.dot_general(a, b, (((1,), (1,)), ((), ())), precision=HI, preferred_element_type=F32)


def _mm(a, b):
    return jnp.dot(a.astype(BF16), b.astype(BF16), preferred_element_type=F32)


def _mm_nt(a, b):
    return _dot_nt(a.astype(BF16), b.astype(BF16))


def _split2(x):
    hi = x.astype(BF16)
    return hi, (x - hi.astype(F32)).astype(BF16)


def _dot_split_lhs(x, w_bf16):
    m = x.shape[0]
    r = _dot(jnp.concatenate(_split2(x), axis=0), w_bf16)
    return r[0:m, :] + r[m:2 * m, :]


def _dot3(a, b):
    m = a.shape[0]
    a_hi, a_lo = _split2(a)
    b_hi, b_lo = _split2(b)
    r = _dot(jnp.concatenate([a_hi, a_lo], axis=0), b_hi)
    return r[0:m, :] + r[m:2 * m, :] + _dot(a_hi, b_lo)


def _sigmoid(x):
    return 1.0 / (1.0 + jnp.exp(-x))


def _log_sigmoid(x):
    return jnp.minimum(x, 0.0) - jnp.log1p(jnp.exp(-jnp.abs(x)))


def _rmsnorm(x, w):
    ms = jnp.mean(x * x, axis=-1, keepdims=True)
    return x * lax.rsqrt(ms + NORM_EPS) * w


def _seg_cumsum_rows(x, seg):
    row = lax.broadcasted_iota(jnp.int32, x.shape, 0) & (seg - 1)
    s = 1
    while s < seg:
        x = x + jnp.where(row >= s, pltpu.roll(x, s, 0), 0.0)
        s *= 2
    return x


def _mod_kernel(c_ref, w_ref, b_ref, o_ref):
    c = c_ref[...]
    o_ref[0] = _dot_hi(c * _sigmoid(c), w_ref[0]) + b_ref[0]


def _mod_call(c, w_mod, b_mod):
    n_layers, d, n = w_mod.shape
    b = c.shape[0]
    tn = 1536
    return pl.pallas_call(
        _mod_kernel,
        grid=(n_layers, n // tn),
        in_specs=[
            pl.BlockSpec((b, d), lambda l, j: (0, 0)),
            pl.BlockSpec((1, d, tn), lambda l, j: (l, 0, j)),
            pl.BlockSpec((1, 1, tn), lambda l, j: (l, 0, j)),
        ],
        out_specs=pl.BlockSpec((1, b, tn), lambda l, j: (l, 0, j)),
        out_shape=jax.ShapeDtypeStruct((n_layers, b, n), F32),
        compiler_params=_cparams(("arbitrary", "arbitrary")),
        name="adaln_mod",
    )(c, w_mod, b_mod.reshape(n_layers, 1, n))


def _inproj_kernel(x_ref, mod_ref, nw_ref, w_ref, o_ref):
    m = mod_ref[0]
    h = _rmsnorm(x_ref[...], nw_ref[...]) * (1.0 + m[1:2, :]) + m[0:1, :]
    o_ref[...] = _dot(h.astype(BF16), w_ref[...])


def _inproj_call(xf, mod_l, norm_w, w_in_p, seq):
    rows, d = xf.shape
    tm = 512
    nt = seq // tm
    return pl.pallas_call(
        _inproj_kernel,
        grid=(rows // tm,),
        in_specs=[
            pl.BlockSpec((tm, d), lambda i: (i, 0)),
            pl.BlockSpec((1, 6, d), lambda i: (i // nt, 0, 0)),
            pl.BlockSpec((1, d), lambda i: (0, 0)),
            pl.BlockSpec((d, NP_COLS), lambda i: (0, 0)),
        ],
        out_specs=pl.BlockSpec((tm, NP_COLS), lambda i: (i, 0)),
        out_shape=jax.ShapeDtypeStruct((rows, NP_COLS), F32),
        compiler_params=_cparams(("arbitrary",)),
        name="in_proj",
    )(xf, mod_l, norm_w, w_in_p)


def _softmax_tiles(s, carry, v_t):
    m_new = [jnp.maximum(c[0], jnp.max(x, axis=0, keepdims=True)) for x, c in zip(s, carry)]
    p = [jnp.exp(x - m) for x, m in zip(s, m_new)]
    pv = [_dot(v, x.astype(BF16)) for v, x in zip(v_t, p)]
    out = []
    for (m, l, acc), mn, x, y in zip(carry, m_new, p, pv):
        alpha = jnp.exp(m - mn)
        out.append((mn, alpha * l + jnp.sum(x, axis=0, keepdims=True), alpha * acc + y))
    return out


def _first_tiles(s, v_t):
    m = [jnp.max(x, axis=0, keepdims=True) for x in s]
    p = [jnp.exp(x - mm) for x, mm in zip(s, m)]
    pv = [_dot(v, x.astype(BF16)) for v, x in zip(v_t, p)]
    return [(mm, jnp.sum(x, axis=0, keepdims=True), y) for mm, x, y in zip(m, p, pv)]


def _flatten(carry):
    return tuple(a for c in carry for a in c)


def _unflatten(flat):
    return [tuple(flat[3 * h:3 * h + 3]) for h in range(len(flat) // 3)]


def _store_heads(o_ref, carry):
    o_ref[...] = jnp.concatenate([acc / l for _, l, acc in carry], axis=0).T


F_COL = HEAD_DIM
N_SPLIT = 3
FOX_XW = FOX_W + N_SPLIT * FF_PAD


def _fox_selectors():
    selk = np.zeros((FOX_HEADS, FOX_XW, LANES), np.float32)
    selq = np.zeros((FOX_HEADS, LANES, FOX_XW), np.float32)
    for h in range(FOX_HEADS):
        for c in range(HEAD_DIM):
            selk[h, h * HEAD_DIM + c, c] = 1.0
            selq[h, c, h * HEAD_DIM + c] = 1.0
        for s in range(N_SPLIT):
            selk[h, FOX_W + s * FF_PAD + h, F_COL + s] = 1.0
            selq[h, F_COL + N_SPLIT + s, FOX_W + s * FF_PAD + h] = 1.0
    return jnp.asarray(selk, BF16), jnp.asarray(selq, BF16)


def _fox_kernel(q_ref, k_ref, v_ref, ff_ref, fb_ref, selk_ref, selq_ref, o_ref, ka, v_t, f3, *, tq, n_kv):
    qi = pl.program_id(1)
    heads = range(FOX_HEADS)

    @pl.when(qi == 0)
    def _():
        f = _seg_cumsum_rows(_log_sigmoid(ff_ref[...] + fb_ref[...]), n_kv * tq)
        hi = f.astype(BF16)
        rest = f - hi.astype(F32)
        mid = rest.astype(BF16)
        lo = (rest - mid.astype(F32)).astype(BF16)
        f3[...] = jnp.concatenate([hi, mid, lo], axis=1)
        x = jnp.concatenate([k_ref[...].astype(BF16), f3[...]], axis=1)
        lane = lax.broadcasted_iota(jnp.int32, (1, LANES), 1)
        ones = jnp.where((lane >= F_COL + N_SPLIT) & (lane < F_COL + 2 * N_SPLIT), 1.0, 0.0)
        for h in heads:
            ka[h] = (_dot(x, selk_ref[h]) + ones).astype(BF16)
        for j in range(n_kv):
            v_t[j] = v_ref[j * tq:(j + 1) * tq, :].T.astype(BF16)

    q0 = pl.multiple_of(qi * tq, tq)
    xq = jnp.concatenate([(q_ref[...] * ATTN_SCALE).astype(BF16), f3[pl.ds(q0, tq), :]], axis=1)
    sub = lax.broadcasted_iota(jnp.int32, (LANES, tq), 0)
    neg = jnp.where((sub >= F_COL) & (sub < F_COL + N_SPLIT), -1.0, 0.0)
    q_t = [(_dot_nt(selq_ref[h], xq) + neg).astype(BF16) for h in heads]
    causal = (lax.broadcasted_iota(jnp.int32, (tq, tq), 0) <= lax.broadcasted_iota(jnp.int32, (tq, tq), 1))

    def tiles(j):
        k0 = pl.multiple_of(j * tq, tq)
        s = [_dot(ka[h, pl.ds(k0, tq), :], q_t[h]) for h in heads]
        vj = v_t[j]
        return s, [vj[h * HEAD_DIM:(h + 1) * HEAD_DIM, :] for h in heads]

    s, vv = tiles(qi)
    carry = _first_tiles([jnp.where(causal, x, -jnp.inf) for x in s], vv)

    def body(j, flat):
        s, vv = tiles(j)
        return _flatten(_softmax_tiles(s, _unflatten(flat), vv))

    _store_heads(o_ref, _unflatten(lax.fori_loop(0, qi, body, _flatten(carry))))


def _fox_call(p, f_bias_row, batch, seq):
    tq = 256
    nq = seq // tq
    selk, selq = _fox_selectors()
    kern = functools.partial(_fox_kernel, tq=tq, n_kv=nq)
    return pl.pallas_call(
        kern,
        grid=(batch, nq),
        in_specs=[
            pl.BlockSpec((tq, FOX_W), lambda b, i: (b * nq + i, C_FQ // FOX_W)),
            pl.BlockSpec((seq, FOX_W), lambda b, i: (b, C_FK // FOX_W)),
            pl.BlockSpec((seq, FOX_W), lambda b, i: (b, C_FV // FOX_W)),
            pl.BlockSpec((seq, FF_PAD), lambda b, i: (b, C_FF // FF_PAD)),
            pl.BlockSpec((1, FF_PAD), lambda b, i: (0, 0)),
            pl.BlockSpec(selk.shape, lambda b, i: (0, 0, 0)),
            pl.BlockSpec(selq.shape, lambda b, i: (0, 0, 0)),
        ],
        out_specs=pl.BlockSpec((tq, FOX_W), lambda b, i: (b * nq + i, 0)),
        out_shape=jax.ShapeDtypeStruct((batch * seq, FOX_W), F32),
        scratch_shapes=[
            pltpu.VMEM((FOX_HEADS, seq, LANES), BF16),
            pltpu.VMEM((nq, FOX_W, tq), BF16),
            pltpu.VMEM((seq, N_SPLIT * FF_PAD), BF16),
        ],
        compiler_params=_cparams(("arbitrary", "arbitrary")),
        name="fox_attention",
    )(p, p, p, p, f_bias_row, selk, selq)


SUBLANES = 8


def _moba_selectors():
    selk = np.zeros((MOBA_HEADS, MOBA_W, LANES), np.float32)
    for h in range(MOBA_HEADS):
        for c in range(HEAD_DIM):
            selk[h, h * HEAD_DIM + c, c] = 1.0
    return jnp.asarray(selk, BF16), jnp.asarray(selk.transpose(0, 2, 1), BF16)


def _moba_kernel(q_ref, k_ref, v_ref, selk_ref, selq_ref, o_ref, ka, v_t, kmean, *, blk, n_kb):
    own = pl.program_id(1)
    heads = range(MOBA_HEADS)

    @pl.when(own == 0)
    def _():
        kf = k_ref[...]
        kbf = kf.astype(BF16)
        for h in heads:
            ka[h] = _dot(kbf, selk_ref[h]).astype(BF16)
        kmean[...] = jnp.zeros(kmean.shape, F32)
        for n in range(n_kb):
            kmean[n:n + 1, :] = jnp.mean(kf[n * blk:(n + 1) * blk, :], axis=0, keepdims=True)
            v_t[n] = v_ref[n * blk:(n + 1) * blk, :].T.astype(BF16)

    qf = q_ref[...]
    qs = (qf * ATTN_SCALE).astype(BF16)
    q_t = [_dot_nt(selq_ref[h], qs).astype(BF16) for h in heads]
    sub = lax.broadcasted_iota(jnp.int32, (SUBLANES, blk), 0)
    past = sub < own
    causal = (lax.broadcasted_iota(jnp.int32, (blk, blk), 0) <= lax.broadcasted_iota(jnp.int32, (blk, blk), 1))

    sel = []
    for h in heads:
        hs = slice(h * HEAD_DIM, (h + 1) * HEAD_DIM)
        gate = _dot_nt_hi(kmean[:, hs], qf[:, hs])
        selm = jnp.zeros((SUBLANES, blk), F32)
        for n in range(n_kb):
            gn = gate[n:n + 1, :]
            beats = past & ((gate > gn) | ((gate == gn) & (sub < n)))
            rank = jnp.sum(beats.astype(F32), axis=0, keepdims=True)
            selm = jnp.where(sub == n, (rank < MOBA_TOPK).astype(F32), selm)
        sel.append(jnp.where(past, selm, 0.0))

    def tiles(j):
        k0 = pl.multiple_of(j * blk, blk)
        s = [_dot(ka[h, pl.ds(k0, blk), :], q_t[h]) for h in heads]
        vj = v_t[j]
        return s, [vj[h * HEAD_DIM:(h + 1) * HEAD_DIM, :] for h in heads]

    s, vv = tiles(own)
    carry = _first_tiles([jnp.where(causal, x, -jnp.inf) for x in s], vv)

    def body(j, flat):
        s, vv = tiles(j)
        picked = [jnp.sum(jnp.where(sub == j, sel[h], 0.0), axis=0, keepdims=True) > 0.5 for h in heads]
        s = [jnp.where(pk, x, -jnp.inf) for pk, x in zip(picked, s)]
        return _flatten(_softmax_tiles(s, _unflatten(flat), vv))

    _store_heads(o_ref, _unflatten(lax.fori_loop(0, own, body, _flatten(carry))))


def _moba_call(p, batch, seq):
    blk = MOBA_BLOCK
    n_kb = seq // blk
    assert n_kb <= SUBLANES
    selk, selq = _moba_selectors()
    kern = functools.partial(_moba_kernel, blk=blk, n_kb=n_kb)
    return pl.pallas_call(
        kern,
        grid=(batch, n_kb),
        in_specs=[
            pl.BlockSpec((blk, MOBA_W), lambda b, i: (b * n_kb + i, C_MQ // MOBA_W)),
            pl.BlockSpec((seq, MOBA_W), lambda b, i: (b, C_MK // MOBA_W)),
            pl.BlockSpec((seq, MOBA_W), lambda b, i: (b, C_MV // MOBA_W)),
            pl.BlockSpec(selk.shape, lambda b, i: (0, 0, 0)),
            pl.BlockSpec(selq.shape, lambda b, i: (0, 0, 0)),
        ],
        out_specs=pl.BlockSpec((blk, MOBA_W), lambda b, i: (b * n_kb + i, 0)),
        out_shape=jax.ShapeDtypeStruct((batch * seq, MOBA_W), F32),
        scratch_shapes=[
            pltpu.VMEM((MOBA_HEADS, seq, LANES), BF16),
            pltpu.VMEM((n_kb, MOBA_W, blk), BF16),
            pltpu.VMEM((SUBLANES, MOBA_W), F32),
        ],
        compiler_params=_cparams(("arbitrary", "arbitrary")),
        name="moba_attention",
    )(p, p, p, selk, selq)


def _rwkv_kernel(r_ref, k_ref, v_ref, lo_ref, mu_r, mu_k, mu_v, mu_l, w0, w2, a0, a2, g2, kkw, kaw, rkw,
                 lnw, lnb, o_ref, prev_r, prev_k, prev_v, prev_l, state, a_t, r_t, k_t, b_t, k_b, b_b, v_s,
                 w_c, y_s, *, tm):
    i = pl.program_id(1)
    ch = RWKV_CHUNK
    n_ch = tm // ch

    @pl.when(i == 0)
    def _():
        prev_r[...] = jnp.zeros(prev_r.shape, F32)
        prev_k[...] = jnp.zeros(prev_k.shape, F32)
        prev_v[...] = jnp.zeros(prev_v.shape, F32)
        prev_l[...] = jnp.zeros(prev_l.shape, F32)
        state[...] = jnp.zeros(state.shape, F32)

    def token_shift(x_ref, prev, mu):
        x = x_ref[...]
        first = lax.broadcasted_iota(jnp.int32, x.shape, 0) == 0
        shifted = jnp.where(first, prev[...], pltpu.roll(x, 1, 0))
        prev[...] = x[tm - 1:tm, :]
        return x + (shifted - x) * mu[...]

    r = token_shift(r_ref, prev_r, mu_r)
    k = token_shift(k_ref, prev_k, mu_k)
    v = token_shift(v_ref, prev_v, mu_v)
    lo = token_shift(lo_ref, prev_l, mu_l)
    w_lo = lo[:, 0:DECAY_LORA]
    a_lo = lo[:, DECAY_LORA:DECAY_LORA + AAA_LORA]
    g_lo = lo[:, DECAY_LORA + AAA_LORA:LORA_W]

    log_w = _log_sigmoid(w0[...] + _dot3(jnp.tanh(w_lo), w2[...])) - 0.5
    lw = -jnp.exp(log_w)
    eta = _sigmoid(a0[...] + _dot3(a_lo, a2[...]))
    gate = _dot(_sigmoid(g_lo).astype(BF16), g2[...])

    gi = lax.broadcasted_iota(jnp.int32, (RWKV_W, RWKV_W), 0) // HEAD_DIM
    gj = lax.broadcasted_iota(jnp.int32, (RWKV_W, RWKV_W), 1) // HEAD_DIM
    group = (gi == gj).astype(BF16)

    kk = k * kkw[...]
    kk = kk / jnp.maximum(jnp.sqrt(_dot_split_lhs(kk * kk, group)), 1e-12)
    kp = k * (1.0 + (eta - 1.0) * kaw[...])
    bb = kk * eta

    lc = _seg_cumsum_rows(lw, ch)
    a_t[...] = -kk * jnp.exp(lc - lw)
    r_t[...] = r * jnp.exp(lc)
    einv = jnp.exp(-lc)
    k_t[...] = kp * einv
    b_t[...] = bb * einv
    v_s[...] = v
    for c in range(n_ch):
        rows = slice(c * ch, (c + 1) * ch)
        last = lc[(c + 1) * ch - 1:(c + 1) * ch, :]
        e = jnp.exp(last - lc[rows, :])
        k_b[rows, :] = kp[rows, :] * e
        b_b[rows, :] = bb[rows, :] * e
        w_c[c] = jnp.broadcast_to(jnp.exp(last), (8, RWKV_W))

    ti = lax.broadcasted_iota(jnp.int32, (ch, ch), 0)
    tj = lax.broadcasted_iota(jnp.int32, (ch, ch), 1)
    strict = tj < ti
    incl = tj <= ti
    eye = (ti == tj).astype(F32)

    def chunk(c, _):
        c0 = pl.multiple_of(c * ch, ch)
        rows = pl.ds(c0, ch)
        loads = []
        for h in range(RWKV_HEADS):
            hs = slice(h * HEAD_DIM, (h + 1) * HEAD_DIM)
            loads.append((a_t[rows, hs], r_t[rows, hs], v_s[rows, hs], k_b[rows, hs], b_b[rows, hs],
                          b_t[rows, hs], k_t[rows, hs], state[h], w_c[c][0:1, hs]))
        hh = range(RWKV_HEADS)
        at, rt, vv, kbar, bbar, bt_, kt_, s0, wc = (list(t) for t in zip(*loads))
        m4 = [_mm_nt(jnp.concatenate([at[h], rt[h]], axis=0), jnp.concatenate([bt_[h], kt_[h]], axis=0))
              for h in hh]
        a_ab = [jnp.where(strict, m4[h][0:ch, 0:ch], 0.0) for h in hh]
        a_ak = [jnp.where(strict, m4[h][0:ch, ch:2 * ch], 0.0) for h in hh]
        a_rb = [jnp.where(incl, m4[h][ch:2 * ch, 0:ch], 0.0) for h in hh]
        a_rk = [jnp.where(incl, m4[h][ch:2 * ch, ch:2 * ch], 0.0) for h in hh]
        tinv = [eye + a_ab[h] for h in hh]
        pw = [_mm(a_ab[h], a_ab[h]) for h in hh]
        span = 2
        while 2 * span < ch:
            x = [_mm(jnp.concatenate([pw[h], tinv[h]], axis=0), pw[h]) for h in hh]
            pw = [x[h][0:ch, :] for h in hh]
            tinv = [tinv[h] + x[h][ch:2 * ch, :] for h in hh]
            span *= 2
        tinv = [tinv[h] + _mm(tinv[h], pw[h]) for h in hh]
        av = [_mm(a_ak[h], vv[h]) for h in hh]
        tx = [_mm(tinv[h], jnp.concatenate([at[h], av[h]], axis=1)) for h in hh]
        ry = [_mm(a_rb[h], tx[h]) for h in hh]
        yk = [_mm(a_rk[h], vv[h]) for h in hh]
        ys = [_mm_nt(rt[h] + ry[h][:, 0:ch], s0[h]) for h in hh]
        z = [_mm(tx[h][:, 0:ch].T, bbar[h]) for h in hh]
        sz = [_mm(s0[h], z[h]) for h in hh]
        kv = [_mm(jnp.concatenate([vv[h], tx[h][:, ch:2 * ch]], axis=0).T,
                  jnp.concatenate([kbar[h], bbar[h]], axis=0)) for h in hh]
        for h in hh:
            hs = slice(h * HEAD_DIM, (h + 1) * HEAD_DIM)
            y_s[rows, hs] = ys[h] + yk[h] + ry[h][:, ch:2 * ch]
            state[h] = s0[h] * wc[h] + sz[h] + kv[h]
        return 0

    lax.fori_loop(0, n_ch, chunk, 0)

    y = y_s[...]
    inv_d = 1.0 / HEAD_DIM
    mean = _dot_split_lhs(y, group) * inv_d
    d = y - mean
    var = _dot_split_lhs(d * d, group) * inv_d
    yn = d * lax.rsqrt(var + GN_EPS) * lnw[...] + lnb[...]
    bonus = _dot_split_lhs(r * kp * rkw[...], group) * v
    o_ref[...] = (yn + bonus) * gate


def _rwkv_call(p, prm, batch, seq):
    tm = 256
    nt = seq // tm
    kern = functools.partial(_rwkv_kernel, tm=tm)

    def rows(width, cstart):
        return pl.BlockSpec((tm, width), lambda b, i: (b * nt + i, cstart // width))

    def full(a):
        return pl.BlockSpec(a.shape, lambda b, i: (0,) * a.ndim)

    params = [prm[n] for n in ("mu_r", "mu_k", "mu_v", "mu_l", "w0", "w2", "a0", "a2", "g2", "k_k", "k_a",
                               "r_k", "ln_w", "ln_b")]
    big = pltpu.VMEM((tm, RWKV_W), F32)
    return pl.pallas_call(
        kern,
        grid=(batch, nt),
        in_specs=[rows(RWKV_W, C_RR), rows(RWKV_W, C_RK), rows(RWKV_W, C_RV), rows(LORA_W, C_LORA)]
        + [full(a) for a in params],
        out_specs=pl.BlockSpec((tm, RWKV_W), lambda b, i: (b * nt + i, 0)),
        out_shape=jax.ShapeDtypeStruct((batch * seq, RWKV_W), F32),
        scratch_shapes=[
            pltpu.VMEM((1, RWKV_W), F32), pltpu.VMEM((1, RWKV_W), F32), pltpu.VMEM((1, RWKV_W), F32),
            pltpu.VMEM((1, LORA_W), F32),
            pltpu.VMEM((RWKV_HEADS, HEAD_DIM, HEAD_DIM), F32),
            big, big, big, big, big, big, big,
            pltpu.VMEM((tm // RWKV_CHUNK, 8, RWKV_W), F32),
            big,
        ],
        compiler_params=_cparams(("arbitrary", "arbitrary")),
        name="rwkv7_mix",
    )(p, p, p, p, *params)


def _outproj_kernel(yf_ref, yr_ref, ym_ref, x_ref, mod_ref, w_ref, o_ref):
    z = _dot(yf_ref[...].astype(BF16), w_ref[0:FOX_W, :])
    z = z + _dot(yr_ref[...].astype(BF16), w_ref[FOX_W:FOX_W + RWKV_W, :])
    z = z + _dot(ym_ref[...].astype(BF16), w_ref[FOX_W + RWKV_W:, :])
    o_ref[...] = x_ref[...] + mod_ref[0][2:3, :] * z


def _outproj_call(yf, yr, ym, xf, mod_l, w_out_b, seq):
    rows, d = xf.shape
    tm = 512
    nt = seq // tm
    return pl.pallas_call(
        _outproj_kernel,
        grid=(rows // tm,),
        in_specs=[
            pl.BlockSpec((tm, FOX_W), lambda i: (i, 0)),
            pl.BlockSpec((tm, RWKV_W), lambda i: (i, 0)),
            pl.BlockSpec((tm, MOBA_W), lambda i: (i, 0)),
            pl.BlockSpec((tm, d), lambda i: (i, 0)),
            pl.BlockSpec((1, 6, d), lambda i: (i // nt, 0, 0)),
            pl.BlockSpec(w_out_b.shape, lambda i: (0, 0)),
        ],
        out_specs=pl.BlockSpec((tm, d), lambda i: (i, 0)),
        out_shape=jax.ShapeDtypeStruct((rows, d), F32),
        compiler_params=_cparams(("arbitrary",)),
        name="out_proj",
    )(yf, yr, ym, xf, mod_l, w_out_b)


FFN_HALO = 16


def _ffn_kernel(x_ref, xp_ref, mod_ref, nw_ref, wg_ref, wv_ref, cwg_ref, cwv_ref, cbg_ref, cbv_ref, wd_ref,
                nf_ref, o_ref, hext, acc, *, tm, nt, n_ff, final):
    i = pl.program_id(0)
    j = pl.program_id(1)
    m = mod_ref[0]

    @pl.when(j == 0)
    def _():
        def pre(x):
            return _rmsnorm(x, nw_ref[...]) * (1.0 + m[4:5, :]) + m[3:4, :]

        keep = jnp.where(i % nt == 0, 0.0, 1.0)
        hext[0:FFN_HALO, :] = (pre(xp_ref[...]) * keep).astype(BF16)
        hext[FFN_HALO:, :] = pre(x_ref[...]).astype(BF16)
        acc[...] = jnp.zeros(acc.shape, F32)

    he = hext[...]

    def conv(u, cw, cb):
        return (cb[...] + cw[0:1, :] * pltpu.roll(u, 2, 0)[FFN_HALO:, :]
                + cw[1:2, :] * pltpu.roll(u, 1, 0)[FFN_HALO:, :] + cw[2:3, :] * u[FFN_HALO:, :])

    ug = conv(_dot(he, wg_ref[...]), cwg_ref, cbg_ref)
    uv = conv(_dot(he, wv_ref[...]), cwv_ref, cbv_ref)
    act = ug * _sigmoid(ug) * uv
    acc[...] += _dot(act.astype(BF16), wd_ref[...])

    @pl.when(j == n_ff - 1)
    def _():
        out = x_ref[...] + m[5:6, :] * acc[...]
        if final:
            out = _rmsnorm(out, nf_ref[...])
        o_ref[...] = out


def _ffn_call(xf, mod_l, norm_w, w_up_b, conv_w, conv_b, w_down_b, norm_final, seq, final):
    rows, d = xf.shape
    d_ff = w_down_b.shape[0]
    tm = 512
    n_ff = 2
    tf = d_ff // n_ff
    nt = seq // tm
    hb = tm // FFN_HALO
    kern = functools.partial(_ffn_kernel, tm=tm, nt=nt, n_ff=n_ff, final=final)
    return pl.pallas_call(
        kern,
        grid=(rows // tm, n_ff),
        in_specs=[
            pl.BlockSpec((tm, d), lambda i, j: (i, 0)),
            pl.BlockSpec((FFN_HALO, d), lambda i, j: (jnp.maximum(i * hb - 1, 0), 0)),
            pl.BlockSpec((1, 6, d), lambda i, j: (i // nt, 0, 0)),
            pl.BlockSpec((1, d), lambda i, j: (0, 0)),
            pl.BlockSpec((d, tf), lambda i, j: (0, j)),
            pl.BlockSpec((d, tf), lambda i, j: (0, n_ff + j)),
            pl.BlockSpec((CONV_W, tf), lambda i, j: (0, j)),
            pl.BlockSpec((CONV_W, tf), lambda i, j: (0, n_ff + j)),
            pl.BlockSpec((1, tf), lambda i, j: (0, j)),
            pl.BlockSpec((1, tf), lambda i, j: (0, n_ff + j)),
            pl.BlockSpec((tf, d), lambda i, j: (j, 0)),
            pl.BlockSpec((1, d), lambda i, j: (0, 0)),
        ],
        out_specs=pl.BlockSpec((tm, d), lambda i, j: (i, 0)),
        out_shape=jax.ShapeDtypeStruct((rows, d), F32),
        scratch_shapes=[pltpu.VMEM((tm + FFN_HALO, d), BF16), pltpu.VMEM((tm, d), F32)],
        compiler_params=_cparams(("arbitrary", "arbitrary")),
        name="conv_ffn",
    )(xf, xf, mod_l, norm_w, w_up_b, w_up_b, conv_w, conv_w, conv_b, conv_b, w_down_b, norm_final)


def kernel(x, c, w_mod, b_mod, norm_mix, w_in, fox_f_bias, rwkv_mu, rwkv_w0, rwkv_w2, rwkv_a0, rwkv_a2,
           rwkv_g2, rwkv_k_k, rwkv_k_a, rwkv_r_k, rwkv_ln_w, rwkv_ln_b, w_out, norm_ffn, w_up, conv_w,
           conv_b, w_down, norm_final):
    batch, seq, d = x.shape
    n_layers = w_mod.shape[0]
    fox_cols = 3 * FOX_W + FOX_HEADS
    rwkv_cols = 3 * RWKV_W + LORA_W

    mod = _mod_call(c, w_mod, b_mod).reshape(n_layers, batch, 6, d)

    w_in_p = jnp.concatenate(
        [w_in[:, :, :3 * FOX_W], w_in[:, :, fox_cols + rwkv_cols:], w_in[:, :, fox_cols:fox_cols + rwkv_cols],
         w_in[:, :, 3 * FOX_W:fox_cols], jnp.zeros((n_layers, d, FF_PAD - FOX_HEADS), w_in.dtype)],
        axis=-1).astype(BF16)
    f_bias = jnp.pad(fox_f_bias, ((0, 0), (0, FF_PAD - FOX_HEADS)))
    w_out_b = w_out.astype(BF16)
    w_up_b = w_up.astype(BF16)
    w_down_b = w_down.astype(BF16)
    g2_b = rwkv_g2.astype(BF16)

    xf = x.reshape(batch * seq, d)
    for l in range(n_layers):
        row = lambda a: a[l].reshape(1, -1)
        mu = rwkv_mu[l]
        prm = {
            "mu_r": mu[0:RWKV_W].reshape(1, -1), "mu_k": mu[RWKV_W:2 * RWKV_W].reshape(1, -1),
            "mu_v": mu[2 * RWKV_W:3 * RWKV_W].reshape(1, -1), "mu_l": mu[3 * RWKV_W:].reshape(1, -1),
            "w0": row(rwkv_w0), "w2": rwkv_w2[l], "a0": row(rwkv_a0), "a2": rwkv_a2[l], "g2": g2_b[l],
            "k_k": row(rwkv_k_k), "k_a": row(rwkv_k_a), "r_k": row(rwkv_r_k), "ln_w": row(rwkv_ln_w),
            "ln_b": row(rwkv_ln_b),
        }
        p = _inproj_call(xf, mod[l], row(norm_mix), w_in_p[l], seq)
        y_fox = _fox_call(p, f_bias[l:l + 1], batch, seq)
        y_moba = _moba_call(p, batch, seq)
        y_rwkv = _rwkv_call(p, prm, batch, seq)
        xf = _outproj_call(y_fox, y_rwkv, y_moba, xf, mod[l], w_out_b[l], seq)
        xf = _ffn_call(xf, mod[l], row(norm_ffn), w_up_b[l], conv_w[l], conv_b[l].reshape(1, -1), w_down_b[l],
                       norm_final.reshape(1, -1), seq, final=(l == n_layers - 1))
    return xf.reshape(batch, seq, d)
```

```python
import functools

import jax
import jax.numpy as jnp
import numpy as np
from jax import lax
from jax.experimental import pallas as pl
from jax.experimental.pallas import tpu as pltpu

F32 = jnp.float32
BF16 = jnp.bfloat16
HI = lax.Precision.HIGHEST

HEAD_DIM = 64
FOX_HEADS = 4
RWKV_HEADS = 8
MOBA_HEADS = 4
FOX_W = FOX_HEADS * HEAD_DIM
RWKV_W = RWKV_HEADS * HEAD_DIM
MOBA_W = MOBA_HEADS * HEAD_DIM
DECAY_LORA = 64
AAA_LORA = 64
GATE_LORA = 128
LORA_W = DECAY_LORA + AAA_LORA + GATE_LORA
MOBA_BLOCK = 256
MOBA_TOPK = 3
CONV_W = 3
NORM_EPS = 1e-6
GN_EPS = 64e-5
ATTN_SCALE = HEAD_DIM ** -0.5

C_FQ, C_FK, C_FV = 0, 256, 512
C_MQ, C_MK, C_MV = 768, 1024, 1280
C_RR, C_RK, C_RV = 1536, 2048, 2560
C_LORA = 3072
C_FF = 3328
FF_PAD = 128
NP_COLS = C_FF + FF_PAD

RWKV_CHUNK = 64
LANES = 128
VMEM_LIMIT = 56 * 1024 * 1024


def _cparams(sem):
    return pltpu.CompilerParams(dimension_semantics=sem, vmem_limit_bytes=VMEM_LIMIT)


def _dot(a, b):
    return jnp.dot(a, b, preferred_element_type=F32)


def _dot_hi(a, b):
    return jnp.dot(a, b, precision=HI, preferred_element_type=F32)


def _dot_nt(a, b):
    return lax.dot_general(a, b, (((1,), (1,)), ((), ())), preferred_element_type=F32)


def _dot_nt_hi(a, b):
    return lax.dot_general(a, b, (((1,), (1,)), ((), ())), precision=HI, preferred_element_type=F32)


def _mm(a, b):
    return jnp.dot(a.astype(BF16), b.astype(BF16), preferred_element_type=F32)


def _mm_nt(a, b):
    return _dot_nt(a.astype(BF16), b.astype(BF16))


def _split2(x):
    hi = x.astype(BF16)
    return hi, (x - hi.astype(F32)).astype(BF16)


def _dot_split_lhs(x, w_bf16):
    m = x.shape[0]
    r = _dot(jnp.concatenate(_split2(x), axis=0), w_bf16)
    return r[0:m, :] + r[m:2 * m, :]


def _dot3(a, b):
    m = a.shape[0]
    a_hi, a_lo = _split2(a)
    b_hi, b_lo = _split2(b)
    r = _dot(jnp.concatenate([a_hi, a_lo], axis=0), b_hi)
    return r[0:m, :] + r[m:2 * m, :] + _dot(a_hi, b_lo)


def _sigmoid(x):
    return 1.0 / (1.0 + jnp.exp(-x))


def _log_sigmoid(x):
    return jnp.minimum(x, 0.0) - jnp.log1p(jnp.exp(-jnp.abs(x)))


def _rmsnorm(x, w):
    ms = jnp.mean(x * x, axis=-1, keepdims=True)
    return x * lax.rsqrt(ms + NORM_EPS) * w


def _seg_cumsum_rows(x, seg):
    row = lax.broadcasted_iota(jnp.int32, x.shape, 0) & (seg - 1)
    s = 1
    while s < seg:
        x = x + jnp.where(row >= s, pltpu.roll(x, s, 0), 0.0)
        s *= 2
    return x


def _mod_kernel(c_ref, w_ref, b_ref, o_ref):
    c = c_ref[...]
    o_ref[0] = _dot_hi(c * _sigmoid(c), w_ref[0]) + b_ref[0]


def _mod_call(c, w_mod, b_mod):
    n_layers, d, n = w_mod.shape
    b = c.shape[0]
    tn = 1536
    return pl.pallas_call(
        _mod_kernel,
        grid=(n_layers, n // tn),
        in_specs=[
            pl.BlockSpec((b, d), lambda l, j: (0, 0)),
            pl.BlockSpec((1, d, tn), lambda l, j: (l, 0, j)),
            pl.BlockSpec((1, 1, tn), lambda l, j: (l, 0, j)),
        ],
        out_specs=pl.BlockSpec((1, b, tn), lambda l, j: (l, 0, j)),
        out_shape=jax.ShapeDtypeStruct((n_layers, b, n), F32),
        compiler_params=_cparams(("arbitrary", "arbitrary")),
        name="adaln_mod",
    )(c, w_mod, b_mod.reshape(n_layers, 1, n))


def _inproj_kernel(x_ref, mod_ref, nw_ref, w_ref, o_ref):
    m = mod_ref[0]
    h = _rmsnorm(x_ref[...], nw_ref[...]) * (1.0 + m[1:2, :]) + m[0:1, :]
    o_ref[...] = _dot(h.astype(BF16), w_ref[...])


def _inproj_call(xf, mod_l, norm_w, w_in_p, seq):
    rows, d = xf.shape
    tm = 512
    nt = seq // tm
    return pl.pallas_call(
        _inproj_kernel,
        grid=(rows // tm,),
        in_specs=[
            pl.BlockSpec((tm, d), lambda i: (i, 0)),
            pl.BlockSpec((1, 6, d), lambda i: (i // nt, 0, 0)),
            pl.BlockSpec((1, d), lambda i: (0, 0)),
            pl.BlockSpec((d, NP_COLS), lambda i: (0, 0)),
        ],
        out_specs=pl.BlockSpec((tm, NP_COLS), lambda i: (i, 0)),
        out_shape=jax.ShapeDtypeStruct((rows, NP_COLS), F32),
        compiler_params=_cparams(("arbitrary",)),
        name="in_proj",
    )(xf, mod_l, norm_w, w_in_p)


def _softmax_tiles(s, carry, v_t):
    m_new = [jnp.maximum(c[0], jnp.max(x, axis=0, keepdims=True)) for x, c in zip(s, carry)]
    p = [jnp.exp(x - m) for x, m in zip(s, m_new)]
    pv = [_dot(v, x.astype(BF16)) for v, x in zip(v_t, p)]
    out = []
    for (m, l, acc), mn, x, y in zip(carry, m_new, p, pv):
        alpha = jnp.exp(m - mn)
        out.append((mn, alpha * l + jnp.sum(x, axis=0, keepdims=True), alpha * acc + y))
    return out


def _first_tiles(s, v_t):
    m = [jnp.max(x, axis=0, keepdims=True) for x in s]
    p = [jnp.exp(x - mm) for x, mm in zip(s, m)]
    pv = [_dot(v, x.astype(BF16)) for v, x in zip(v_t, p)]
    return [(mm, jnp.sum(x, axis=0, keepdims=True), y) for mm, x, y in zip(m, p, pv)]


def _flatten(carry):
    return tuple(a for c in carry for a in c)


def _unflatten(flat):
    return [tuple(flat[3 * h:3 * h + 3]) for h in range(len(flat) // 3)]


def _store_heads(o_ref, carry):
    o_ref[...] = jnp.concatenate([acc / l for _, l, acc in carry], axis=0).T


F_COL = HEAD_DIM
N_SPLIT = 3
FOX_XW = FOX_W + N_SPLIT * FF_PAD


def _fox_selectors():
    selk = np.zeros((FOX_HEADS, FOX_XW, LANES), np.float32)
    selq = np.zeros((FOX_HEADS, LANES, FOX_XW), np.float32)
    for h in range(FOX_HEADS):
        for c in range(HEAD_DIM):
            selk[h, h * HEAD_DIM + c, c] = 1.0
            selq[h, c, h * HEAD_DIM + c] = 1.0
        for s in range(N_SPLIT):
            selk[h, FOX_W + s * FF_PAD + h, F_COL + s] = 1.0
            selq[h, F_COL + N_SPLIT + s, FOX_W + s * FF_PAD + h] = 1.0
    return jnp.asarray(selk, BF16), jnp.asarray(selq, BF16)


def _fox_kernel(q_ref, k_ref, v_ref, ff_ref, fb_ref, selk_ref, selq_ref, o_ref, ka, v_t, f3, *, tq, n_kv):
    qi = pl.program_id(1)
    heads = range(FOX_HEADS)

    @pl.when(qi == 0)
    def _():
        f = _seg_cumsum_rows(_log_sigmoid(ff_ref[...] + fb_ref[...]), n_kv * tq)
        hi = f.astype(BF16)
        rest = f - hi.astype(F32)
        mid = rest.astype(BF16)
        lo = (rest - mid.astype(F32)).astype(BF16)
        f3[...] = jnp.concatenate([hi, mid, lo], axis=1)
        x = jnp.concatenate([k_ref[...].astype(BF16), f3[...]], axis=1)
        lane = lax.broadcasted_iota(jnp.int32, (1, LANES), 1)
        ones = jnp.where((lane >= F_COL + N_SPLIT) & (lane < F_COL + 2 * N_SPLIT), 1.0, 0.0)
        for h in heads:
            ka[h] = (_dot(x, selk_ref[h]) + ones).astype(BF16)
        for j in range(n_kv):
            v_t[j] = v_ref[j * tq:(j + 1) * tq, :].T.astype(BF16)

    q0 = pl.multiple_of(qi * tq, tq)
    xq = jnp.concatenate([(q_ref[...] * ATTN_SCALE).astype(BF16), f3[pl.ds(q0, tq), :]], axis=1)
    sub = lax.broadcasted_iota(jnp.int32, (LANES, tq), 0)
    neg = jnp.where((sub >= F_COL) & (sub < F_COL + N_SPLIT), -1.0, 0.0)
    q_t = [(_dot_nt(selq_ref[h], xq) + neg).astype(BF16) for h in heads]
    causal = (lax.broadcasted_iota(jnp.int32, (tq, tq), 0) <= lax.broadcasted_iota(jnp.int32, (tq, tq), 1))

    def tiles(j):
        k0 = pl.multiple_of(j * tq, tq)
        s = [_dot(ka[h, pl.ds(k0, tq), :], q_t[h]) for h in heads]
        vj = v_t[j]
        return s, [vj[h * HEAD_DIM:(h + 1) * HEAD_DIM, :] for h in heads]

    s, vv = tiles(qi)
    carry = _first_tiles([jnp.where(causal, x, -jnp.inf) for x in s], vv)

    def body(j, flat):
        s, vv = tiles(j)
        return _flatten(_softmax_tiles(s, _unflatten(flat), vv))

    _store_heads(o_ref, _unflatten(lax.fori_loop(0, qi, body, _flatten(carry))))


def _fox_call(p, f_bias_row, batch, seq):
    tq = 256
    nq = seq // tq
    selk, selq = _fox_selectors()
    kern = functools.partial(_fox_kernel, tq=tq, n_kv=nq)
    return pl.pallas_call(
        kern,
        grid=(batch, nq),
        in_specs=[
            pl.BlockSpec((tq, FOX_W), lambda b, i: (b * nq + i, C_FQ // FOX_W)),
            pl.BlockSpec((seq, FOX_W), lambda b, i: (b, C_FK // FOX_W)),
            pl.BlockSpec((seq, FOX_W), lambda b, i: (b, C_FV // FOX_W)),
            pl.BlockSpec((seq, FF_PAD), lambda b, i: (b, C_FF // FF_PAD)),
            pl.BlockSpec((1, FF_PAD), lambda b, i: (0, 0)),
            pl.BlockSpec(selk.shape, lambda b, i: (0, 0, 0)),
            pl.BlockSpec(selq.shape, lambda b, i: (0, 0, 0)),
        ],
        out_specs=pl.BlockSpec((tq, FOX_W), lambda b, i: (b * nq + i, 0)),
        out_shape=jax.ShapeDtypeStruct((batch * seq, FOX_W), F32),
        scratch_shapes=[
            pltpu.VMEM((FOX_HEADS, seq, LANES), BF16),
            pltpu.VMEM((nq, FOX_W, tq), BF16),
            pltpu.VMEM((seq, N_SPLIT * FF_PAD), BF16),
        ],
        compiler_params=_cparams(("arbitrary", "arbitrary")),
        name="fox_attention",
    )(p, p, p, p, f_bias_row, selk, selq)


SUBLANES = 8


def _moba_selectors():
    selk = np.zeros((MOBA_HEADS, MOBA_W, LANES), np.float32)
    for h in range(MOBA_HEADS):
        for c in range(HEAD_DIM):
            selk[h, h * HEAD_DIM + c, c] = 1.0
    return jnp.asarray(selk, BF16), jnp.asarray(selk.transpose(0, 2, 1), BF16)


def _moba_kernel(q_ref, k_ref, v_ref, selk_ref, selq_ref, o_ref, ka, v_t, kmean, *, blk, n_kb):
    own = pl.program_id(1)
    heads = range(MOBA_HEADS)

    @pl.when(own == 0)
    def _():
        kf = k_ref[...]
        kbf = kf.astype(BF16)
        for h in heads:
            ka[h] = _dot(kbf, selk_ref[h]).astype(BF16)
        kmean[...] = jnp.zeros(kmean.shape, F32)
        for n in range(n_kb):
            kmean[n:n + 1, :] = jnp.mean(kf[n * blk:(n + 1) * blk, :], axis=0, keepdims=True)
            v_t[n] = v_ref[n * blk:(n + 1) * blk, :].T.astype(BF16)

    qf = q_ref[...]
    qs = (qf * ATTN_SCALE).astype(BF16)
    q_t = [_dot_nt(selq_ref[h], qs).astype(BF16) for h in heads]
    sub = lax.broadcasted_iota(jnp.int32, (SUBLANES, blk), 0)
    past = sub < own
    causal = (lax.broadcasted_iota(jnp.int32, (blk, blk), 0) <= lax.broadcasted_iota(jnp.int32, (blk, blk), 1))

    sel = []
    for h in heads:
        hs = slice(h * HEAD_DIM, (h + 1) * HEAD_DIM)
        gate = _dot_nt_hi(kmean[:, hs], qf[:, hs])
        selm = jnp.zeros((SUBLANES, blk), F32)
        for n in range(n_kb):
            gn = gate[n:n + 1, :]
            beats = past & ((gate > gn) | ((gate == gn) & (sub < n)))
            rank = jnp.sum(beats.astype(F32), axis=0, keepdims=True)
            selm = jnp.where(sub == n, (rank < MOBA_TOPK).astype(F32), selm)
        sel.append(jnp.where(past, selm, 0.0))

    def tiles(j):
        k0 = pl.multiple_of(j * blk, blk)
        s = [_dot(ka[h, pl.ds(k0, blk), :], q_t[h]) for h in heads]
        vj = v_t[j]
        return s, [vj[h * HEAD_DIM:(h + 1) * HEAD_DIM, :] for h in heads]

    s, vv = tiles(own)
    carry = _first_tiles([jnp.where(causal, x, -jnp.inf) for x in s], vv)

    def body(j, flat):
        s, vv = tiles(j)
        picked = [jnp.sum(jnp.where(sub == j, sel[h], 0.0), axis=0, keepdims=True) > 0.5 for h in heads]
        s = [jnp.where(pk, x, -jnp.inf) for pk, x in zip(picked, s)]
        return _flatten(_softmax_tiles(s, _unflatten(flat), vv))

    _store_heads(o_ref, _unflatten(lax.fori_loop(0, own, body, _flatten(carry))))


def _moba_call(p, batch, seq):
    blk = MOBA_BLOCK
    n_kb = seq // blk
    assert n_kb <= SUBLANES
    selk, selq = _moba_selectors()
    kern = functools.partial(_moba_kernel, blk=blk, n_kb=n_kb)
    return pl.pallas_call(
        kern,
        grid=(batch, n_kb),
        in_specs=[
            pl.BlockSpec((blk, MOBA_W), lambda b, i: (b * n_kb + i, C_MQ // MOBA_W)),
            pl.BlockSpec((seq, MOBA_W), lambda b, i: (b, C_MK // MOBA_W)),
            pl.BlockSpec((seq, MOBA_W), lambda b, i: (b, C_MV // MOBA_W)),
            pl.BlockSpec(selk.shape, lambda b, i: (0, 0, 0)),
            pl.BlockSpec(selq.shape, lambda b, i: (0, 0, 0)),
        ],
        out_specs=pl.BlockSpec((blk, MOBA_W), lambda b, i: (b * n_kb + i, 0)),
        out_shape=jax.ShapeDtypeStruct((batch * seq, MOBA_W), F32),
        scratch_shapes=[
            pltpu.VMEM((MOBA_HEADS, seq, LANES), BF16),
            pltpu.VMEM((n_kb, MOBA_W, blk), BF16),
            pltpu.VMEM((SUBLANES, MOBA_W), F32),
        ],
        compiler_params=_cparams(("arbitrary", "arbitrary")),
        name="moba_attention",
    )(p, p, p, selk, selq)


def _rwkv_kernel(r_ref, k_ref, v_ref, lo_ref, mu_r, mu_k, mu_v, mu_l, w0, w2, a0, a2, g2, kkw, kaw, rkw,
                 lnw, lnb, o_ref, prev_r, prev_k, prev_v, prev_l, state, a_t, r_t, k_t, b_t, k_b, b_b, v_s,
                 w_c, y_s, *, tm):
    i = pl.program_id(1)
    ch = RWKV_CHUNK
    n_ch = tm // ch

    @pl.when(i == 0)
    def _():
        prev_r[...] = jnp.zeros(prev_r.shape, F32)
        prev_k[...] = jnp.zeros(prev_k.shape, F32)
        prev_v[...] = jnp.zeros(prev_v.shape, F32)
        prev_l[...] = jnp.zeros(prev_l.shape, F32)
        state[...] = jnp.zeros(state.shape, F32)

    def token_shift(x_ref, prev, mu):
        x = x_ref[...]
        first = lax.broadcasted_iota(jnp.int32, x.shape, 0) == 0
        shifted = jnp.where(first, prev[...], pltpu.roll(x, 1, 0))
        prev[...] = x[tm - 1:tm, :]
        return x + (shifted - x) * mu[...]

    r = token_shift(r_ref, prev_r, mu_r)
    k = token_shift(k_ref, prev_k, mu_k)
    v = token_shift(v_ref, prev_v, mu_v)
    lo = token_shift(lo_ref, prev_l, mu_l)
    w_lo = lo[:, 0:DECAY_LORA]
    a_lo = lo[:, DECAY_LORA:DECAY_LORA + AAA_LORA]
    g_lo = lo[:, DECAY_LORA + AAA_LORA:LORA_W]

    log_w = _log_sigmoid(w0[...] + _dot3(jnp.tanh(w_lo), w2[...])) - 0.5
    lw = -jnp.exp(log_w)
    eta = _sigmoid(a0[...] + _dot3(a_lo, a2[...]))
    gate = _dot(_sigmoid(g_lo).astype(BF16), g2[...])

    gi = lax.broadcasted_iota(jnp.int32, (RWKV_W, RWKV_W), 0) // HEAD_DIM
    gj = lax.broadcasted_iota(jnp.int32, (RWKV_W, RWKV_W), 1) // HEAD_DIM
    group = (gi == gj).astype(BF16)

    kk = k * kkw[...]
    kk = kk / jnp.maximum(jnp.sqrt(_dot_split_lhs(kk * kk, group)), 1e-12)
    kp = k * (1.0 + (eta - 1.0) * kaw[...])
    bb = kk * eta

    lc = _seg_cumsum_rows(lw, ch)
    a_t[...] = -kk * jnp.exp(lc - lw)
    r_t[...] = r * jnp.exp(lc)
    einv = jnp.exp(-lc)
    k_t[...] = kp * einv
    b_t[...] = bb * einv
    v_s[...] = v
    for c in range(n_ch):
        rows = slice(c * ch, (c + 1) * ch)
        last = lc[(c + 1) * ch - 1:(c + 1) * ch, :]
        e = jnp.exp(last - lc[rows, :])
        k_b[rows, :] = kp[rows, :] * e
        b_b[rows, :] = bb[rows, :] * e
        w_c[c] = jnp.broadcast_to(jnp.exp(last), (8, RWKV_W))

    ti = lax.broadcasted_iota(jnp.int32, (ch, ch), 0)
    tj = lax.broadcasted_iota(jnp.int32, (ch, ch), 1)
    strict = tj < ti
    incl = tj <= ti

    chains = [(c, h) for c in range(n_ch) for h in range(RWKV_HEADS)]
    ids = range(len(chains))

    def tile(ref):
        return [ref[c * ch:(c + 1) * ch, h * HEAD_DIM:(h + 1) * HEAD_DIM] for c, h in chains]

    at, rt, vv, kbar, bbar = tile(a_t), tile(r_t), tile(v_s), tile(k_b), tile(b_b)
    btl, ktl = tile(b_t), tile(k_t)
    pad = jnp.zeros((LANES - ch, HEAD_DIM), BF16)
    m4 = [_dot_nt(jnp.concatenate([at[i], rt[i]], axis=0).astype(BF16),
                  jnp.concatenate([btl[i].astype(BF16), pad, ktl[i].astype(BF16), pad], axis=0)) for i in ids]
    a_ab = [jnp.where(strict, m4[i][0:ch, 0:ch], 0.0) for i in ids]
    a_ak = [jnp.where(strict, m4[i][0:ch, LANES:LANES + ch], 0.0) for i in ids]
    a_rb = [jnp.where(incl, m4[i][ch:2 * ch, 0:ch], 0.0) for i in ids]
    a_rk = [jnp.where(incl, m4[i][ch:2 * ch, LANES:LANES + ch], 0.0) for i in ids]
    avk = [_mm(jnp.concatenate([a_ak[i], a_rk[i]], axis=0), vv[i]) for i in ids]
    pw = a_ab
    tx = [jnp.concatenate([at[i], avk[i][0:ch, :]], axis=1) for i in ids]
    span = 1
    while 2 * span < ch:
        x = [_mm(pw[i], jnp.concatenate([tx[i], pw[i]], axis=1)) for i in ids]
        pw = [x[i][:, 2 * ch:3 * ch] for i in ids]
        tx = [tx[i] + x[i][:, 0:2 * ch] for i in ids]
        span *= 2
    tx = [tx[i] + _mm(pw[i], tx[i]) for i in ids]
    ry = [_mm(a_rb[i], tx[i]) for i in ids]
    rhat = [rt[i] + ry[i][:, 0:ch] for i in ids]
    yhat = [avk[i][ch:2 * ch, :] + ry[i][:, ch:2 * ch] for i in ids]
    z = [_mm(tx[i][:, 0:ch].T, bbar[i]) for i in ids]
    kv = [_mm(jnp.concatenate([vv[i], tx[i][:, ch:2 * ch]], axis=0).T,
              jnp.concatenate([kbar[i], bbar[i]], axis=0)) for i in ids]

    s_cur = [state[h] for h in range(RWKV_HEADS)]
    for c in range(n_ch):
        base = c * RWKV_HEADS
        ys = [_mm_nt(rhat[base + h], s_cur[h]) for h in range(RWKV_HEADS)]
        sz = [_mm(s_cur[h], z[base + h]) for h in range(RWKV_HEADS)]
        for h in range(RWKV_HEADS):
            hs = slice(h * HEAD_DIM, (h + 1) * HEAD_DIM)
            y_s[c * ch:(c + 1) * ch, hs] = ys[h] + yhat[base + h]
            s_cur[h] = s_cur[h] * w_c[c][0:1, hs] + sz[h] + kv[base + h]
    for h in range(RWKV_HEADS):
        state[h] = s_cur[h]

    y = y_s[...]
    inv_d = 1.0 / HEAD_DIM
    mean = _dot_split_lhs(y, group) * inv_d
    d = y - mean
    var = _dot_split_lhs(d * d, group) * inv_d
    yn = d * lax.rsqrt(var + GN_EPS) * lnw[...] + lnb[...]
    bonus = _dot_split_lhs(r * kp * rkw[...], group) * v
    o_ref[...] = (yn + bonus) * gate


def _rwkv_call(p, prm, batch, seq):
    tm = 256
    nt = seq // tm
    kern = functools.partial(_rwkv_kernel, tm=tm)

    def rows(width, cstart):
        return pl.BlockSpec((tm, width), lambda b, i: (b * nt + i, cstart // width))

    def full(a):
        return pl.BlockSpec(a.shape, lambda b, i: (0,) * a.ndim)

    params = [prm[n] for n in ("mu_r", "mu_k", "mu_v", "mu_l", "w0", "w2", "a0", "a2", "g2", "k_k", "k_a",
                               "r_k", "ln_w", "ln_b")]
    big = pltpu.VMEM((tm, RWKV_W), F32)
    return pl.pallas_call(
        kern,
        grid=(batch, nt),
        in_specs=[rows(RWKV_W, C_RR), rows(RWKV_W, C_RK), rows(RWKV_W, C_RV), rows(LORA_W, C_LORA)]
        + [full(a) for a in params],
        out_specs=pl.BlockSpec((tm, RWKV_W), lambda b, i: (b * nt + i, 0)),
        out_shape=jax.ShapeDtypeStruct((batch * seq, RWKV_W), F32),
        scratch_shapes=[
            pltpu.VMEM((1, RWKV_W), F32), pltpu.VMEM((1, RWKV_W), F32), pltpu.VMEM((1, RWKV_W), F32),
            pltpu.VMEM((1, LORA_W), F32),
            pltpu.VMEM((RWKV_HEADS, HEAD_DIM, HEAD_DIM), F32),
            big, big, big, big, big, big, big,
            pltpu.VMEM((tm // RWKV_CHUNK, 8, RWKV_W), F32),
            big,
        ],
        compiler_params=_cparams(("arbitrary", "arbitrary")),
        name="rwkv7_mix",
    )(p, p, p, p, *params)


def _outproj_kernel(yf_ref, yr_ref, ym_ref, x_ref, mod_ref, w_ref, o_ref):
    z = _dot(yf_ref[...].astype(BF16), w_ref[0:FOX_W, :])
    z = z + _dot(yr_ref[...].astype(BF16), w_ref[FOX_W:FOX_W + RWKV_W, :])
    z = z + _dot(ym_ref[...].astype(BF16), w_ref[FOX_W + RWKV_W:, :])
    o_ref[...] = x_ref[...] + mod_ref[0][2:3, :] * z


def _outproj_call(yf, yr, ym, xf, mod_l, w_out_b, seq):
    rows, d = xf.shape
    tm = 512
    nt = seq // tm
    return pl.pallas_call(
        _outproj_kernel,
        grid=(rows // tm,),
        in_specs=[
            pl.BlockSpec((tm, FOX_W), lambda i: (i, 0)),
            pl.BlockSpec((tm, RWKV_W), lambda i: (i, 0)),
            pl.BlockSpec((tm, MOBA_W), lambda i: (i, 0)),
            pl.BlockSpec((tm, d), lambda i: (i, 0)),
            pl.BlockSpec((1, 6, d), lambda i: (i // nt, 0, 0)),
            pl.BlockSpec(w_out_b.shape, lambda i: (0, 0)),
        ],
        out_specs=pl.BlockSpec((tm, d), lambda i: (i, 0)),
        out_shape=jax.ShapeDtypeStruct((rows, d), F32),
        compiler_params=_cparams(("arbitrary",)),
        name="out_proj",
    )(yf, yr, ym, xf, mod_l, w_out_b)


FFN_HALO = 16


def _ffn_kernel(x_ref, xp_ref, mod_ref, nw_ref, wg_ref, wv_ref, cwg_ref, cwv_ref, cbg_ref, cbv_ref, wd_ref,
                nf_ref, o_ref, hext, acc, *, tm, nt, n_ff, final):
    i = pl.program_id(0)
    j = pl.program_id(1)
    m = mod_ref[0]

    @pl.when(j == 0)
    def _():
        def pre(x):
            return _rmsnorm(x, nw_ref[...]) * (1.0 + m[4:5, :]) + m[3:4, :]

        keep = jnp.where(i % nt == 0, 0.0, 1.0)
        hext[0:FFN_HALO, :] = (pre(xp_ref[...]) * keep).astype(BF16)
        hext[FFN_HALO:, :] = pre(x_ref[...]).astype(BF16)
        acc[...] = jnp.zeros(acc.shape, F32)

    he = hext[...]

    def conv(u, cw, cb):
        return (cb[...] + cw[0:1, :] * pltpu.roll(u, 2, 0)[FFN_HALO:, :]
                + cw[1:2, :] * pltpu.roll(u, 1, 0)[FFN_HALO:, :] + cw[2:3, :] * u[FFN_HALO:, :])

    ug = conv(_dot(he, wg_ref[...]), cwg_ref, cbg_ref)
    uv = conv(_dot(he, wv_ref[...]), cwv_ref, cbv_ref)
    act = ug * _sigmoid(ug) * uv
    acc[...] += _dot(act.astype(BF16), wd_ref[...])

    @pl.when(j == n_ff - 1)
    def _():
        out = x_ref[...] + m[5:6, :] * acc[...]
        if final:
            out = _rmsnorm(out, nf_ref[...])
        o_ref[...] = out


def _ffn_call(xf, mod_l, norm_w, w_up_b, conv_w, conv_b, w_down_b, norm_final, seq, final):
    rows, d = xf.shape
    d_ff = w_down_b.shape[0]
    tm = 512
    n_ff = 2
    tf = d_ff // n_ff
    nt = seq // tm
    hb = tm // FFN_HALO
    kern = functools.partial(_ffn_kernel, tm=tm, nt=nt, n_ff=n_ff, final=final)
    return pl.pallas_call(
        kern,
        grid=(rows // tm, n_ff),
        in_specs=[
            pl.BlockSpec((tm, d), lambda i, j: (i, 0)),
            pl.BlockSpec((FFN_HALO, d), lambda i, j: (jnp.maximum(i * hb - 1, 0), 0)),
            pl.BlockSpec((1, 6, d), lambda i, j: (i // nt, 0, 0)),
            pl.BlockSpec((1, d), lambda i, j: (0, 0)),
            pl.BlockSpec((d, tf), lambda i, j: (0, j)),
            pl.BlockSpec((d, tf), lambda i, j: (0, n_ff + j)),
            pl.BlockSpec((CONV_W, tf), lambda i, j: (0, j)),
            pl.BlockSpec((CONV_W, tf), lambda i, j: (0, n_ff + j)),
            pl.BlockSpec((1, tf), lambda i, j: (0, j)),
            pl.BlockSpec((1, tf), lambda i, j: (0, n_ff + j)),
            pl.BlockSpec((tf, d), lambda i, j: (j, 0)),
            pl.BlockSpec((1, d), lambda i, j: (0, 0)),
        ],
        out_specs=pl.BlockSpec((tm, d), lambda i, j: (i, 0)),
        out_shape=jax.ShapeDtypeStruct((rows, d), F32),
        scratch_shapes=[pltpu.VMEM((tm + FFN_HALO, d), BF16), pltpu.VMEM((tm, d), F32)],
        compiler_params=_cparams(("arbitrary", "arbitrary")),
        name="conv_ffn",
    )(xf, xf, mod_l, norm_w, w_up_b, w_up_b, conv_w, conv_w, conv_b, conv_b, w_down_b, norm_final)


def kernel(x, c, w_mod, b_mod, norm_mix, w_in, fox_f_bias, rwkv_mu, rwkv_w0, rwkv_w2, rwkv_a0, rwkv_a2,
           rwkv_g2, rwkv_k_k, rwkv_k_a, rwkv_r_k, rwkv_ln_w, rwkv_ln_b, w_out, norm_ffn, w_up, conv_w,
           conv_b, w_down, norm_final):
    batch, seq, d = x.shape
    n_layers = w_mod.shape[0]
    fox_cols = 3 * FOX_W + FOX_HEADS
    rwkv_cols = 3 * RWKV_W + LORA_W

    mod = _mod_call(c, w_mod, b_mod).reshape(n_layers, batch, 6, d)

    w_in_p = jnp.concatenate(
        [w_in[:, :, :3 * FOX_W], w_in[:, :, fox_cols + rwkv_cols:], w_in[:, :, fox_cols:fox_cols + rwkv_cols],
         w_in[:, :, 3 * FOX_W:fox_cols], jnp.zeros((n_layers, d, FF_PAD - FOX_HEADS), w_in.dtype)],
        axis=-1).astype(BF16)
    f_bias = jnp.pad(fox_f_bias, ((0, 0), (0, FF_PAD - FOX_HEADS)))
    w_out_b = w_out.astype(BF16)
    w_up_b = w_up.astype(BF16)
    w_down_b = w_down.astype(BF16)
    g2_b = rwkv_g2.astype(BF16)

    xf = x.reshape(batch * seq, d)
    for l in range(n_layers):
        row = lambda a: a[l].reshape(1, -1)
        mu = rwkv_mu[l]
        prm = {
            "mu_r": mu[0:RWKV_W].reshape(1, -1), "mu_k": mu[RWKV_W:2 * RWKV_W].reshape(1, -1),
            "mu_v": mu[2 * RWKV_W:3 * RWKV_W].reshape(1, -1), "mu_l": mu[3 * RWKV_W:].reshape(1, -1),
            "w0": row(rwkv_w0), "w2": rwkv_w2[l], "a0": row(rwkv_a0), "a2": rwkv_a2[l], "g2": g2_b[l],
            "k_k": row(rwkv_k_k), "k_a": row(rwkv_k_a), "r_k": row(rwkv_r_k), "ln_w": row(rwkv_ln_w),
            "ln_b": row(rwkv_ln_b),
        }
        p = _inproj_call(xf, mod[l], row(norm_mix), w_in_p[l], seq)
        y_fox = _fox_call(p, f_bias[l:l + 1], batch, seq)
        y_moba = _moba_call(p, batch, seq)
        y_rwkv = _rwkv_call(p, prm, batch, seq)
        xf = _outproj_call(y_fox, y_rwkv, y_moba, xf, mod[l], w_out_b[l], seq)
        xf = _ffn_call(xf, mod[l], row(norm_ffn), w_up_b[l], conv_w[l], conv_b[l].reshape(1, -1), w_down_b[l],
                       norm_final.reshape(1, -1), seq, final=(l == n_layers - 1))
    return xf.reshape(batch, seq, d)
```

```python
import functools

import jax
import jax.numpy as jnp
import numpy as np
from jax import lax
from jax.experimental import pallas as pl
from jax.experimental.pallas import tpu as pltpu

F32 = jnp.float32
BF16 = jnp.bfloat16
HI = lax.Precision.HIGHEST

HEAD_DIM = 64
FOX_HEADS = 4
RWKV_HEADS = 8
MOBA_HEADS = 4
FOX_W = FOX_HEADS * HEAD_DIM
RWKV_W = RWKV_HEADS * HEAD_DIM
MOBA_W = MOBA_HEADS * HEAD_DIM
DECAY_LORA = 64
AAA_LORA = 64
GATE_LORA = 128
LORA_W = DECAY_LORA + AAA_LORA + GATE_LORA
MOBA_BLOCK = 256
MOBA_TOPK = 3
CONV_W = 3
NORM_EPS = 1e-6
GN_EPS = 64e-5
ATTN_SCALE = HEAD_DIM ** -0.5

C_FQ, C_FK, C_FV = 0, 256, 512
C_MQ, C_MK, C_MV = 768, 1024, 1280
C_RR, C_RK, C_RV = 1536, 2048, 2560
C_LORA = 3072
C_FF = 3328
FF_PAD = 128
NP_COLS = C_FF + FF_PAD

RWKV_CHUNK = 64
LANES = 128
SUBLANES = 8
VMEM_LIMIT = 56 * 1024 * 1024


def _cparams(sem):
    return pltpu.CompilerParams(dimension_semantics=sem, vmem_limit_bytes=VMEM_LIMIT)


def _dot(a, b):
    return jnp.dot(a, b, preferred_element_type=F32)


def _dot_hi(a, b):
    return jnp.dot(a, b, precision=HI, preferred_element_type=F32)


def _dot_nt(a, b):
    return lax.dot_general(a, b, (((1,), (1,)), ((), ())), preferred_element_type=F32)


def _dot_nt_hi(a, b):
    return lax.dot_general(a, b, (((1,), (1,)), ((), ())), precision=HI, preferred_element_type=F32)


def _mm(a, b):
    return jnp.dot(a.astype(BF16), b.astype(BF16), preferred_element_type=F32)


def _mm_nt(a, b):
    return _dot_nt(a.astype(BF16), b.astype(BF16))


def _split2(x):
    hi = x.astype(BF16)
    return hi, (x - hi.astype(F32)).astype(BF16)


def _dot_split_lhs(x, w_bf16):
    m = x.shape[0]
    r = _dot(jnp.concatenate(_split2(x), axis=0), w_bf16)
    return r[0:m, :] + r[m:2 * m, :]


def _dot3(a, b):
    m = a.shape[0]
    a_hi, a_lo = _split2(a)
    b_hi, b_lo = _split2(b)
    r = _dot(jnp.concatenate([a_hi, a_lo], axis=0), b_hi)
    return r[0:m, :] + r[m:2 * m, :] + _dot(a_hi, b_lo)


def _sigmoid(x):
    return 1.0 / (1.0 + jnp.exp(-x))


def _log_sigmoid(x):
    return jnp.minimum(x, 0.0) - jnp.log1p(jnp.exp(-jnp.abs(x)))


def _rmsnorm(x, w):
    ms = jnp.mean(x * x, axis=-1, keepdims=True)
    return x * lax.rsqrt(ms + NORM_EPS) * w


def _seg_cumsum_rows(x, seg):
    row = lax.broadcasted_iota(jnp.int32, x.shape, 0) & (seg - 1)
    s = 1
    while s < seg:
        x = x + jnp.where(row >= s, pltpu.roll(x, s, 0), 0.0)
        s *= 2
    return x


def _mod_kernel(c_ref, w_ref, b_ref, o_ref):
    c = c_ref[...]
    o_ref[0] = _dot_hi(c * _sigmoid(c), w_ref[0]) + b_ref[0]


def _mod_call(c, w_mod, b_mod):
    n_layers, d, n = w_mod.shape
    b = c.shape[0]
    tn = 1536
    return pl.pallas_call(
        _mod_kernel,
        grid=(n_layers, n // tn),
        in_specs=[
            pl.BlockSpec((b, d), lambda l, j: (0, 0)),
            pl.BlockSpec((1, d, tn), lambda l, j: (l, 0, j)),
            pl.BlockSpec((1, 1, tn), lambda l, j: (l, 0, j)),
        ],
        out_specs=pl.BlockSpec((1, b, tn), lambda l, j: (l, 0, j)),
        out_shape=jax.ShapeDtypeStruct((n_layers, b, n), F32),
        compiler_params=_cparams(("arbitrary", "arbitrary")),
        name="adaln_mod",
    )(c, w_mod, b_mod.reshape(n_layers, 1, n))


def _win_layout_kernel(w_ref, o_ref):
    fox_cols = 3 * FOX_W + FOX_HEADS
    rwkv_cols = 3 * RWKV_W + LORA_W
    w = w_ref[...]
    o_ref[:, C_FQ:C_MQ] = w[:, 0:3 * FOX_W].astype(BF16)
    o_ref[:, C_MQ:C_RR] = w[:, fox_cols + rwkv_cols:fox_cols + rwkv_cols + 3 * MOBA_W].astype(BF16)
    o_ref[:, C_RR:C_FF] = w[:, fox_cols:fox_cols + rwkv_cols].astype(BF16)
    lane = lax.broadcasted_iota(jnp.int32, (w.shape[0], FF_PAD), 1)
    o_ref[:, C_FF:NP_COLS] = jnp.where(lane < FOX_HEADS, w[:, 3 * FOX_W:3 * FOX_W + FF_PAD], 0.0).astype(BF16)


def _win_layout_call(w_in):
    n_layers, d, n = w_in.shape
    tr = 256
    return pl.pallas_call(
        _win_layout_kernel,
        grid=(n_layers, d // tr),
        in_specs=[pl.BlockSpec((None, tr, n), lambda l, i: (l, i, 0))],
        out_specs=pl.BlockSpec((None, tr, NP_COLS), lambda l, i: (l, i, 0)),
        out_shape=jax.ShapeDtypeStruct((n_layers, d, NP_COLS), BF16),
        compiler_params=_cparams(("arbitrary", "arbitrary")),
        name="w_in_layout",
    )(w_in)


def _inproj_kernel(x_ref, mod_ref, nw_ref, w_ref, mu_ref, o_ref, prev, *, tm, nt):
    i = pl.program_id(0)
    m = mod_ref[0]
    h = _rmsnorm(x_ref[...], nw_ref[...]) * (1.0 + m[1:2, :]) + m[0:1, :]
    p = _dot(h.astype(BF16), w_ref[...])
    o_ref[:, 0:C_RR] = p[:, 0:C_RR]
    o_ref[:, C_FF:NP_COLS] = p[:, C_FF:NP_COLS]
    feat = p[:, C_RR:C_FF]

    @pl.when(i % nt == 0)
    def _():
        prev[...] = jnp.zeros(prev.shape, F32)

    rolled = pltpu.roll(feat, 1, 0)
    first = lax.broadcasted_iota(jnp.int32, (SUBLANES, C_FF - C_RR), 0) == 0
    top = jnp.where(first, prev[...], rolled[0:SUBLANES, :])
    shifted = jnp.concatenate([top, rolled[SUBLANES:, :]], axis=0)
    prev[...] = feat[tm - 1:tm, :]
    o_ref[:, C_RR:C_FF] = feat + (shifted - feat) * mu_ref[...]


def _inproj_call(xf, mod_l, norm_w, w_in_p, mu_row, layer, seq):
    rows, d = xf.shape
    tm = 512
    nt = seq // tm
    kern = functools.partial(_inproj_kernel, tm=tm, nt=nt)
    return pl.pallas_call(
        kern,
        grid=(rows // tm,),
        in_specs=[
            pl.BlockSpec((tm, d), lambda i: (i, 0)),
            pl.BlockSpec((1, 6, d), lambda i: (i // nt, 0, 0)),
            pl.BlockSpec((1, d), lambda i: (0, 0)),
            pl.BlockSpec((None, d, NP_COLS), lambda i: (layer, 0, 0)),
            pl.BlockSpec((1, C_FF - C_RR), lambda i: (0, 0)),
        ],
        out_specs=pl.BlockSpec((tm, NP_COLS), lambda i: (i, 0)),
        out_shape=jax.ShapeDtypeStruct((rows, NP_COLS), F32),
        scratch_shapes=[pltpu.VMEM((1, C_FF - C_RR), F32)],
        compiler_params=_cparams(("arbitrary",)),
        name="in_proj",
    )(xf, mod_l, norm_w, w_in_p, mu_row)


def _softmax_tiles(s, carry, v_t):
    m_new = [jnp.maximum(c[0], jnp.max(x, axis=0, keepdims=True)) for x, c in zip(s, carry)]
    p = [jnp.exp(x - m) for x, m in zip(s, m_new)]
    pv = [_dot(v, x.astype(BF16)) for v, x in zip(v_t, p)]
    out = []
    for (m, l, acc), mn, x, y in zip(carry, m_new, p, pv):
        alpha = jnp.exp(m - mn)
        out.append((mn, alpha * l + jnp.sum(x, axis=0, keepdims=True), alpha * acc + y))
    return out


def _first_tiles(s, v_t):
    m = [jnp.max(x, axis=0, keepdims=True) for x in s]
    p = [jnp.exp(x - mm) for x, mm in zip(s, m)]
    pv = [_dot(v, x.astype(BF16)) for v, x in zip(v_t, p)]
    return [(mm, jnp.sum(x, axis=0, keepdims=True), y) for mm, x, y in zip(m, p, pv)]


def _flatten(carry):
    return tuple(a for c in carry for a in c)


def _unflatten(flat):
    return [tuple(flat[3 * h:3 * h + 3]) for h in range(len(flat) // 3)]


def _store_heads(o_ref, carry):
    o_ref[...] = jnp.concatenate([acc / l for _, l, acc in carry], axis=0).T


F_COL = HEAD_DIM
N_SPLIT = 3
FOX_XW = FOX_W + N_SPLIT * FF_PAD


def _fox_selectors():
    selk = np.zeros((FOX_HEADS, FOX_XW, LANES), np.float32)
    selq = np.zeros((FOX_HEADS, LANES, FOX_XW), np.float32)
    for h in range(FOX_HEADS):
        for c in range(HEAD_DIM):
            selk[h, h * HEAD_DIM + c, c] = 1.0
            selq[h, c, h * HEAD_DIM + c] = 1.0
        for s in range(N_SPLIT):
            selk[h, FOX_W + s * FF_PAD + h, F_COL + s] = 1.0
            selq[h, F_COL + N_SPLIT + s, FOX_W + s * FF_PAD + h] = 1.0
    return jnp.asarray(selk, BF16), jnp.asarray(selq, BF16)


def _fox_kernel(q_ref, k_ref, v_ref, ff_ref, fb_ref, selk_ref, selq_ref, o_ref, ka, v_t, f3, *, tq, tk, n_kv):
    qi = pl.program_id(1)
    heads = range(FOX_HEADS)

    @pl.when(qi == 0)
    def _():
        f = _seg_cumsum_rows(_log_sigmoid(ff_ref[...] + fb_ref[...]), n_kv * tk)
        hi = f.astype(BF16)
        rest = f - hi.astype(F32)
        mid = rest.astype(BF16)
        lo = (rest - mid.astype(F32)).astype(BF16)
        f3[...] = jnp.concatenate([hi, mid, lo], axis=1)
        x = jnp.concatenate([k_ref[...].astype(BF16), f3[...]], axis=1)
        lane = lax.broadcasted_iota(jnp.int32, (1, LANES), 1)
        ones = jnp.where((lane >= F_COL + N_SPLIT) & (lane < F_COL + 2 * N_SPLIT), 1.0, 0.0)
        for h in heads:
            ka[h] = (_dot(x, selk_ref[h]) + ones).astype(BF16)
        for j in range(n_kv):
            v_t[j] = v_ref[j * tk:(j + 1) * tk, :].T.astype(BF16)

    q0 = pl.multiple_of(qi * tq, tq)
    xq = jnp.concatenate([(q_ref[...] * ATTN_SCALE).astype(BF16), f3[pl.ds(q0, tq), :]], axis=1)
    sub = lax.broadcasted_iota(jnp.int32, (LANES, tq), 0)
    neg = jnp.where((sub >= F_COL) & (sub < F_COL + N_SPLIT), -1.0, 0.0)
    q_t = [(_dot_nt(selq_ref[h], xq) + neg).astype(BF16) for h in heads]
    lag = (lax.broadcasted_iota(jnp.int32, (tk, tq), 0) - lax.broadcasted_iota(jnp.int32, (tk, tq), 1))
    n_diag = tq // tk

    def tiles(j):
        k0 = pl.multiple_of(j * tk, tk)
        s = [_dot(ka[h, pl.ds(k0, tk), :], q_t[h]) for h in heads]
        vj = v_t[j]
        return s, [vj[h * HEAD_DIM:(h + 1) * HEAD_DIM, :] for h in heads]

    j0 = qi * n_diag
    s, vv = tiles(j0)
    carry = _first_tiles([jnp.where(lag <= 0, x, -jnp.inf) for x in s], vv)
    for dd in range(1, n_diag):
        s, vv = tiles(j0 + dd)
        carry = _softmax_tiles([jnp.where(lag <= -dd * tk, x, -jnp.inf) for x in s], carry, vv)

    def body(j, flat):
        s, vv = tiles(j)
        return _flatten(_softmax_tiles(s, _unflatten(flat), vv))

    _store_heads(o_ref, _unflatten(lax.fori_loop(0, j0, body, _flatten(carry))))


def _fox_call(p, f_bias_row, batch, seq):
    tq = 512
    tk = 256
    nq = seq // tq
    selk, selq = _fox_selectors()
    kern = functools.partial(_fox_kernel, tq=tq, tk=tk, n_kv=seq // tk)
    return pl.pallas_call(
        kern,
        grid=(batch, nq),
        in_specs=[
            pl.BlockSpec((tq, FOX_W), lambda b, i: (b * nq + i, C_FQ // FOX_W)),
            pl.BlockSpec((seq, FOX_W), lambda b, i: (b, C_FK // FOX_W)),
            pl.BlockSpec((seq, FOX_W), lambda b, i: (b, C_FV // FOX_W)),
            pl.BlockSpec((seq, FF_PAD), lambda b, i: (b, C_FF // FF_PAD)),
            pl.BlockSpec((1, FF_PAD), lambda b, i: (0, 0)),
            pl.BlockSpec(selk.shape, lambda b, i: (0, 0, 0)),
            pl.BlockSpec(selq.shape, lambda b, i: (0, 0, 0)),
        ],
        out_specs=pl.BlockSpec((tq, FOX_W), lambda b, i: (b * nq + i, 0)),
        out_shape=jax.ShapeDtypeStruct((batch * seq, FOX_W), F32),
        scratch_shapes=[
            pltpu.VMEM((FOX_HEADS, seq, LANES), BF16),
            pltpu.VMEM((seq // tk, FOX_W, tk), BF16),
            pltpu.VMEM((seq, N_SPLIT * FF_PAD), BF16),
        ],
        compiler_params=_cparams(("arbitrary", "arbitrary")),
        name="fox_attention",
    )(p, p, p, p, f_bias_row, selk, selq)


def _moba_selectors():
    selk = np.zeros((MOBA_HEADS, MOBA_W, LANES), np.float32)
    for h in range(MOBA_HEADS):
        for c in range(HEAD_DIM):
            selk[h, h * HEAD_DIM + c, c] = 1.0
    return jnp.asarray(selk, BF16), jnp.asarray(selk.transpose(0, 2, 1), BF16)


def _moba_kernel(q_ref, k_ref, v_ref, selk_ref, selq_ref, o_ref, ka, v_t, kmean, *, blk, n_kb):
    own = pl.program_id(1)
    heads = range(MOBA_HEADS)

    @pl.when(own == 0)
    def _():
        kf = k_ref[...]
        kbf = kf.astype(BF16)
        for h in heads:
            ka[h] = _dot(kbf, selk_ref[h]).astype(BF16)
        kmean[...] = jnp.zeros(kmean.shape, F32)
        for n in range(n_kb):
            kmean[n:n + 1, :] = jnp.mean(kf[n * blk:(n + 1) * blk, :], axis=0, keepdims=True)
            v_t[n] = v_ref[n * blk:(n + 1) * blk, :].T.astype(BF16)

    qf = q_ref[...]
    qs = (qf * ATTN_SCALE).astype(BF16)
    q_t = [_dot_nt(selq_ref[h], qs).astype(BF16) for h in heads]
    sub = lax.broadcasted_iota(jnp.int32, (SUBLANES, blk), 0)
    past = sub < own
    causal = (lax.broadcasted_iota(jnp.int32, (blk, blk), 0) <= lax.broadcasted_iota(jnp.int32, (blk, blk), 1))

    sel = []
    for h in heads:
        hs = slice(h * HEAD_DIM, (h + 1) * HEAD_DIM)
        gate = _dot_nt_hi(kmean[:, hs], qf[:, hs])
        selm = jnp.zeros((SUBLANES, blk), F32)
        for n in range(n_kb):
            gn = gate[n:n + 1, :]
            beats = past & ((gate > gn) | ((gate == gn) & (sub < n)))
            rank = jnp.sum(beats.astype(F32), axis=0, keepdims=True)
            selm = jnp.where(sub == n, (rank < MOBA_TOPK).astype(F32), selm)
        sel.append(jnp.where(past, selm, 0.0))

    def tiles(j):
        k0 = pl.multiple_of(j * blk, blk)
        s = [_dot(ka[h, pl.ds(k0, blk), :], q_t[h]) for h in heads]
        vj = v_t[j]
        return s, [vj[h * HEAD_DIM:(h + 1) * HEAD_DIM, :] for h in heads]

    s, vv = tiles(own)
    carry = _first_tiles([jnp.where(causal, x, -jnp.inf) for x in s], vv)

    def body(j, flat):
        s, vv = tiles(j)
        picked = [jnp.sum(jnp.where(sub == j, sel[h], 0.0), axis=0, keepdims=True) > 0.5 for h in heads]
        s = [jnp.where(pk, x, -jnp.inf) for pk, x in zip(picked, s)]
        return _flatten(_softmax_tiles(s, _unflatten(flat), vv))

    _store_heads(o_ref, _unflatten(lax.fori_loop(0, own, body, _flatten(carry))))


def _moba_call(p, batch, seq):
    blk = MOBA_BLOCK
    n_kb = seq // blk
    assert n_kb <= SUBLANES
    selk, selq = _moba_selectors()
    kern = functools.partial(_moba_kernel, blk=blk, n_kb=n_kb)
    return pl.pallas_call(
        kern,
        grid=(batch, n_kb),
        in_specs=[
            pl.BlockSpec((blk, MOBA_W), lambda b, i: (b * n_kb + i, C_MQ // MOBA_W)),
            pl.BlockSpec((seq, MOBA_W), lambda b, i: (b, C_MK // MOBA_W)),
            pl.BlockSpec((seq, MOBA_W), lambda b, i: (b, C_MV // MOBA_W)),
            pl.BlockSpec(selk.shape, lambda b, i: (0, 0, 0)),
            pl.BlockSpec(selq.shape, lambda b, i: (0, 0, 0)),
        ],
        out_specs=pl.BlockSpec((blk, MOBA_W), lambda b, i: (b * n_kb + i, 0)),
        out_shape=jax.ShapeDtypeStruct((batch * seq, MOBA_W), F32),
        scratch_shapes=[
            pltpu.VMEM((MOBA_HEADS, seq, LANES), BF16),
            pltpu.VMEM((n_kb, MOBA_W, blk), BF16),
            pltpu.VMEM((SUBLANES, MOBA_W), F32),
        ],
        compiler_params=_cparams(("arbitrary", "arbitrary")),
        name="moba_attention",
    )(p, p, p, selk, selq)


DECAY_SCALE = float(np.exp(-0.5))


def _rwkv_kernel(r_ref, k_ref, v_ref, lo_ref, w0, w2, a0, a2, g2, kkw, kaw, rkw, lnw, lnb, o_ref,
                 state, a_t, r_t, k_t, b_t, k_b, b_b, v_s, w_c, y_s, *, tm):
    i = pl.program_id(1)
    ch = RWKV_CHUNK
    n_ch = tm // ch

    @pl.when(i == 0)
    def _():
        state[...] = jnp.zeros(state.shape, F32)

    r = r_ref[...]
    k = k_ref[...]
    v = v_ref[...]
    lo = lo_ref[...]
    w_lo = lo[:, 0:DECAY_LORA]
    a_lo = lo[:, DECAY_LORA:DECAY_LORA + AAA_LORA]
    g_lo = lo[:, DECAY_LORA + AAA_LORA:LORA_W]

    lw = -DECAY_SCALE * _sigmoid(w0[...] + _dot3(jnp.tanh(w_lo), w2[...]))
    eta = _sigmoid(a0[...] + _dot3(a_lo, a2[...]))
    gate = _dot(_sigmoid(g_lo).astype(BF16), g2[...])

    gi = lax.broadcasted_iota(jnp.int32, (RWKV_W, RWKV_W), 0) // HEAD_DIM
    gj = lax.broadcasted_iota(jnp.int32, (RWKV_W, RWKV_W), 1) // HEAD_DIM
    group = (gi == gj).astype(BF16)

    kk = k * kkw[...]
    kk = kk * jnp.minimum(lax.rsqrt(_dot_split_lhs(kk * kk, group)), 1e12)
    kp = k * (1.0 + (eta - 1.0) * kaw[...])
    bb = kk * eta

    lc = _seg_cumsum_rows(lw, ch)
    a_t[...] = -kk * jnp.exp(lc - lw)
    r_t[...] = r * jnp.exp(lc)
    einv = jnp.exp(-lc)
    k_t[...] = kp * einv
    b_t[...] = bb * einv
    v_s[...] = v
    for c in range(n_ch):
        rows = slice(c * ch, (c + 1) * ch)
        last = lc[(c + 1) * ch - 1:(c + 1) * ch, :]
        e = jnp.exp(last - lc[rows, :])
        k_b[rows, :] = kp[rows, :] * e
        b_b[rows, :] = bb[rows, :] * e
        w_c[c] = jnp.broadcast_to(jnp.exp(last), (8, RWKV_W))

    ti = lax.broadcasted_iota(jnp.int32, (ch, ch), 0)
    tj = lax.broadcasted_iota(jnp.int32, (ch, ch), 1)
    strict = tj < ti
    incl = tj <= ti

    chains = [(c, h) for c in range(n_ch) for h in range(RWKV_HEADS)]
    ids = range(len(chains))

    def tile(ref):
        return [ref[c * ch:(c + 1) * ch, h * HEAD_DIM:(h + 1) * HEAD_DIM] for c, h in chains]

    at, rt, vv, kbar, bbar = tile(a_t), tile(r_t), tile(v_s), tile(k_b), tile(b_b)
    btl, ktl = tile(b_t), tile(k_t)
    pad = jnp.zeros((LANES - ch, HEAD_DIM), BF16)
    m4 = [_dot_nt(jnp.concatenate([at[i], rt[i]], axis=0).astype(BF16),
                  jnp.concatenate([btl[i].astype(BF16), pad, ktl[i].astype(BF16), pad], axis=0)) for i in ids]
    a_ab = [jnp.where(strict, m4[i][0:ch, 0:ch], 0.0) for i in ids]
    a_ak = [jnp.where(strict, m4[i][0:ch, LANES:LANES + ch], 0.0) for i in ids]
    a_rb = [jnp.where(incl, m4[i][ch:2 * ch, 0:ch], 0.0) for i in ids]
    a_rk = [jnp.where(incl, m4[i][ch:2 * ch, LANES:LANES + ch], 0.0) for i in ids]
    avk = [_mm(jnp.concatenate([a_ak[i], a_rk[i]], axis=0), vv[i]) for i in ids]
    pw = a_ab
    tx = [jnp.concatenate([at[i], avk[i][0:ch, :]], axis=1) for i in ids]
    span = 1
    while 2 * span < ch:
        x = [_mm(pw[i], jnp.concatenate([tx[i], pw[i]], axis=1)) for i in ids]
        pw = [x[i][:, 2 * ch:3 * ch] for i in ids]
        tx = [tx[i] + x[i][:, 0:2 * ch] for i in ids]
        span *= 2
    tx = [tx[i] + _mm(pw[i], tx[i]) for i in ids]
    ry = [_mm(a_rb[i], tx[i]) for i in ids]
    rhat = [rt[i] + ry[i][:, 0:ch] for i in ids]
    yhat = [avk[i][ch:2 * ch, :] + ry[i][:, ch:2 * ch] for i in ids]
    z = [_mm(tx[i][:, 0:ch].T, bbar[i]) for i in ids]
    kv = [_mm(jnp.concatenate([vv[i], tx[i][:, ch:2 * ch]], axis=0).T,
              jnp.concatenate([kbar[i], bbar[i]], axis=0)) for i in ids]

    s_cur = [state[h] for h in range(RWKV_HEADS)]
    for c in range(n_ch):
        base = c * RWKV_HEADS
        ys = [_mm_nt(rhat[base + h], s_cur[h]) for h in range(RWKV_HEADS)]
        sz = [_mm(s_cur[h], z[base + h]) for h in range(RWKV_HEADS)]
        for h in range(RWKV_HEADS):
            hs = slice(h * HEAD_DIM, (h + 1) * HEAD_DIM)
            y_s[c * ch:(c + 1) * ch, hs] = ys[h] + yhat[base + h]
            s_cur[h] = s_cur[h] * w_c[c][0:1, hs] + sz[h] + kv[base + h]
    for h in range(RWKV_HEADS):
        state[h] = s_cur[h]

    y = y_s[...]
    inv_d = 1.0 / HEAD_DIM
    mean = _dot_split_lhs(y, group) * inv_d
    d = y - mean
    var = _dot_split_lhs(d * d, group) * inv_d
    yn = d * lax.rsqrt(var + GN_EPS) * lnw[...] + lnb[...]
    bonus = _dot_split_lhs(r * kp * rkw[...], group) * v
    o_ref[...] = (yn + bonus) * gate


def _rwkv_call(p, prm, batch, seq):
    tm = 256
    nt = seq // tm
    kern = functools.partial(_rwkv_kernel, tm=tm)

    def rows(width, cstart):
        return pl.BlockSpec((tm, width), lambda b, i: (b * nt + i, cstart // width))

    def full(a):
        return pl.BlockSpec(a.shape, lambda b, i: (0,) * a.ndim)

    params = [prm[n] for n in ("w0", "w2", "a0", "a2", "g2", "k_k", "k_a", "r_k", "ln_w", "ln_b")]
    big = pltpu.VMEM((tm, RWKV_W), F32)
    return pl.pallas_call(
        kern,
        grid=(batch, nt),
        in_specs=[rows(RWKV_W, C_RR), rows(RWKV_W, C_RK), rows(RWKV_W, C_RV), rows(LORA_W, C_LORA)]
        + [full(a) for a in params],
        out_specs=pl.BlockSpec((tm, RWKV_W), lambda b, i: (b * nt + i, 0)),
        out_shape=jax.ShapeDtypeStruct((batch * seq, RWKV_W), F32),
        scratch_shapes=[
            pltpu.VMEM((RWKV_HEADS, HEAD_DIM, HEAD_DIM), F32),
            big, big, big, big, big, big, big,
            pltpu.VMEM((tm // RWKV_CHUNK, 8, RWKV_W), F32),
            big,
        ],
        compiler_params=_cparams(("arbitrary", "arbitrary")),
        name="rwkv7_mix",
    )(p, p, p, p, *params)


def _outproj_kernel(yf_ref, yr_ref, ym_ref, x_ref, mod_ref, w_ref, o_ref):
    z = _dot(yf_ref[...].astype(BF16), w_ref[0:FOX_W, :])
    z = z + _dot(yr_ref[...].astype(BF16), w_ref[FOX_W:FOX_W + RWKV_W, :])
    z = z + _dot(ym_ref[...].astype(BF16), w_ref[FOX_W + RWKV_W:, :])
    o_ref[...] = x_ref[...] + mod_ref[0][2:3, :] * z


def _outproj_call(yf, yr, ym, xf, mod_l, w_out_b, layer, seq):
    rows, d = xf.shape
    tm = 512
    nt = seq // tm
    return pl.pallas_call(
        _outproj_kernel,
        grid=(rows // tm,),
        in_specs=[
            pl.BlockSpec((tm, FOX_W), lambda i: (i, 0)),
            pl.BlockSpec((tm, RWKV_W), lambda i: (i, 0)),
            pl.BlockSpec((tm, MOBA_W), lambda i: (i, 0)),
            pl.BlockSpec((tm, d), lambda i: (i, 0)),
            pl.BlockSpec((1, 6, d), lambda i: (i // nt, 0, 0)),
            pl.BlockSpec((None,) + w_out_b.shape[1:], lambda i: (layer, 0, 0)),
        ],
        out_specs=pl.BlockSpec((tm, d), lambda i: (i, 0)),
        out_shape=jax.ShapeDtypeStruct((rows, d), F32),
        compiler_params=_cparams(("arbitrary",)),
        name="out_proj",
    )(yf, yr, ym, xf, mod_l, w_out_b)


FFN_HALO = 16


def _ffn_kernel(x_ref, xp_ref, mod_ref, nw_ref, wg_ref, wv_ref, cwg_ref, cwv_ref, cbg_ref, cbv_ref, wd_ref,
                nf_ref, o_ref, hext, acc, *, tm, nt, n_ff, final):
    i = pl.program_id(0)
    j = pl.program_id(1)
    m = mod_ref[0]

    @pl.when(j == 0)
    def _():
        def pre(x):
            return _rmsnorm(x, nw_ref[...]) * (1.0 + m[4:5, :]) + m[3:4, :]

        keep = jnp.where(i % nt == 0, 0.0, 1.0)
        hext[0:FFN_HALO, :] = (pre(xp_ref[...]) * keep).astype(BF16)
        hext[FFN_HALO:, :] = pre(x_ref[...]).astype(BF16)
        acc[...] = jnp.zeros(acc.shape, F32)

    he = hext[...]

    def conv(u, cw, cb):
        return (cb[...] + cw[0:1, :] * pltpu.roll(u, 2, 0)[FFN_HALO:, :]
                + cw[1:2, :] * pltpu.roll(u, 1, 0)[FFN_HALO:, :] + cw[2:3, :] * u[FFN_HALO:, :])

    ug = conv(_dot(he, wg_ref[...]), cwg_ref, cbg_ref)
    uv = conv(_dot(he, wv_ref[...]), cwv_ref, cbv_ref)
    act = ug * _sigmoid(ug) * uv
    acc[...] += _dot(act.astype(BF16), wd_ref[...])

    @pl.when(j == n_ff - 1)
    def _():
        out = x_ref[...] + m[5:6, :] * acc[...]
        if final:
            out = _rmsnorm(out, nf_ref[...])
        o_ref[...] = out


def _ffn_call(xf, mod_l, norm_w, w_up_b, conv_w, conv_b, w_down_b, norm_final, layer, seq, final):
    rows, d = xf.shape
    d_ff = w_down_b.shape[1]
    tm = 512
    n_ff = 2
    tf = d_ff // n_ff
    nt = seq // tm
    hb = tm // FFN_HALO
    kern = functools.partial(_ffn_kernel, tm=tm, nt=nt, n_ff=n_ff, final=final)
    return pl.pallas_call(
        kern,
        grid=(rows // tm, n_ff),
        in_specs=[
            pl.BlockSpec((tm, d), lambda i, j: (i, 0)),
            pl.BlockSpec((FFN_HALO, d), lambda i, j: (jnp.maximum(i * hb - 1, 0), 0)),
            pl.BlockSpec((1, 6, d), lambda i, j: (i // nt, 0, 0)),
            pl.BlockSpec((1, d), lambda i, j: (0, 0)),
            pl.BlockSpec((None, d, tf), lambda i, j: (layer, 0, j)),
            pl.BlockSpec((None, d, tf), lambda i, j: (layer, 0, n_ff + j)),
            pl.BlockSpec((CONV_W, tf), lambda i, j: (0, j)),
            pl.BlockSpec((CONV_W, tf), lambda i, j: (0, n_ff + j)),
            pl.BlockSpec((1, tf), lambda i, j: (0, j)),
            pl.BlockSpec((1, tf), lambda i, j: (0, n_ff + j)),
            pl.BlockSpec((None, tf, d), lambda i, j: (layer, j, 0)),
            pl.BlockSpec((1, d), lambda i, j: (0, 0)),
        ],
        out_specs=pl.BlockSpec((tm, d), lambda i, j: (i, 0)),
        out_shape=jax.ShapeDtypeStruct((rows, d), F32),
        scratch_shapes=[pltpu.VMEM((tm + FFN_HALO, d), BF16), pltpu.VMEM((tm, d), F32)],
        compiler_params=_cparams(("arbitrary", "arbitrary")),
        name="conv_ffn",
    )(xf, xf, mod_l, norm_w, w_up_b, w_up_b, conv_w, conv_w, conv_b, conv_b, w_down_b, norm_final)


def kernel(x, c, w_mod, b_mod, norm_mix, w_in, fox_f_bias, rwkv_mu, rwkv_w0, rwkv_w2, rwkv_a0, rwkv_a2,
           rwkv_g2, rwkv_k_k, rwkv_k_a, rwkv_r_k, rwkv_ln_w, rwkv_ln_b, w_out, norm_ffn, w_up, conv_w,
           conv_b, w_down, norm_final):
    batch, seq, d = x.shape
    n_layers = w_mod.shape[0]

    mod = _mod_call(c, w_mod, b_mod).reshape(n_layers, batch, 6, d)
    w_in_p = _win_layout_call(w_in)
    f_bias = jnp.pad(fox_f_bias, ((0, 0), (0, FF_PAD - FOX_HEADS)))
    w_out_b = w_out.astype(BF16)
    w_up_b = w_up.astype(BF16)
    w_down_b = w_down.astype(BF16)
    g2_b = rwkv_g2.astype(BF16)

    xf = x.reshape(batch * seq, d)
    for l in range(n_layers):
        row = lambda a: a[l].reshape(1, -1)
        prm = {
            "w0": row(rwkv_w0), "w2": rwkv_w2[l], "a0": row(rwkv_a0), "a2": rwkv_a2[l], "g2": g2_b[l],
            "k_k": row(rwkv_k_k), "k_a": row(rwkv_k_a), "r_k": row(rwkv_r_k), "ln_w": row(rwkv_ln_w),
            "ln_b": row(rwkv_ln_b),
        }
        p = _inproj_call(xf, mod[l], row(norm_mix), w_in_p, row(rwkv_mu), l, seq)
        y_fox = _fox_call(p, f_bias[l:l + 1], batch, seq)
        y_moba = _moba_call(p, batch, seq)
        y_rwkv = _rwkv_call(p, prm, batch, seq)
        xf = _outproj_call(y_fox, y_rwkv, y_moba, xf, mod[l], w_out_b, l, seq)
        xf = _ffn_call(xf, mod[l], row(norm_ffn), w_up_b, conv_w[l], conv_b[l].reshape(1, -1), w_down_b,
                       norm_final.reshape(1, -1), l, seq, final=(l == n_layers - 1))
    return xf.reshape(batch, seq, d)
```

```python
import functools

import jax
import jax.numpy as jnp
import numpy as np
from jax import lax
from jax.experimental import pallas as pl
from jax.experimental.pallas import tpu as pltpu

F32 = jnp.float32
BF16 = jnp.bfloat16
HI = lax.Precision.HIGHEST

HEAD_DIM = 64
FOX_HEADS = 4
RWKV_HEADS = 8
MOBA_HEADS = 4
FOX_W = FOX_HEADS * HEAD_DIM
RWKV_W = RWKV_HEADS * HEAD_DIM
MOBA_W = MOBA_HEADS * HEAD_DIM
DECAY_LORA = 64
AAA_LORA = 64
GATE_LORA = 128
LORA_W = DECAY_LORA + AAA_LORA + GATE_LORA
MOBA_BLOCK = 256
MOBA_TOPK = 3
CONV_W = 3
NORM_EPS = 1e-6
GN_EPS = 64e-5
ATTN_SCALE = HEAD_DIM ** -0.5

C_FQ, C_FK, C_FV = 0, 256, 512
C_MQ, C_MK, C_MV = 768, 1024, 1280
C_RR, C_RK, C_RV = 1536, 2048, 2560
C_LORA = 3072
C_FF = 3328
FF_PAD = 128
NP_COLS = C_FF + FF_PAD

RWKV_CHUNK = 64
LANES = 128
SUBLANES = 8
VMEM_LIMIT = 56 * 1024 * 1024


def _cparams(sem):
    return pltpu.CompilerParams(dimension_semantics=sem, vmem_limit_bytes=VMEM_LIMIT)


def _dot(a, b):
    return jnp.dot(a, b, preferred_element_type=F32)


def _dot_hi(a, b):
    return jnp.dot(a, b, precision=HI, preferred_element_type=F32)


def _dot_nt(a, b):
    return lax.dot_general(a, b, (((1,), (1,)), ((), ())), preferred_element_type=F32)


def _dot_nt_hi(a, b):
    return lax.dot_general(a, b, (((1,), (1,)), ((), ())), precision=HI, preferred_element_type=F32)


def _mm(a, b):
    return jnp.dot(a.astype(BF16), b.astype(BF16), preferred_element_type=F32)


def _mm_nt(a, b):
    return _dot_nt(a.astype(BF16), b.astype(BF16))


def _split2(x):
    hi = x.astype(BF16)
    return hi, (x - hi.astype(F32)).astype(BF16)


def _dot_split_lhs(x, w_bf16):
    m = x.shape[0]
    r = _dot(jnp.concatenate(_split2(x), axis=0), w_bf16)
    return r[0:m, :] + r[m:2 * m, :]


def _dot3(a, b):
    m = a.shape[0]
    a_hi, a_lo = _split2(a)
    b_hi, b_lo = _split2(b)
    r = _dot(jnp.concatenate([a_hi, a_lo], axis=0), b_hi)
    return r[0:m, :] + r[m:2 * m, :] + _dot(a_hi, b_lo)


def _sigmoid(x):
    return 1.0 / (1.0 + jnp.exp(-x))


def _log_sigmoid(x):
    return jnp.minimum(x, 0.0) - jnp.log1p(jnp.exp(-jnp.abs(x)))


def _rmsnorm(x, w):
    ms = jnp.mean(x * x, axis=-1, keepdims=True)
    return x * lax.rsqrt(ms + NORM_EPS) * w


def _seg_cumsum_rows(x, seg):
    row = lax.broadcasted_iota(jnp.int32, x.shape, 0) & (seg - 1)
    s = 1
    while s < seg:
        x = x + jnp.where(row >= s, pltpu.roll(x, s, 0), 0.0)
        s *= 2
    return x


def _mod_kernel(c_ref, w_ref, b_ref, o_ref):
    c = c_ref[...]
    o_ref[0] = _dot_hi(c * _sigmoid(c), w_ref[0]) + b_ref[0]


def _mod_call(c, w_mod, b_mod):
    n_layers, d, n = w_mod.shape
    b = c.shape[0]
    tn = 1536
    return pl.pallas_call(
        _mod_kernel,
        grid=(n_layers, n // tn),
        in_specs=[
            pl.BlockSpec((b, d), lambda l, j: (0, 0)),
            pl.BlockSpec((1, d, tn), lambda l, j: (l, 0, j)),
            pl.BlockSpec((1, 1, tn), lambda l, j: (l, 0, j)),
        ],
        out_specs=pl.BlockSpec((1, b, tn), lambda l, j: (l, 0, j)),
        out_shape=jax.ShapeDtypeStruct((n_layers, b, n), F32),
        compiler_params=_cparams(("arbitrary", "arbitrary")),
        name="adaln_mod",
    )(c, w_mod, b_mod.reshape(n_layers, 1, n))


def _win_layout_kernel(w_ref, o_ref):
    fox_cols = 3 * FOX_W + FOX_HEADS
    rwkv_cols = 3 * RWKV_W + LORA_W
    w = w_ref[...]
    o_ref[:, C_FQ:C_MQ] = w[:, 0:3 * FOX_W].astype(BF16)
    o_ref[:, C_MQ:C_RR] = w[:, fox_cols + rwkv_cols:fox_cols + rwkv_cols + 3 * MOBA_W].astype(BF16)
    o_ref[:, C_RR:C_FF] = w[:, fox_cols:fox_cols + rwkv_cols].astype(BF16)
    lane = lax.broadcasted_iota(jnp.int32, (w.shape[0], FF_PAD), 1)
    o_ref[:, C_FF:NP_COLS] = jnp.where(lane < FOX_HEADS, w[:, 3 * FOX_W:3 * FOX_W + FF_PAD], 0.0).astype(BF16)


def _win_layout_call(w_in):
    n_layers, d, n = w_in.shape
    tr = 256
    return pl.pallas_call(
        _win_layout_kernel,
        grid=(n_layers, d // tr),
        in_specs=[pl.BlockSpec((None, tr, n), lambda l, i: (l, i, 0))],
        out_specs=pl.BlockSpec((None, tr, NP_COLS), lambda l, i: (l, i, 0)),
        out_shape=jax.ShapeDtypeStruct((n_layers, d, NP_COLS), BF16),
        compiler_params=_cparams(("arbitrary", "arbitrary")),
        name="w_in_layout",
    )(w_in)


def _inproj_kernel(x_ref, mod_ref, nw_ref, w_ref, mu_ref, o_ref, prev, *, tm, nt):
    i = pl.program_id(0)
    m = mod_ref[0]
    h = _rmsnorm(x_ref[...], nw_ref[...]) * (1.0 + m[1:2, :]) + m[0:1, :]
    p = _dot(h.astype(BF16), w_ref[...])
    o_ref[:, 0:C_RR] = p[:, 0:C_RR]
    o_ref[:, C_FF:NP_COLS] = p[:, C_FF:NP_COLS]
    feat = p[:, C_RR:C_FF]

    @pl.when(i % nt == 0)
    def _():
        prev[...] = jnp.zeros(prev.shape, F32)

    rolled = pltpu.roll(feat, 1, 0)
    first = lax.broadcasted_iota(jnp.int32, (SUBLANES, C_FF - C_RR), 0) == 0
    top = jnp.where(first, prev[...], rolled[0:SUBLANES, :])
    shifted = jnp.concatenate([top, rolled[SUBLANES:, :]], axis=0)
    prev[...] = feat[tm - 1:tm, :]
    o_ref[:, C_RR:C_FF] = feat + (shifted - feat) * mu_ref[...]


def _inproj_call(xf, mod_l, norm_w, w_in_p, mu_row, layer, seq):
    rows, d = xf.shape
    tm = 512
    nt = seq // tm
    kern = functools.partial(_inproj_kernel, tm=tm, nt=nt)
    return pl.pallas_call(
        kern,
        grid=(rows // tm,),
        in_specs=[
            pl.BlockSpec((tm, d), lambda i: (i, 0)),
            pl.BlockSpec((1, 6, d), lambda i: (i // nt, 0, 0)),
            pl.BlockSpec((1, d), lambda i: (0, 0)),
            pl.BlockSpec((None, d, NP_COLS), lambda i: (layer, 0, 0)),
            pl.BlockSpec((1, C_FF - C_RR), lambda i: (0, 0)),
        ],
        out_specs=pl.BlockSpec((tm, NP_COLS), lambda i: (i, 0)),
        out_shape=jax.ShapeDtypeStruct((rows, NP_COLS), F32),
        scratch_shapes=[pltpu.VMEM((1, C_FF - C_RR), F32)],
        compiler_params=_cparams(("arbitrary",)),
        name="in_proj",
    )(xf, mod_l, norm_w, w_in_p, mu_row)


def _softmax_tiles(s, carry, v_t):
    m_new = [jnp.maximum(c[0], jnp.max(x, axis=0, keepdims=True)) for x, c in zip(s, carry)]
    p = [jnp.exp(x - m) for x, m in zip(s, m_new)]
    pv = [_dot(v, x.astype(BF16)) for v, x in zip(v_t, p)]
    out = []
    for (m, l, acc), mn, x, y in zip(carry, m_new, p, pv):
        alpha = jnp.exp(m - mn)
        out.append((mn, alpha * l + jnp.sum(x, axis=0, keepdims=True), alpha * acc + y))
    return out


def _first_tiles(s, v_t):
    m = [jnp.max(x, axis=0, keepdims=True) for x in s]
    p = [jnp.exp(x - mm) for x, mm in zip(s, m)]
    pv = [_dot(v, x.astype(BF16)) for v, x in zip(v_t, p)]
    return [(mm, jnp.sum(x, axis=0, keepdims=True), y) for mm, x, y in zip(m, p, pv)]


def _flatten(carry):
    return tuple(a for c in carry for a in c)


def _unflatten(flat):
    return [tuple(flat[3 * h:3 * h + 3]) for h in range(len(flat) // 3)]


def _store_heads(o_ref, carry):
    o_ref[...] = jnp.concatenate([acc / l for _, l, acc in carry], axis=0).T


F_COL = HEAD_DIM
N_SPLIT = 3
FOX_XW = FOX_W + N_SPLIT * FF_PAD


def _fox_selectors():
    selk = np.zeros((FOX_HEADS, FOX_XW, LANES), np.float32)
    selq = np.zeros((FOX_HEADS, LANES, FOX_XW), np.float32)
    for h in range(FOX_HEADS):
        for c in range(HEAD_DIM):
            selk[h, h * HEAD_DIM + c, c] = 1.0
            selq[h, c, h * HEAD_DIM + c] = 1.0
        for s in range(N_SPLIT):
            selk[h, FOX_W + s * FF_PAD + h, F_COL + s] = 1.0
            selq[h, F_COL + N_SPLIT + s, FOX_W + s * FF_PAD + h] = 1.0
    return jnp.asarray(selk, BF16), jnp.asarray(selq, BF16)


def _fox_kernel(q_ref, k_ref, v_ref, ff_ref, fb_ref, selk_ref, selq_ref, o_ref, ka, v_t, f3, s_buf, *, tq, tk,
                n_kv):
    qi = pl.program_id(1)
    heads = range(FOX_HEADS)

    @pl.when(qi == 0)
    def _():
        f = _seg_cumsum_rows(_log_sigmoid(ff_ref[...] + fb_ref[...]), n_kv * tk)
        hi = f.astype(BF16)
        rest = f - hi.astype(F32)
        mid = rest.astype(BF16)
        lo = (rest - mid.astype(F32)).astype(BF16)
        f3[...] = jnp.concatenate([hi, mid, lo], axis=1)
        x = jnp.concatenate([k_ref[...].astype(BF16), f3[...]], axis=1)
        lane = lax.broadcasted_iota(jnp.int32, (1, LANES), 1)
        ones = jnp.where((lane >= F_COL + N_SPLIT) & (lane < F_COL + 2 * N_SPLIT), 1.0, 0.0)
        for h in heads:
            ka[h] = (_dot(x, selk_ref[h]) + ones).astype(BF16)
        for j in range(n_kv):
            v_t[j] = v_ref[j * tk:(j + 1) * tk, :].T.astype(BF16)

    q0 = pl.multiple_of(qi * tq, tq)
    xq = jnp.concatenate([(q_ref[...] * ATTN_SCALE).astype(BF16), f3[pl.ds(q0, tq), :]], axis=1)
    sub = lax.broadcasted_iota(jnp.int32, (LANES, tq), 0)
    neg = jnp.where((sub >= F_COL) & (sub < F_COL + N_SPLIT), -1.0, 0.0)
    q_t = [(_dot_nt(selq_ref[h], xq) + neg).astype(BF16) for h in heads]
    lag = (lax.broadcasted_iota(jnp.int32, (tk, tq), 0) - lax.broadcasted_iota(jnp.int32, (tk, tq), 1))
    n_diag = tq // tk

    def logits(j):
        k0 = pl.multiple_of(j * tk, tk)
        return [_dot(ka[h, pl.ds(k0, tk), :], q_t[h]) for h in heads]

    def values(j):
        vj = v_t[j]
        return [vj[h * HEAD_DIM:(h + 1) * HEAD_DIM, :] for h in heads]

    def put(slot, s):
        for h in heads:
            s_buf[slot * FOX_HEADS + h] = s[h]

    j0 = qi * n_diag
    last = jnp.maximum(j0 - 1, 0)
    s_diag = [logits(j0 + dd) for dd in range(n_diag)]
    put(0, logits(0))
    carry = _first_tiles([jnp.where(lag <= 0, x, -jnp.inf) for x in s_diag[0]], values(j0))
    for dd in range(1, n_diag):
        carry = _softmax_tiles([jnp.where(lag <= -dd * tk, x, -jnp.inf) for x in s_diag[dd]], carry,
                               values(j0 + dd))

    def step(j, slot, carry):
        put(1 - slot, logits(jnp.minimum(j + 1, last)))
        s_cur = [s_buf[slot * FOX_HEADS + h] for h in heads]
        return _softmax_tiles(s_cur, carry, values(j))

    def body(jj, flat):
        carry = _unflatten(flat)
        for slot in range(2):
            carry = step(2 * jj + slot, slot, carry)
        return _flatten(carry)

    assert n_diag % 2 == 0
    _store_heads(o_ref, _unflatten(lax.fori_loop(0, j0 // 2, body, _flatten(carry))))


def _fox_call(p, f_bias_row, batch, seq):
    tq = 512
    tk = 256
    nq = seq // tq
    selk, selq = _fox_selectors()
    kern = functools.partial(_fox_kernel, tq=tq, tk=tk, n_kv=seq // tk)
    return pl.pallas_call(
        kern,
        grid=(batch, nq),
        in_specs=[
            pl.BlockSpec((tq, FOX_W), lambda b, i: (b * nq + i, C_FQ // FOX_W)),
            pl.BlockSpec((seq, FOX_W), lambda b, i: (b, C_FK // FOX_W)),
            pl.BlockSpec((seq, FOX_W), lambda b, i: (b, C_FV // FOX_W)),
            pl.BlockSpec((seq, FF_PAD), lambda b, i: (b, C_FF // FF_PAD)),
            pl.BlockSpec((1, FF_PAD), lambda b, i: (0, 0)),
            pl.BlockSpec(selk.shape, lambda b, i: (0, 0, 0)),
            pl.BlockSpec(selq.shape, lambda b, i: (0, 0, 0)),
        ],
        out_specs=pl.BlockSpec((tq, FOX_W), lambda b, i: (b * nq + i, 0)),
        out_shape=jax.ShapeDtypeStruct((batch * seq, FOX_W), F32),
        scratch_shapes=[
            pltpu.VMEM((FOX_HEADS, seq, LANES), BF16),
            pltpu.VMEM((seq // tk, FOX_W, tk), BF16),
            pltpu.VMEM((seq, N_SPLIT * FF_PAD), BF16),
            pltpu.VMEM((2 * FOX_HEADS, tk, tq), F32),
        ],
        compiler_params=_cparams(("arbitrary", "arbitrary")),
        name="fox_attention",
    )(p, p, p, p, f_bias_row, selk, selq)


def _moba_selectors():
    selk = np.zeros((MOBA_HEADS, MOBA_W, LANES), np.float32)
    for h in range(MOBA_HEADS):
        for c in range(HEAD_DIM):
            selk[h, h * HEAD_DIM + c, c] = 1.0
    return jnp.asarray(selk, BF16), jnp.asarray(selk.transpose(0, 2, 1), BF16)


def _moba_kernel(q_ref, k_ref, v_ref, selk_ref, selq_ref, o_ref, ka, v_t, kmean, s_buf, *, blk, n_kb):
    own = pl.program_id(1)
    heads = range(MOBA_HEADS)

    @pl.when(own == 0)
    def _():
        kf = k_ref[...]
        kbf = kf.astype(BF16)
        for h in heads:
            ka[h] = _dot(kbf, selk_ref[h]).astype(BF16)
        kmean[...] = jnp.zeros(kmean.shape, F32)
        for n in range(n_kb):
            kmean[n:n + 1, :] = jnp.mean(kf[n * blk:(n + 1) * blk, :], axis=0, keepdims=True)
            v_t[n] = v_ref[n * blk:(n + 1) * blk, :].T.astype(BF16)

    qf = q_ref[...]
    qs = (qf * ATTN_SCALE).astype(BF16)
    q_t = [_dot_nt(selq_ref[h], qs).astype(BF16) for h in heads]
    sub = lax.broadcasted_iota(jnp.int32, (SUBLANES, blk), 0)
    past = sub < own
    causal = (lax.broadcasted_iota(jnp.int32, (blk, blk), 0) <= lax.broadcasted_iota(jnp.int32, (blk, blk), 1))

    sel = []
    for h in heads:
        hs = slice(h * HEAD_DIM, (h + 1) * HEAD_DIM)
        gate = _dot_nt_hi(kmean[:, hs], qf[:, hs])
        selm = jnp.zeros((SUBLANES, blk), F32)
        for n in range(n_kb):
            gn = gate[n:n + 1, :]
            beats = past & ((gate > gn) | ((gate == gn) & (sub < n)))
            rank = jnp.sum(beats.astype(F32), axis=0, keepdims=True)
            selm = jnp.where(sub == n, (rank < MOBA_TOPK).astype(F32), selm)
        sel.append(jnp.where(past, selm, 0.0))

    def logits(j):
        k0 = pl.multiple_of(j * blk, blk)
        return [_dot(ka[h, pl.ds(k0, blk), :], q_t[h]) for h in heads]

    def values(j):
        vj = v_t[j]
        return [vj[h * HEAD_DIM:(h + 1) * HEAD_DIM, :] for h in heads]

    def put(slot, s):
        for h in heads:
            s_buf[slot * MOBA_HEADS + h] = s[h]

    def step(j, slot, carry, live, prefetch):
        if prefetch:
            put(1 - slot, logits(jnp.minimum(j + 1, last)))
        picked = [(jnp.sum(jnp.where(sub == j, sel[h], 0.0), axis=0, keepdims=True) > 0.5) & live for h in heads]
        s = [jnp.where(pk, s_buf[slot * MOBA_HEADS + h], -jnp.inf) for h, pk in zip(heads, picked)]
        return _softmax_tiles(s, carry, values(j))

    last = jnp.maximum(own - 1, 0)
    s_own = logits(own)
    put(0, logits(0))
    carry = _first_tiles([jnp.where(causal, x, -jnp.inf) for x in s_own], values(own))

    def body(jj, flat):
        carry = _unflatten(flat)
        for slot in range(2):
            carry = step(2 * jj + slot, slot, carry, True, True)
        return _flatten(carry)

    carry = _unflatten(lax.fori_loop(0, own // 2, body, _flatten(carry)))
    carry = step(last, 0, carry, own % 2 == 1, False)
    _store_heads(o_ref, carry)


def _moba_call(p, batch, seq):
    blk = MOBA_BLOCK
    n_kb = seq // blk
    assert n_kb <= SUBLANES
    selk, selq = _moba_selectors()
    kern = functools.partial(_moba_kernel, blk=blk, n_kb=n_kb)
    return pl.pallas_call(
        kern,
        grid=(batch, n_kb),
        in_specs=[
            pl.BlockSpec((blk, MOBA_W), lambda b, i: (b * n_kb + i, C_MQ // MOBA_W)),
            pl.BlockSpec((seq, MOBA_W), lambda b, i: (b, C_MK // MOBA_W)),
            pl.BlockSpec((seq, MOBA_W), lambda b, i: (b, C_MV // MOBA_W)),
            pl.BlockSpec(selk.shape, lambda b, i: (0, 0, 0)),
            pl.BlockSpec(selq.shape, lambda b, i: (0, 0, 0)),
        ],
        out_specs=pl.BlockSpec((blk, MOBA_W), lambda b, i: (b * n_kb + i, 0)),
        out_shape=jax.ShapeDtypeStruct((batch * seq, MOBA_W), F32),
        scratch_shapes=[
            pltpu.VMEM((MOBA_HEADS, seq, LANES), BF16),
            pltpu.VMEM((n_kb, MOBA_W, blk), BF16),
            pltpu.VMEM((SUBLANES, MOBA_W), F32),
            pltpu.VMEM((2 * MOBA_HEADS, blk, blk), F32),
        ],
        compiler_params=_cparams(("arbitrary", "arbitrary")),
        name="moba_attention",
    )(p, p, p, selk, selq)


DECAY_SCALE = float(np.exp(-0.5))


def _rwkv_kernel(r_ref, k_ref, v_ref, lo_ref, w0, w2, a0, a2, g2, kkw, kaw, rkw, lnw, lnb, o_ref,
                 state, a_t, r_t, k_t, b_t, k_b, b_b, v_s, w_c, y_s, *, tm):
    i = pl.program_id(1)
    ch = RWKV_CHUNK
    n_ch = tm // ch

    @pl.when(i == 0)
    def _():
        state[...] = jnp.zeros(state.shape, F32)

    r = r_ref[...]
    k = k_ref[...]
    v = v_ref[...]
    lo = lo_ref[...]
    w_lo = lo[:, 0:DECAY_LORA]
    a_lo = lo[:, DECAY_LORA:DECAY_LORA + AAA_LORA]
    g_lo = lo[:, DECAY_LORA + AAA_LORA:LORA_W]

    lw = -DECAY_SCALE * _sigmoid(w0[...] + _dot3(jnp.tanh(w_lo), w2[...]))
    eta = _sigmoid(a0[...] + _dot3(a_lo, a2[...]))
    gate = _dot(_sigmoid(g_lo).astype(BF16), g2[...])

    gi = lax.broadcasted_iota(jnp.int32, (RWKV_W, RWKV_W), 0) // HEAD_DIM
    gj = lax.broadcasted_iota(jnp.int32, (RWKV_W, RWKV_W), 1) // HEAD_DIM
    group = (gi == gj).astype(BF16)

    kk = k * kkw[...]
    kk = kk * jnp.minimum(lax.rsqrt(_dot_split_lhs(kk * kk, group)), 1e12)
    kp = k * (1.0 + (eta - 1.0) * kaw[...])
    bb = kk * eta

    lc = _seg_cumsum_rows(lw, ch)
    a_t[...] = -kk * jnp.exp(lc - lw)
    r_t[...] = r * jnp.exp(lc)
    einv = jnp.exp(-lc)
    k_t[...] = kp * einv
    b_t[...] = bb * einv
    v_s[...] = v
    for c in range(n_ch):
        rows = slice(c * ch, (c + 1) * ch)
        last = lc[(c + 1) * ch - 1:(c + 1) * ch, :]
        e = jnp.exp(last - lc[rows, :])
        k_b[rows, :] = kp[rows, :] * e
        b_b[rows, :] = bb[rows, :] * e
        w_c[c] = jnp.broadcast_to(jnp.exp(last), (8, RWKV_W))

    ti = lax.broadcasted_iota(jnp.int32, (ch, ch), 0)
    tj = lax.broadcasted_iota(jnp.int32, (ch, ch), 1)
    strict = tj < ti
    incl = tj <= ti

    chains = [(c, h) for c in range(n_ch) for h in range(RWKV_HEADS)]
    ids = range(len(chains))

    def tile(ref):
        return [ref[c * ch:(c + 1) * ch, h * HEAD_DIM:(h + 1) * HEAD_DIM] for c, h in chains]

    at, rt, vv, kbar, bbar = tile(a_t), tile(r_t), tile(v_s), tile(k_b), tile(b_b)
    btl, ktl = tile(b_t), tile(k_t)
    pad = jnp.zeros((LANES - ch, HEAD_DIM), BF16)
    m4 = [_dot_nt(jnp.concatenate([at[i], rt[i]], axis=0).astype(BF16),
                  jnp.concatenate([btl[i].astype(BF16), pad, ktl[i].astype(BF16), pad], axis=0)) for i in ids]
    a_ab = [jnp.where(strict, m4[i][0:ch, 0:ch], 0.0) for i in ids]
    a_ak = [jnp.where(strict, m4[i][0:ch, LANES:LANES + ch], 0.0) for i in ids]
    a_rb = [jnp.where(incl, m4[i][ch:2 * ch, 0:ch], 0.0) for i in ids]
    a_rk = [jnp.where(incl, m4[i][ch:2 * ch, LANES:LANES + ch], 0.0) for i in ids]
    avk = [_mm(jnp.concatenate([a_ak[i], a_rk[i]], axis=0), vv[i]) for i in ids]
    pw = a_ab
    tx = [jnp.concatenate([at[i], avk[i][0:ch, :]], axis=1) for i in ids]
    span = 1
    while 2 * span < ch:
        x = [_mm(pw[i], jnp.concatenate([tx[i], pw[i]], axis=1)) for i in ids]
        pw = [x[i][:, 2 * ch:3 * ch] for i in ids]
        tx = [tx[i] + x[i][:, 0:2 * ch] for i in ids]
        span *= 2
    tx = [tx[i] + _mm(pw[i], tx[i]) for i in ids]
    ry = [_mm(a_rb[i], tx[i]) for i in ids]
    rhat = [rt[i] + ry[i][:, 0:ch] for i in ids]
    yhat = [avk[i][ch:2 * ch, :] + ry[i][:, ch:2 * ch] for i in ids]
    z = [_mm(tx[i][:, 0:ch].T, bbar[i]) for i in ids]
    kv = [_mm(jnp.concatenate([vv[i], tx[i][:, ch:2 * ch]], axis=0).T,
              jnp.concatenate([kbar[i], bbar[i]], axis=0)) for i in ids]

    s_cur = [state[h] for h in range(RWKV_HEADS)]
    for c in range(n_ch):
        base = c * RWKV_HEADS
        ys = [_mm_nt(rhat[base + h], s_cur[h]) for h in range(RWKV_HEADS)]
        sz = [_mm(s_cur[h], z[base + h]) for h in range(RWKV_HEADS)]
        for h in range(RWKV_HEADS):
            hs = slice(h * HEAD_DIM, (h + 1) * HEAD_DIM)
            y_s[c * ch:(c + 1) * ch, hs] = ys[h] + yhat[base + h]
            s_cur[h] = s_cur[h] * w_c[c][0:1, hs] + sz[h] + kv[base + h]
    for h in range(RWKV_HEADS):
        state[h] = s_cur[h]

    y = y_s[...]
    inv_d = 1.0 / HEAD_DIM
    mean = _dot_split_lhs(y, group) * inv_d
    d = y - mean
    var = _dot_split_lhs(d * d, group) * inv_d
    yn = d * lax.rsqrt(var + GN_EPS) * lnw[...] + lnb[...]
    bonus = _dot_split_lhs(r * kp * rkw[...], group) * v
    o_ref[...] = (yn + bonus) * gate


def _rwkv_call(p, prm, batch, seq):
    tm = 256
    nt = seq // tm
    kern = functools.partial(_rwkv_kernel, tm=tm)

    def rows(width, cstart):
        return pl.BlockSpec((tm, width), lambda b, i: (b * nt + i, cstart // width))

    def full(a):
        return pl.BlockSpec(a.shape, lambda b, i: (0,) * a.ndim)

    params = [prm[n] for n in ("w0", "w2", "a0", "a2", "g2", "k_k", "k_a", "r_k", "ln_w", "ln_b")]
    big = pltpu.VMEM((tm, RWKV_W), F32)
    return pl.pallas_call(
        kern,
        grid=(batch, nt),
        in_specs=[rows(RWKV_W, C_RR), rows(RWKV_W, C_RK), rows(RWKV_W, C_RV), rows(LORA_W, C_LORA)]
        + [full(a) for a in params],
        out_specs=pl.BlockSpec((tm, RWKV_W), lambda b, i: (b * nt + i, 0)),
        out_shape=jax.ShapeDtypeStruct((batch * seq, RWKV_W), F32),
        scratch_shapes=[
            pltpu.VMEM((RWKV_HEADS, HEAD_DIM, HEAD_DIM), F32),
            big, big, big, big, big, big, big,
            pltpu.VMEM((tm // RWKV_CHUNK, 8, RWKV_W), F32),
            big,
        ],
        compiler_params=_cparams(("arbitrary", "arbitrary")),
        name="rwkv7_mix",
    )(p, p, p, p, *params)


def _outproj_kernel(yf_ref, yr_ref, ym_ref, x_ref, mod_ref, w_ref, o_ref):
    z = _dot(yf_ref[...].astype(BF16), w_ref[0:FOX_W, :])
    z = z + _dot(yr_ref[...].astype(BF16), w_ref[FOX_W:FOX_W + RWKV_W, :])
    z = z + _dot(ym_ref[...].astype(BF16), w_ref[FOX_W + RWKV_W:, :])
    o_ref[...] = x_ref[...] + mod_ref[0][2:3, :] * z


def _outproj_call(yf, yr, ym, xf, mod_l, w_out_b, layer, seq):
    rows, d = xf.shape
    tm = 512
    nt = seq // tm
    return pl.pallas_call(
        _outproj_kernel,
        grid=(rows // tm,),
        in_specs=[
            pl.BlockSpec((tm, FOX_W), lambda i: (i, 0)),
            pl.BlockSpec((tm, RWKV_W), lambda i: (i, 0)),
            pl.BlockSpec((tm, MOBA_W), lambda i: (i, 0)),
            pl.BlockSpec((tm, d), lambda i: (i, 0)),
            pl.BlockSpec((1, 6, d), lambda i: (i // nt, 0, 0)),
            pl.BlockSpec((None,) + w_out_b.shape[1:], lambda i: (layer, 0, 0)),
        ],
        out_specs=pl.BlockSpec((tm, d), lambda i: (i, 0)),
        out_shape=jax.ShapeDtypeStruct((rows, d), F32),
        compiler_params=_cparams(("arbitrary",)),
        name="out_proj",
    )(yf, yr, ym, xf, mod_l, w_out_b)


FFN_HALO = 16


def _ffn_kernel(x_ref, xp_ref, mod_ref, nw_ref, wg_ref, wv_ref, cwg_ref, cwv_ref, cbg_ref, cbv_ref, wd_ref,
                nf_ref, o_ref, hext, acc, *, tm, nt, n_ff, final):
    i = pl.program_id(0)
    j = pl.program_id(1)
    m = mod_ref[0]

    @pl.when(j == 0)
    def _():
        def pre(x):
            return _rmsnorm(x, nw_ref[...]) * (1.0 + m[4:5, :]) + m[3:4, :]

        keep = jnp.where(i % nt == 0, 0.0, 1.0)
        hext[0:FFN_HALO, :] = (pre(xp_ref[...]) * keep).astype(BF16)
        hext[FFN_HALO:, :] = pre(x_ref[...]).astype(BF16)
        acc[...] = jnp.zeros(acc.shape, F32)

    he = hext[...]

    def conv(u, cw, cb):
        return (cb[...] + cw[0:1, :] * pltpu.roll(u, 2, 0)[FFN_HALO:, :]
                + cw[1:2, :] * pltpu.roll(u, 1, 0)[FFN_HALO:, :] + cw[2:3, :] * u[FFN_HALO:, :])

    ug = conv(_dot(he, wg_ref[...]), cwg_ref, cbg_ref)
    uv = conv(_dot(he, wv_ref[...]), cwv_ref, cbv_ref)
    act = ug * _sigmoid(ug) * uv
    acc[...] += _dot(act.astype(BF16), wd_ref[...])

    @pl.when(j == n_ff - 1)
    def _():
        out = x_ref[...] + m[5:6, :] * acc[...]
        if final:
            out = _rmsnorm(out, nf_ref[...])
        o_ref[...] = out


def _ffn_call(xf, mod_l, norm_w, w_up_b, conv_w, conv_b, w_down_b, norm_final, layer, seq, final):
    rows, d = xf.shape
    d_ff = w_down_b.shape[1]
    tm = 512
    n_ff = 2
    tf = d_ff // n_ff
    nt = seq // tm
    hb = tm // FFN_HALO
    kern = functools.partial(_ffn_kernel, tm=tm, nt=nt, n_ff=n_ff, final=final)
    return pl.pallas_call(
        kern,
        grid=(rows // tm, n_ff),
        in_specs=[
            pl.BlockSpec((tm, d), lambda i, j: (i, 0)),
            pl.BlockSpec((FFN_HALO, d), lambda i, j: (jnp.maximum(i * hb - 1, 0), 0)),
            pl.BlockSpec((1, 6, d), lambda i, j: (i // nt, 0, 0)),
            pl.BlockSpec((1, d), lambda i, j: (0, 0)),
            pl.BlockSpec((None, d, tf), lambda i, j: (layer, 0, j)),
            pl.BlockSpec((None, d, tf), lambda i, j: (layer, 0, n_ff + j)),
            pl.BlockSpec((CONV_W, tf), lambda i, j: (0, j)),
            pl.BlockSpec((CONV_W, tf), lambda i, j: (0, n_ff + j)),
            pl.BlockSpec((1, tf), lambda i, j: (0, j)),
            pl.BlockSpec((1, tf), lambda i, j: (0, n_ff + j)),
            pl.BlockSpec((None, tf, d), lambda i, j: (layer, j, 0)),
            pl.BlockSpec((1, d), lambda i, j: (0, 0)),
        ],
        out_specs=pl.BlockSpec((tm, d), lambda i, j: (i, 0)),
        out_shape=jax.ShapeDtypeStruct((rows, d), F32),
        scratch_shapes=[pltpu.VMEM((tm + FFN_HALO, d), BF16), pltpu.VMEM((tm, d), F32)],
        compiler_params=_cparams(("arbitrary", "arbitrary")),
        name="conv_ffn",
    )(xf, xf, mod_l, norm_w, w_up_b, w_up_b, conv_w, conv_w, conv_b, conv_b, w_down_b, norm_final)


def kernel(x, c, w_mod, b_mod, norm_mix, w_in, fox_f_bias, rwkv_mu, rwkv_w0, rwkv_w2, rwkv_a0, rwkv_a2,
           rwkv_g2, rwkv_k_k, rwkv_k_a, rwkv_r_k, rwkv_ln_w, rwkv_ln_b, w_out, norm_ffn, w_up, conv_w,
           conv_b, w_down, norm_final):
    batch, seq, d = x.shape
    n_layers = w_mod.shape[0]

    mod = _mod_call(c, w_mod, b_mod).reshape(n_layers, batch, 6, d)
    w_in_p = _win_layout_call(w_in)
    f_bias = jnp.pad(fox_f_bias, ((0, 0), (0, FF_PAD - FOX_HEADS)))
    w_out_b = w_out.astype(BF16)
    w_up_b = w_up.astype(BF16)
    w_down_b = w_down.astype(BF16)
    g2_b = rwkv_g2.astype(BF16)

    xf = x.reshape(batch * seq, d)
    for l in range(n_layers):
        row = lambda a: a[l].reshape(1, -1)
        prm = {
            "w0": row(rwkv_w0), "w2": rwkv_w2[l], "a0": row(rwkv_a0), "a2": rwkv_a2[l], "g2": g2_b[l],
            "k_k": row(rwkv_k_k), "k_a": row(rwkv_k_a), "r_k": row(rwkv_r_k), "ln_w": row(rwkv_ln_w),
            "ln_b": row(rwkv_ln_b),
        }
        p = _inproj_call(xf, mod[l], row(norm_mix), w_in_p, row(rwkv_mu), l, seq)
        y_fox = _fox_call(p, f_bias[l:l + 1], batch, seq)
        y_moba = _moba_call(p, batch, seq)
        y_rwkv = _rwkv_call(p, prm, batch, seq)
        xf = _outproj_call(y_fox, y_rwkv, y_moba, xf, mod[l], w_out_b, l, seq)
        xf = _ffn_call(xf, mod[l], row(norm_ffn), w_up_b, conv_w[l], conv_b[l].reshape(1, -1), w_down_b,
                       norm_final.reshape(1, -1), l, seq, final=(l == n_layers - 1))
    return xf.reshape(batch, seq, d)
```

```python
import functools

import jax
import jax.numpy as jnp
import numpy as np
from jax import lax
from jax.experimental import pallas as pl
from jax.experimental.pallas import tpu as pltpu

F32 = jnp.float32
BF16 = jnp.bfloat16
HI = lax.Precision.HIGHEST

HEAD_DIM = 64
FOX_HEADS = 4
RWKV_HEADS = 8
MOBA_HEADS = 4
FOX_W = FOX_HEADS * HEAD_DIM
RWKV_W = RWKV_HEADS * HEAD_DIM
MOBA_W = MOBA_HEADS * HEAD_DIM
DECAY_LORA = 64
AAA_LORA = 64
GATE_LORA = 128
LORA_W = DECAY_LORA + AAA_LORA + GATE_LORA
MOBA_BLOCK = 256
MOBA_TOPK = 3
CONV_W = 3
NORM_EPS = 1e-6
GN_EPS = 64e-5
LOG2E = float(np.log2(np.e))
ATTN_SCALE = HEAD_DIM ** -0.5 * LOG2E

C_FQ, C_FK, C_FV = 0, 256, 512
C_MQ, C_MK, C_MV = 768, 1024, 1280
C_RR, C_RK, C_RV = 1536, 2048, 2560
C_LORA = 3072
C_FF = 3328
FF_PAD = 128
NP_COLS = C_FF + FF_PAD

RWKV_CHUNK = 64
LANES = 128
SUBLANES = 8
VMEM_LIMIT = 56 * 1024 * 1024


def _cparams(sem):
    return pltpu.CompilerParams(dimension_semantics=sem, vmem_limit_bytes=VMEM_LIMIT)


def _dot(a, b):
    return jnp.dot(a, b, preferred_element_type=F32)


def _dot_hi(a, b):
    return jnp.dot(a, b, precision=HI, preferred_element_type=F32)


def _dot_nt(a, b):
    return lax.dot_general(a, b, (((1,), (1,)), ((), ())), preferred_element_type=F32)


def _dot_nt_hi(a, b):
    return lax.dot_general(a, b, (((1,), (1,)), ((), ())), precision=HI, preferred_element_type=F32)


def _mm(a, b):
    return jnp.dot(a.astype(BF16), b.astype(BF16), preferred_element_type=F32)


def _mm_nt(a, b):
    return _dot_nt(a.astype(BF16), b.astype(BF16))


def _split2(x):
    hi = x.astype(BF16)
    return hi, (x - hi.astype(F32)).astype(BF16)


def _dot_split_lhs(x, w_bf16):
    m = x.shape[0]
    r = _dot(jnp.concatenate(_split2(x), axis=0), w_bf16)
    return r[0:m, :] + r[m:2 * m, :]


def _dot3(a, b):
    m = a.shape[0]
    a_hi, a_lo = _split2(a)
    b_hi, b_lo = _split2(b)
    r = _dot(jnp.concatenate([a_hi, a_lo], axis=0), b_hi)
    return r[0:m, :] + r[m:2 * m, :] + _dot(a_hi, b_lo)


def _sigmoid(x):
    return 1.0 / (1.0 + jnp.exp(-x))


def _log_sigmoid(x):
    return jnp.minimum(x, 0.0) - jnp.log1p(jnp.exp(-jnp.abs(x)))


def _rmsnorm(x, w):
    ms = jnp.mean(x * x, axis=-1, keepdims=True)
    return x * lax.rsqrt(ms + NORM_EPS) * w


def _seg_cumsum_rows(x, seg):
    row = lax.broadcasted_iota(jnp.int32, x.shape, 0) & (seg - 1)
    s = 1
    while s < seg:
        x = x + jnp.where(row >= s, pltpu.roll(x, s, 0), 0.0)
        s *= 2
    return x


def _mod_kernel(c_ref, w_ref, b_ref, o_ref):
    c = c_ref[...]
    o_ref[0] = _dot_hi(c * _sigmoid(c), w_ref[0]) + b_ref[0]


def _mod_call(c, w_mod, b_mod):
    n_layers, d, n = w_mod.shape
    b = c.shape[0]
    tn = 1536
    return pl.pallas_call(
        _mod_kernel,
        grid=(n_layers, n // tn),
        in_specs=[
            pl.BlockSpec((b, d), lambda l, j: (0, 0)),
            pl.BlockSpec((1, d, tn), lambda l, j: (l, 0, j)),
            pl.BlockSpec((1, 1, tn), lambda l, j: (l, 0, j)),
        ],
        out_specs=pl.BlockSpec((1, b, tn), lambda l, j: (l, 0, j)),
        out_shape=jax.ShapeDtypeStruct((n_layers, b, n), F32),
        compiler_params=_cparams(("arbitrary", "arbitrary")),
        name="adaln_mod",
    )(c, w_mod, b_mod.reshape(n_layers, 1, n))


def _win_layout_kernel(w_ref, o_ref):
    fox_cols = 3 * FOX_W + FOX_HEADS
    rwkv_cols = 3 * RWKV_W + LORA_W
    w = w_ref[...]
    o_ref[:, C_FQ:C_MQ] = w[:, 0:3 * FOX_W].astype(BF16)
    o_ref[:, C_MQ:C_RR] = w[:, fox_cols + rwkv_cols:fox_cols + rwkv_cols + 3 * MOBA_W].astype(BF16)
    o_ref[:, C_RR:C_FF] = w[:, fox_cols:fox_cols + rwkv_cols].astype(BF16)
    lane = lax.broadcasted_iota(jnp.int32, (w.shape[0], FF_PAD), 1)
    o_ref[:, C_FF:NP_COLS] = jnp.where(lane < FOX_HEADS, w[:, 3 * FOX_W:3 * FOX_W + FF_PAD], 0.0).astype(BF16)


def _win_layout_call(w_in):
    n_layers, d, n = w_in.shape
    tr = 256
    return pl.pallas_call(
        _win_layout_kernel,
        grid=(n_layers, d // tr),
        in_specs=[pl.BlockSpec((None, tr, n), lambda l, i: (l, i, 0))],
        out_specs=pl.BlockSpec((None, tr, NP_COLS), lambda l, i: (l, i, 0)),
        out_shape=jax.ShapeDtypeStruct((n_layers, d, NP_COLS), BF16),
        compiler_params=_cparams(("arbitrary", "arbitrary")),
        name="w_in_layout",
    )(w_in)


def _inproj_kernel(x_ref, mod_ref, nw_ref, w_ref, mu_ref, o_ref, prev, *, tm, nt):
    i = pl.program_id(0)
    m = mod_ref[0]
    h = _rmsnorm(x_ref[...], nw_ref[...]) * (1.0 + m[1:2, :]) + m[0:1, :]
    p = _dot(h.astype(BF16), w_ref[...])
    o_ref[:, 0:C_RR] = p[:, 0:C_RR]
    o_ref[:, C_FF:NP_COLS] = p[:, C_FF:NP_COLS]
    feat = p[:, C_RR:C_FF]

    @pl.when(i % nt == 0)
    def _():
        prev[...] = jnp.zeros(prev.shape, F32)

    rolled = pltpu.roll(feat, 1, 0)
    first = lax.broadcasted_iota(jnp.int32, (SUBLANES, C_FF - C_RR), 0) == 0
    top = jnp.where(first, prev[...], rolled[0:SUBLANES, :])
    shifted = jnp.concatenate([top, rolled[SUBLANES:, :]], axis=0)
    prev[...] = feat[tm - 1:tm, :]
    o_ref[:, C_RR:C_FF] = feat + (shifted - feat) * mu_ref[...]


def _inproj_call(xf, mod_l, norm_w, w_in_p, mu_row, layer, seq):
    rows, d = xf.shape
    tm = 512
    nt = seq // tm
    kern = functools.partial(_inproj_kernel, tm=tm, nt=nt)
    return pl.pallas_call(
        kern,
        grid=(rows // tm,),
        in_specs=[
            pl.BlockSpec((tm, d), lambda i: (i, 0)),
            pl.BlockSpec((1, 6, d), lambda i: (i // nt, 0, 0)),
            pl.BlockSpec((1, d), lambda i: (0, 0)),
            pl.BlockSpec((None, d, NP_COLS), lambda i: (layer, 0, 0)),
            pl.BlockSpec((1, C_FF - C_RR), lambda i: (0, 0)),
        ],
        out_specs=pl.BlockSpec((tm, NP_COLS), lambda i: (i, 0)),
        out_shape=jax.ShapeDtypeStruct((rows, NP_COLS), F32),
        scratch_shapes=[pltpu.VMEM((1, C_FF - C_RR), F32)],
        compiler_params=_cparams(("arbitrary",)),
        name="in_proj",
    )(xf, mod_l, norm_w, w_in_p, mu_row)


def _softmax_tiles(s, carry, v_t):
    m_new = [jnp.maximum(c[0], jnp.max(x, axis=0, keepdims=True)) for x, c in zip(s, carry)]
    p = [jnp.exp2(x - m) for x, m in zip(s, m_new)]
    pv = [_dot(v, x.astype(BF16)) for v, x in zip(v_t, p)]
    out = []
    for (m, l, acc), mn, x, y in zip(carry, m_new, p, pv):
        alpha = jnp.exp2(m - mn)
        out.append((mn, alpha * l + jnp.sum(x, axis=0, keepdims=True), alpha * acc + y))
    return out


def _first_tiles(s, v_t):
    m = [jnp.max(x, axis=0, keepdims=True) for x in s]
    p = [jnp.exp2(x - mm) for x, mm in zip(s, m)]
    pv = [_dot(v, x.astype(BF16)) for v, x in zip(v_t, p)]
    return [(mm, jnp.sum(x, axis=0, keepdims=True), y) for mm, x, y in zip(m, p, pv)]


def _flatten(carry):
    return tuple(a for c in carry for a in c)


def _unflatten(flat):
    return [tuple(flat[3 * h:3 * h + 3]) for h in range(len(flat) // 3)]


def _store_heads(o_ref, carry):
    o_ref[...] = jnp.concatenate([acc / l for _, l, acc in carry], axis=0).T


F_COL = HEAD_DIM
N_SPLIT = 3
FOX_XW = FOX_W + N_SPLIT * FF_PAD


def _fox_selectors():
    selk = np.zeros((FOX_HEADS, FOX_XW, LANES), np.float32)
    selq = np.zeros((FOX_HEADS, LANES, FOX_XW), np.float32)
    for h in range(FOX_HEADS):
        for c in range(HEAD_DIM):
            selk[h, h * HEAD_DIM + c, c] = 1.0
            selq[h, c, h * HEAD_DIM + c] = 1.0
        for s in range(N_SPLIT):
            selk[h, FOX_W + s * FF_PAD + h, F_COL + s] = 1.0
            selq[h, F_COL + N_SPLIT + s, FOX_W + s * FF_PAD + h] = 1.0
    return jnp.asarray(selk, BF16), jnp.asarray(selq, BF16)


def _fox_kernel(q_ref, k_ref, v_ref, ff_ref, fb_ref, selk_ref, selq_ref, o_ref, ka, v_t, f3, s_buf, *, tq, tk,
                n_kv):
    qi = pl.program_id(1)
    heads = range(FOX_HEADS)

    @pl.when(qi == 0)
    def _():
        f = _seg_cumsum_rows(_log_sigmoid(ff_ref[...] + fb_ref[...]), n_kv * tk) * LOG2E
        hi = f.astype(BF16)
        rest = f - hi.astype(F32)
        mid = rest.astype(BF16)
        lo = (rest - mid.astype(F32)).astype(BF16)
        f3[...] = jnp.concatenate([hi, mid, lo], axis=1)
        x = jnp.concatenate([k_ref[...].astype(BF16), f3[...]], axis=1)
        lane = lax.broadcasted_iota(jnp.int32, (1, LANES), 1)
        ones = jnp.where((lane >= F_COL + N_SPLIT) & (lane < F_COL + 2 * N_SPLIT), 1.0, 0.0)
        for h in heads:
            ka[h] = (_dot(x, selk_ref[h]) + ones).astype(BF16)
        for j in range(n_kv):
            v_t[j] = v_ref[j * tk:(j + 1) * tk, :].T.astype(BF16)

    q0 = pl.multiple_of(qi * tq, tq)
    xq = jnp.concatenate([(q_ref[...] * ATTN_SCALE).astype(BF16), f3[pl.ds(q0, tq), :]], axis=1)
    sub = lax.broadcasted_iota(jnp.int32, (LANES, tq), 0)
    neg = jnp.where((sub >= F_COL) & (sub < F_COL + N_SPLIT), -1.0, 0.0)
    q_t = [(_dot_nt(selq_ref[h], xq) + neg).astype(BF16) for h in heads]
    lag = (lax.broadcasted_iota(jnp.int32, (tk, tq), 0) - lax.broadcasted_iota(jnp.int32, (tk, tq), 1))
    n_diag = tq // tk

    def logits(j):
        k0 = pl.multiple_of(j * tk, tk)
        return [_dot(ka[h, pl.ds(k0, tk), :], q_t[h]) for h in heads]

    def values(j):
        vj = v_t[j]
        return [vj[h * HEAD_DIM:(h + 1) * HEAD_DIM, :] for h in heads]

    def put(slot, s):
        for h in heads:
            s_buf[slot * FOX_HEADS + h] = s[h]

    j0 = qi * n_diag
    last = jnp.maximum(j0 - 1, 0)
    s_diag = [logits(j0 + dd) for dd in range(n_diag)]
    put(0, logits(0))
    carry = _first_tiles([jnp.where(lag <= 0, x, -jnp.inf) for x in s_diag[0]], values(j0))
    for dd in range(1, n_diag):
        carry = _softmax_tiles([jnp.where(lag <= -dd * tk, x, -jnp.inf) for x in s_diag[dd]], carry,
                               values(j0 + dd))

    def step(j, slot, carry):
        put(1 - slot, logits(jnp.minimum(j + 1, last)))
        s_cur = [s_buf[slot * FOX_HEADS + h] for h in heads]
        return _softmax_tiles(s_cur, carry, values(j))

    def body(jj, flat):
        carry = _unflatten(flat)
        for slot in range(2):
            carry = step(2 * jj + slot, slot, carry)
        return _flatten(carry)

    assert n_diag % 2 == 0
    _store_heads(o_ref, _unflatten(lax.fori_loop(0, j0 // 2, body, _flatten(carry))))


def _fox_call(p, f_bias_row, batch, seq):
    tq = 512
    tk = 256
    nq = seq // tq
    selk, selq = _fox_selectors()
    kern = functools.partial(_fox_kernel, tq=tq, tk=tk, n_kv=seq // tk)
    return pl.pallas_call(
        kern,
        grid=(batch, nq),
        in_specs=[
            pl.BlockSpec((tq, FOX_W), lambda b, i: (b * nq + i, C_FQ // FOX_W)),
            pl.BlockSpec((seq, FOX_W), lambda b, i: (b, C_FK // FOX_W)),
            pl.BlockSpec((seq, FOX_W), lambda b, i: (b, C_FV // FOX_W)),
            pl.BlockSpec((seq, FF_PAD), lambda b, i: (b, C_FF // FF_PAD)),
            pl.BlockSpec((1, FF_PAD), lambda b, i: (0, 0)),
            pl.BlockSpec(selk.shape, lambda b, i: (0, 0, 0)),
            pl.BlockSpec(selq.shape, lambda b, i: (0, 0, 0)),
        ],
        out_specs=pl.BlockSpec((tq, FOX_W), lambda b, i: (b * nq + i, 0)),
        out_shape=jax.ShapeDtypeStruct((batch * seq, FOX_W), F32),
        scratch_shapes=[
            pltpu.VMEM((FOX_HEADS, seq, LANES), BF16),
            pltpu.VMEM((seq // tk, FOX_W, tk), BF16),
            pltpu.VMEM((seq, N_SPLIT * FF_PAD), BF16),
            pltpu.VMEM((2 * FOX_HEADS, tk, tq), F32),
        ],
        compiler_params=_cparams(("arbitrary", "arbitrary")),
        name="fox_attention",
    )(p, p, p, p, f_bias_row, selk, selq)


def _moba_selectors():
    selk = np.zeros((MOBA_HEADS, MOBA_W, LANES), np.float32)
    for h in range(MOBA_HEADS):
        for c in range(HEAD_DIM):
            selk[h, h * HEAD_DIM + c, c] = 1.0
    return jnp.asarray(selk, BF16), jnp.asarray(selk.transpose(0, 2, 1), BF16)


def _moba_kernel(q_ref, k_ref, v_ref, selk_ref, selq_ref, o_ref, ka, v_t, kmean, s_buf, *, blk, n_kb):
    own = pl.program_id(1)
    heads = range(MOBA_HEADS)

    @pl.when(own == 0)
    def _():
        kf = k_ref[...]
        kbf = kf.astype(BF16)
        for h in heads:
            ka[h] = _dot(kbf, selk_ref[h]).astype(BF16)
        kmean[...] = jnp.zeros(kmean.shape, F32)
        for n in range(n_kb):
            kmean[n:n + 1, :] = jnp.mean(kf[n * blk:(n + 1) * blk, :], axis=0, keepdims=True)
            v_t[n] = v_ref[n * blk:(n + 1) * blk, :].T.astype(BF16)

    qf = q_ref[...]
    qs = (qf * ATTN_SCALE).astype(BF16)
    q_t = [_dot_nt(selq_ref[h], qs).astype(BF16) for h in heads]
    sub = lax.broadcasted_iota(jnp.int32, (SUBLANES, blk), 0)
    past = sub < own
    causal = (lax.broadcasted_iota(jnp.int32, (blk, blk), 0) <= lax.broadcasted_iota(jnp.int32, (blk, blk), 1))

    sel = []
    for h in heads:
        hs = slice(h * HEAD_DIM, (h + 1) * HEAD_DIM)
        gate = _dot_nt_hi(kmean[:, hs], qf[:, hs])
        selm = jnp.zeros((SUBLANES, blk), F32)
        for n in range(n_kb):
            gn = gate[n:n + 1, :]
            beats = past & ((gate > gn) | ((gate == gn) & (sub < n)))
            rank = jnp.sum(beats.astype(F32), axis=0, keepdims=True)
            selm = jnp.where(sub == n, (rank < MOBA_TOPK).astype(F32), selm)
        sel.append(jnp.where(past, selm, 0.0))

    def logits(j):
        k0 = pl.multiple_of(j * blk, blk)
        return [_dot(ka[h, pl.ds(k0, blk), :], q_t[h]) for h in heads]

    def values(j):
        vj = v_t[j]
        return [vj[h * HEAD_DIM:(h + 1) * HEAD_DIM, :] for h in heads]

    def put(slot, s):
        for h in heads:
            s_buf[slot * MOBA_HEADS + h] = s[h]

    def step(j, slot, carry, live, prefetch):
        if prefetch:
            put(1 - slot, logits(jnp.minimum(j + 1, last)))
        picked = [(jnp.sum(jnp.where(sub == j, sel[h], 0.0), axis=0, keepdims=True) > 0.5) & live for h in heads]
        s = [jnp.where(pk, s_buf[slot * MOBA_HEADS + h], -jnp.inf) for h, pk in zip(heads, picked)]
        return _softmax_tiles(s, carry, values(j))

    last = jnp.maximum(own - 1, 0)
    s_own = logits(own)
    put(0, logits(0))
    carry = _first_tiles([jnp.where(causal, x, -jnp.inf) for x in s_own], values(own))

    def body(jj, flat):
        carry = _unflatten(flat)
        for slot in range(2):
            carry = step(2 * jj + slot, slot, carry, True, True)
        return _flatten(carry)

    carry = _unflatten(lax.fori_loop(0, own // 2, body, _flatten(carry)))
    carry = step(last, 0, carry, own % 2 == 1, False)
    _store_heads(o_ref, carry)


def _moba_call(p, batch, seq):
    blk = MOBA_BLOCK
    n_kb = seq // blk
    assert n_kb <= SUBLANES
    selk, selq = _moba_selectors()
    kern = functools.partial(_moba_kernel, blk=blk, n_kb=n_kb)
    return pl.pallas_call(
        kern,
        grid=(batch, n_kb),
        in_specs=[
            pl.BlockSpec((blk, MOBA_W), lambda b, i: (b * n_kb + i, C_MQ // MOBA_W)),
            pl.BlockSpec((seq, MOBA_W), lambda b, i: (b, C_MK // MOBA_W)),
            pl.BlockSpec((seq, MOBA_W), lambda b, i: (b, C_MV // MOBA_W)),
            pl.BlockSpec(selk.shape, lambda b, i: (0, 0, 0)),
            pl.BlockSpec(selq.shape, lambda b, i: (0, 0, 0)),
        ],
        out_specs=pl.BlockSpec((blk, MOBA_W), lambda b, i: (b * n_kb + i, 0)),
        out_shape=jax.ShapeDtypeStruct((batch * seq, MOBA_W), F32),
        scratch_shapes=[
            pltpu.VMEM((MOBA_HEADS, seq, LANES), BF16),
            pltpu.VMEM((n_kb, MOBA_W, blk), BF16),
            pltpu.VMEM((SUBLANES, MOBA_W), F32),
            pltpu.VMEM((2 * MOBA_HEADS, blk, blk), F32),
        ],
        compiler_params=_cparams(("arbitrary", "arbitrary")),
        name="moba_attention",
    )(p, p, p, selk, selq)


DECAY_SCALE = float(np.exp(-0.5))


def _rwkv_kernel(r_ref, k_ref, v_ref, lo_ref, w0, w2, a0, a2, g2, kkw, kaw, rkw, lnw, lnb, o_ref,
                 state, a_t, r_t, k_t, b_t, k_b, b_b, v_s, w_c, y_s, *, tm, sub):
    i = pl.program_id(1)
    ch = RWKV_CHUNK
    n_ch = sub // ch

    @pl.when(i == 0)
    def _():
        state[...] = jnp.zeros(state.shape, F32)

    gi = lax.broadcasted_iota(jnp.int32, (RWKV_W, RWKV_W), 0) // HEAD_DIM
    gj = lax.broadcasted_iota(jnp.int32, (RWKV_W, RWKV_W), 1) // HEAD_DIM
    group = (gi == gj).astype(BF16)
    ti = lax.broadcasted_iota(jnp.int32, (ch, ch), 0)
    tj = lax.broadcasted_iota(jnp.int32, (ch, ch), 1)
    strict = tj < ti
    incl = tj <= ti

    def prepare(r0):
        span = slice(r0, r0 + sub)
        r = r_ref[span, :]
        k = k_ref[span, :]
        v = v_ref[span, :]
        lo = lo_ref[span, :]
        w_lo = lo[:, 0:DECAY_LORA]
        a_lo = lo[:, DECAY_LORA:DECAY_LORA + AAA_LORA]
        g_lo = lo[:, DECAY_LORA + AAA_LORA:LORA_W]

        lw = -DECAY_SCALE * _sigmoid(w0[...] + _dot3(jnp.tanh(w_lo), w2[...]))
        eta = _sigmoid(a0[...] + _dot3(a_lo, a2[...]))
        gate = _dot(_sigmoid(g_lo).astype(BF16), g2[...])

        kk = k * kkw[...]
        kk = kk * jnp.minimum(lax.rsqrt(_dot_split_lhs(kk * kk, group)), 1e12)
        kp = k * (1.0 + (eta - 1.0) * kaw[...])
        bb = kk * eta

        lc = _seg_cumsum_rows(lw, ch)
        a_t[span, :] = -kk * jnp.exp(lc - lw)
        r_t[span, :] = r * jnp.exp(lc)
        einv = jnp.exp(-lc)
        k_t[span, :] = kp * einv
        b_t[span, :] = bb * einv
        v_s[span, :] = v
        for c in range(n_ch):
            rows = slice(c * ch, (c + 1) * ch)
            dst = slice(r0 + c * ch, r0 + (c + 1) * ch)
            last = lc[(c + 1) * ch - 1:(c + 1) * ch, :]
            e = jnp.exp(last - lc[rows, :])
            k_b[dst, :] = kp[rows, :] * e
            b_b[dst, :] = bb[rows, :] * e
            w_c[r0 // ch + c] = jnp.broadcast_to(jnp.exp(last), (8, RWKV_W))
        return r * kp * rkw[...], v, gate

    def mix(r0, s_cur):
        chains = [(c, h) for c in range(n_ch) for h in range(RWKV_HEADS)]
        ids = range(len(chains))

        def tile(ref):
            return [ref[r0 + c * ch:r0 + (c + 1) * ch, h * HEAD_DIM:(h + 1) * HEAD_DIM] for c, h in chains]

        at, rt, vv, kbar, bbar = tile(a_t), tile(r_t), tile(v_s), tile(k_b), tile(b_b)
        btl, ktl = tile(b_t), tile(k_t)
        return _rwkv_chains(at, rt, vv, kbar, bbar, btl, ktl, s_cur, strict, incl, w_c, y_s, r0, n_ch)

    def finish(r0, bonus_arg, v, gate):
        span = slice(r0, r0 + sub)
        y = y_s[span, :]
        inv_d = 1.0 / HEAD_DIM
        mean = _dot_split_lhs(y, group) * inv_d
        d = y - mean
        var = _dot_split_lhs(d * d, group) * inv_d
        yn = d * lax.rsqrt(var + GN_EPS) * lnw[...] + lnb[...]
        bonus = _dot_split_lhs(bonus_arg, group) * v
        o_ref[span, :] = (yn + bonus) * gate

    starts = range(0, tm, sub)
    prepared = [prepare(r0) for r0 in starts]
    s_cur = [state[h] for h in range(RWKV_HEADS)]
    for r0 in starts:
        s_cur = mix(r0, s_cur)
    for h in range(RWKV_HEADS):
        state[h] = s_cur[h]
    for r0, vals in zip(starts, prepared):
        finish(r0, *vals)


def _rwkv_chains(at, rt, vv, kbar, bbar, btl, ktl, s_cur, strict, incl, w_c, y_s, r0, n_ch):
    ch = RWKV_CHUNK
    ids = range(len(at))
    pad = jnp.zeros((LANES - ch, HEAD_DIM), BF16)
    m4 = [_dot_nt(jnp.concatenate([at[i], rt[i]], axis=0).astype(BF16),
                  jnp.concatenate([btl[i].astype(BF16), pad, ktl[i].astype(BF16), pad], axis=0)) for i in ids]
    a_ab = [jnp.where(strict, m4[i][0:ch, 0:ch], 0.0) for i in ids]
    a_ak = [jnp.where(strict, m4[i][0:ch, LANES:LANES + ch], 0.0) for i in ids]
    a_rb = [jnp.where(incl, m4[i][ch:2 * ch, 0:ch], 0.0) for i in ids]
    a_rk = [jnp.where(incl, m4[i][ch:2 * ch, LANES:LANES + ch], 0.0) for i in ids]
    avk = [_mm(jnp.concatenate([a_ak[i], a_rk[i]], axis=0), vv[i]) for i in ids]
    pw = a_ab
    tx = [jnp.concatenate([at[i], avk[i][0:ch, :]], axis=1) for i in ids]
    span = 1
    while 2 * span < ch:
        x = [_mm(pw[i], jnp.concatenate([tx[i], pw[i]], axis=1)) for i in ids]
        pw = [x[i][:, 2 * ch:3 * ch] for i in ids]
        tx = [tx[i] + x[i][:, 0:2 * ch] for i in ids]
        span *= 2
    tx = [tx[i] + _mm(pw[i], tx[i]) for i in ids]
    ry = [_mm(a_rb[i], tx[i]) for i in ids]
    rhat = [rt[i] + ry[i][:, 0:ch] for i in ids]
    yhat = [avk[i][ch:2 * ch, :] + ry[i][:, ch:2 * ch] for i in ids]
    z = [_mm(tx[i][:, 0:ch].T, bbar[i]) for i in ids]
    kv = [_mm(jnp.concatenate([vv[i], tx[i][:, ch:2 * ch]], axis=0).T,
              jnp.concatenate([kbar[i], bbar[i]], axis=0)) for i in ids]

    s_cur = list(s_cur)
    for c in range(n_ch):
        base = c * RWKV_HEADS
        ys = [_mm_nt(rhat[base + h], s_cur[h]) for h in range(RWKV_HEADS)]
        sz = [_mm(s_cur[h], z[base + h]) for h in range(RWKV_HEADS)]
        for h in range(RWKV_HEADS):
            hs = slice(h * HEAD_DIM, (h + 1) * HEAD_DIM)
            y_s[r0 + c * ch:r0 + (c + 1) * ch, hs] = ys[h] + yhat[base + h]
            s_cur[h] = s_cur[h] * w_c[r0 // ch + c][0:1, hs] + sz[h] + kv[base + h]
    return s_cur


def _rwkv_call(p, prm, batch, seq):
    tm = 512
    sub = 256
    nt = seq // tm
    kern = functools.partial(_rwkv_kernel, tm=tm, sub=sub)

    def rows(width, cstart):
        return pl.BlockSpec((tm, width), lambda b, i: (b * nt + i, cstart // width))

    def full(a):
        return pl.BlockSpec(a.shape, lambda b, i: (0,) * a.ndim)

    params = [prm[n] for n in ("w0", "w2", "a0", "a2", "g2", "k_k", "k_a", "r_k", "ln_w", "ln_b")]
    big = pltpu.VMEM((tm, RWKV_W), F32)
    return pl.pallas_call(
        kern,
        grid=(batch, nt),
        in_specs=[rows(RWKV_W, C_RR), rows(RWKV_W, C_RK), rows(RWKV_W, C_RV), rows(LORA_W, C_LORA)]
        + [full(a) for a in params],
        out_specs=pl.BlockSpec((tm, RWKV_W), lambda b, i: (b * nt + i, 0)),
        out_shape=jax.ShapeDtypeStruct((batch * seq, RWKV_W), F32),
        scratch_shapes=[
            pltpu.VMEM((RWKV_HEADS, HEAD_DIM, HEAD_DIM), F32),
            big, big, big, big, big, big, big,
            pltpu.VMEM((tm // RWKV_CHUNK, 8, RWKV_W), F32),
            big,
        ],
        compiler_params=_cparams(("arbitrary", "arbitrary")),
        name="rwkv7_mix",
    )(p, p, p, p, *params)


FFN_HALO = 16


def _ffn_kernel(yf_ref, yfp_ref, yr_ref, yrp_ref, ym_ref, ymp_ref, x_ref, xp_ref, mod_ref, nw_ref, wo_ref,
                wg_ref, wv_ref, cwg_ref, cwv_ref, cbg_ref, cbv_ref, wd_ref, nf_ref, o_ref, hext, acc, xmid,
                *, tm, nt, n_ff, final):
    i = pl.program_id(0)
    j = pl.program_id(1)
    m = mod_ref[0]

    @pl.when(j == 0)
    def _():
        def ext(prev_ref, cur_ref):
            return jnp.concatenate([prev_ref[...], cur_ref[...]], axis=0)

        z = _dot(ext(yfp_ref, yf_ref).astype(BF16), wo_ref[0:FOX_W, :])
        z = z + _dot(ext(yrp_ref, yr_ref).astype(BF16), wo_ref[FOX_W:FOX_W + RWKV_W, :])
        z = z + _dot(ext(ymp_ref, ym_ref).astype(BF16), wo_ref[FOX_W + RWKV_W:, :])
        x1 = ext(xp_ref, x_ref) + m[2:3, :] * z
        xmid[...] = x1[FFN_HALO:, :]
        h = _rmsnorm(x1, nw_ref[...]) * (1.0 + m[4:5, :]) + m[3:4, :]
        keep = jnp.where(i % nt == 0, 0.0, 1.0)
        hext[0:FFN_HALO, :] = (h[0:FFN_HALO, :] * keep).astype(BF16)
        hext[FFN_HALO:, :] = h[FFN_HALO:, :].astype(BF16)
        acc[...] = jnp.zeros(acc.shape, F32)

    he = hext[...]

    def conv(u, cw, cb):
        return (cb[...] + cw[0:1, :] * pltpu.roll(u, 2, 0)[FFN_HALO:, :]
                + cw[1:2, :] * pltpu.roll(u, 1, 0)[FFN_HALO:, :] + cw[2:3, :] * u[FFN_HALO:, :])

    ug = conv(_dot(he, wg_ref[...]), cwg_ref, cbg_ref)
    uv = conv(_dot(he, wv_ref[...]), cwv_ref, cbv_ref)
    act = ug * _sigmoid(ug) * uv
    acc[...] += _dot(act.astype(BF16), wd_ref[...])

    @pl.when(j == n_ff - 1)
    def _():
        out = xmid[...] + m[5:6, :] * acc[...]
        if final:
            out = _rmsnorm(out, nf_ref[...])
        o_ref[...] = out


def _ffn_call(y_fox, y_rwkv, y_moba, xf, mod_l, norm_w, w_out_b, w_up_b, conv_w, conv_b, w_down_b, norm_final,
              layer, seq, final):
    rows, d = xf.shape
    d_ff = w_down_b.shape[1]
    tm = 512
    n_ff = 2
    tf = d_ff // n_ff
    nt = seq // tm
    hb = tm // FFN_HALO
    kern = functools.partial(_ffn_kernel, tm=tm, nt=nt, n_ff=n_ff, final=final)

    def tile_and_halo(width):
        return [pl.BlockSpec((tm, width), lambda i, j: (i, 0)),
                pl.BlockSpec((FFN_HALO, width), lambda i, j: (jnp.maximum(i * hb - 1, 0), 0))]

    return pl.pallas_call(
        kern,
        grid=(rows // tm, n_ff),
        in_specs=tile_and_halo(FOX_W) + tile_and_halo(RWKV_W) + tile_and_halo(MOBA_W) + tile_and_halo(d) + [
            pl.BlockSpec((1, 6, d), lambda i, j: (i // nt, 0, 0)),
            pl.BlockSpec((1, d), lambda i, j: (0, 0)),
            pl.BlockSpec((None,) + w_out_b.shape[1:], lambda i, j: (layer, 0, 0)),
            pl.BlockSpec((None, d, tf), lambda i, j: (layer, 0, j)),
            pl.BlockSpec((None, d, tf), lambda i, j: (layer, 0, n_ff + j)),
            pl.BlockSpec((CONV_W, tf), lambda i, j: (0, j)),
            pl.BlockSpec((CONV_W, tf), lambda i, j: (0, n_ff + j)),
            pl.BlockSpec((1, tf), lambda i, j: (0, j)),
            pl.BlockSpec((1, tf), lambda i, j: (0, n_ff + j)),
            pl.BlockSpec((None, tf, d), lambda i, j: (layer, j, 0)),
            pl.BlockSpec((1, d), lambda i, j: (0, 0)),
        ],
        out_specs=pl.BlockSpec((tm, d), lambda i, j: (i, 0)),
        out_shape=jax.ShapeDtypeStruct((rows, d), F32),
        scratch_shapes=[pltpu.VMEM((tm + FFN_HALO, d), BF16), pltpu.VMEM((tm, d), F32),
                        pltpu.VMEM((tm, d), F32)],
        compiler_params=_cparams(("arbitrary", "arbitrary")),
        name="out_proj_conv_ffn",
    )(y_fox, y_fox, y_rwkv, y_rwkv, y_moba, y_moba, xf, xf, mod_l, norm_w, w_out_b, w_up_b, w_up_b, conv_w,
      conv_w, conv_b, conv_b, w_down_b, norm_final)


def kernel(x, c, w_mod, b_mod, norm_mix, w_in, fox_f_bias, rwkv_mu, rwkv_w0, rwkv_w2, rwkv_a0, rwkv_a2,
           rwkv_g2, rwkv_k_k, rwkv_k_a, rwkv_r_k, rwkv_ln_w, rwkv_ln_b, w_out, norm_ffn, w_up, conv_w,
           conv_b, w_down, norm_final):
    batch, seq, d = x.shape
    n_layers = w_mod.shape[0]

    mod = _mod_call(c, w_mod, b_mod).reshape(n_layers, batch, 6, d)
    w_in_p = _win_layout_call(w_in)
    f_bias = jnp.pad(fox_f_bias, ((0, 0), (0, FF_PAD - FOX_HEADS)))
    w_out_b = w_out.astype(BF16)
    w_up_b = w_up.astype(BF16)
    w_down_b = w_down.astype(BF16)
    g2_b = rwkv_g2.astype(BF16)

    xf = x.reshape(batch * seq, d)
    for l in range(n_layers):
        row = lambda a: a[l].reshape(1, -1)
        prm = {
            "w0": row(rwkv_w0), "w2": rwkv_w2[l], "a0": row(rwkv_a0), "a2": rwkv_a2[l], "g2": g2_b[l],
            "k_k": row(rwkv_k_k), "k_a": row(rwkv_k_a), "r_k": row(rwkv_r_k), "ln_w": row(rwkv_ln_w),
            "ln_b": row(rwkv_ln_b),
        }
        p = _inproj_call(xf, mod[l], row(norm_mix), w_in_p, row(rwkv_mu), l, seq)
        y_fox = _fox_call(p, f_bias[l:l + 1], batch, seq)
        y_moba = _moba_call(p, batch, seq)
        y_rwkv = _rwkv_call(p, prm, batch, seq)
        xf = _ffn_call(y_fox, y_rwkv, y_moba, xf, mod[l], row(norm_ffn), w_out_b, w_up_b, conv_w[l],
                       conv_b[l].reshape(1, -1), w_down_b, norm_final.reshape(1, -1), l, seq,
                       final=(l == n_layers - 1))
    return xf.reshape(batch, seq, d)
```

```python
import functools

import jax
import jax.numpy as jnp
import numpy as np
from jax import lax
from jax.experimental import pallas as pl
from jax.experimental.pallas import tpu as pltpu

F32 = jnp.float32
BF16 = jnp.bfloat16
HI = lax.Precision.HIGHEST

HEAD_DIM = 64
FOX_HEADS = 4
RWKV_HEADS = 8
MOBA_HEADS = 4
FOX_W = FOX_HEADS * HEAD_DIM
RWKV_W = RWKV_HEADS * HEAD_DIM
MOBA_W = MOBA_HEADS * HEAD_DIM
DECAY_LORA = 64
AAA_LORA = 64
GATE_LORA = 128
LORA_W = DECAY_LORA + AAA_LORA + GATE_LORA
MOBA_BLOCK = 256
MOBA_TOPK = 3
CONV_W = 3
NORM_EPS = 1e-6
GN_EPS = 64e-5
LOG2E = float(np.log2(np.e))
ATTN_SCALE = HEAD_DIM ** -0.5 * LOG2E

C_FQ, C_FK, C_FV = 0, 256, 512
C_MQ, C_MK, C_MV = 768, 1024, 1280
C_RR, C_RK, C_RV = 1536, 2048, 2560
C_LORA = 3072
C_FF = 3328
FF_PAD = 128
NP_COLS = C_FF + FF_PAD

RWKV_CHUNK = 64
LANES = 128
SUBLANES = 8
VMEM_LIMIT = 56 * 1024 * 1024


def _cparams(sem):
    return pltpu.CompilerParams(dimension_semantics=sem, vmem_limit_bytes=VMEM_LIMIT)


def _dot(a, b):
    return jnp.dot(a, b, preferred_element_type=F32)


def _dot_hi(a, b):
    return jnp.dot(a, b, precision=HI, preferred_element_type=F32)


def _dot_nt(a, b):
    return lax.dot_general(a, b, (((1,), (1,)), ((), ())), preferred_element_type=F32)


def _dot_nt_hi(a, b):
    return lax.dot_general(a, b, (((1,), (1,)), ((), ())), precision=HI, preferred_element_type=F32)


def _mm(a, b):
    return jnp.dot(a.astype(BF16), b.astype(BF16), preferred_element_type=F32)


def _mm_nt(a, b):
    return _dot_nt(a.astype(BF16), b.astype(BF16))


def _split2(x):
    hi = x.astype(BF16)
    return hi, (x - hi.astype(F32)).astype(BF16)


def _dot_split_lhs(x, w_bf16):
    m = x.shape[0]
    r = _dot(jnp.concatenate(_split2(x), axis=0), w_bf16)
    return r[0:m, :] + r[m:2 * m, :]


def _dot3(a, b):
    m = a.shape[0]
    a_hi, a_lo = _split2(a)
    b_hi, b_lo = _split2(b)
    r = _dot(jnp.concatenate([a_hi, a_lo], axis=0), b_hi)
    return r[0:m, :] + r[m:2 * m, :] + _dot(a_hi, b_lo)


def _sigmoid(x):
    return 1.0 / (1.0 + jnp.exp(-x))


def _log_sigmoid(x):
    return jnp.minimum(x, 0.0) - jnp.log1p(jnp.exp(-jnp.abs(x)))


def _rmsnorm(x, w):
    ms = jnp.mean(x * x, axis=-1, keepdims=True)
    return x * lax.rsqrt(ms + NORM_EPS) * w


def _seg_cumsum_rows(x, seg):
    row = lax.broadcasted_iota(jnp.int32, x.shape, 0) & (seg - 1)
    s = 1
    while s < seg:
        x = x + jnp.where(row >= s, pltpu.roll(x, s, 0), 0.0)
        s *= 2
    return x


def _mod_kernel(c_ref, w_ref, b_ref, o_ref):
    c = c_ref[...]
    o_ref[0] = _dot_hi(c * _sigmoid(c), w_ref[0]) + b_ref[0]


def _mod_call(c, w_mod, b_mod):
    n_layers, d, n = w_mod.shape
    b = c.shape[0]
    tn = 1536
    return pl.pallas_call(
        _mod_kernel,
        grid=(n_layers, n // tn),
        in_specs=[
            pl.BlockSpec((b, d), lambda l, j: (0, 0)),
            pl.BlockSpec((1, d, tn), lambda l, j: (l, 0, j)),
            pl.BlockSpec((1, 1, tn), lambda l, j: (l, 0, j)),
        ],
        out_specs=pl.BlockSpec((1, b, tn), lambda l, j: (l, 0, j)),
        out_shape=jax.ShapeDtypeStruct((n_layers, b, n), F32),
        compiler_params=_cparams(("arbitrary", "arbitrary")),
        name="adaln_mod",
    )(c, w_mod, b_mod.reshape(n_layers, 1, n))


def _win_layout_kernel(w_ref, o_ref):
    fox_cols = 3 * FOX_W + FOX_HEADS
    rwkv_cols = 3 * RWKV_W + LORA_W
    w = w_ref[...]
    o_ref[:, C_FQ:C_MQ] = w[:, 0:3 * FOX_W].astype(BF16)
    o_ref[:, C_MQ:C_RR] = w[:, fox_cols + rwkv_cols:fox_cols + rwkv_cols + 3 * MOBA_W].astype(BF16)
    o_ref[:, C_RR:C_FF] = w[:, fox_cols:fox_cols + rwkv_cols].astype(BF16)
    lane = lax.broadcasted_iota(jnp.int32, (w.shape[0], FF_PAD), 1)
    o_ref[:, C_FF:NP_COLS] = jnp.where(lane < FOX_HEADS, w[:, 3 * FOX_W:3 * FOX_W + FF_PAD], 0.0).astype(BF16)


def _win_layout_call(w_in):
    n_layers, d, n = w_in.shape
    tr = 256
    return pl.pallas_call(
        _win_layout_kernel,
        grid=(n_layers, d // tr),
        in_specs=[pl.BlockSpec((None, tr, n), lambda l, i: (l, i, 0))],
        out_specs=pl.BlockSpec((None, tr, NP_COLS), lambda l, i: (l, i, 0)),
        out_shape=jax.ShapeDtypeStruct((n_layers, d, NP_COLS), BF16),
        compiler_params=_cparams(("arbitrary", "arbitrary")),
        name="w_in_layout",
    )(w_in)


def _inproj_kernel(x_ref, mod_ref, nw_ref, w_ref, mu_ref, o_ref, prev, *, tm, nt):
    i = pl.program_id(0)
    m = mod_ref[0]
    h = _rmsnorm(x_ref[...], nw_ref[...]) * (1.0 + m[1:2, :]) + m[0:1, :]
    p = _dot(h.astype(BF16), w_ref[...])
    o_ref[:, 0:C_RR] = p[:, 0:C_RR]
    o_ref[:, C_FF:NP_COLS] = p[:, C_FF:NP_COLS]
    feat = p[:, C_RR:C_FF]

    @pl.when(i % nt == 0)
    def _():
        prev[...] = jnp.zeros(prev.shape, F32)

    rolled = pltpu.roll(feat, 1, 0)
    first = lax.broadcasted_iota(jnp.int32, (SUBLANES, C_FF - C_RR), 0) == 0
    top = jnp.where(first, prev[...], rolled[0:SUBLANES, :])
    shifted = jnp.concatenate([top, rolled[SUBLANES:, :]], axis=0)
    prev[...] = feat[tm - 1:tm, :]
    o_ref[:, C_RR:C_FF] = feat + (shifted - feat) * mu_ref[...]


def _inproj_call(xf, mod_l, norm_w, w_in_p, mu_row, layer, seq):
    rows, d = xf.shape
    tm = 512
    nt = seq // tm
    kern = functools.partial(_inproj_kernel, tm=tm, nt=nt)
    return pl.pallas_call(
        kern,
        grid=(rows // tm,),
        in_specs=[
            pl.BlockSpec((tm, d), lambda i: (i, 0)),
            pl.BlockSpec((1, 6, d), lambda i: (i // nt, 0, 0)),
            pl.BlockSpec((1, d), lambda i: (0, 0)),
            pl.BlockSpec((None, d, NP_COLS), lambda i: (layer, 0, 0)),
            pl.BlockSpec((1, C_FF - C_RR), lambda i: (0, 0)),
        ],
        out_specs=pl.BlockSpec((tm, NP_COLS), lambda i: (i, 0)),
        out_shape=jax.ShapeDtypeStruct((rows, NP_COLS), F32),
        scratch_shapes=[pltpu.VMEM((1, C_FF - C_RR), F32)],
        compiler_params=_cparams(("arbitrary",)),
        name="in_proj",
    )(xf, mod_l, norm_w, w_in_p, mu_row)


def _softmax_tiles(s, carry, v_t):
    m_new = [jnp.maximum(c[0], jnp.max(x, axis=0, keepdims=True)) for x, c in zip(s, carry)]
    p = [jnp.exp2(x - m) for x, m in zip(s, m_new)]
    pv = [_dot(v, x.astype(BF16)) for v, x in zip(v_t, p)]
    out = []
    for (m, l, acc), mn, x, y in zip(carry, m_new, p, pv):
        alpha = jnp.exp2(m - mn)
        out.append((mn, alpha * l + jnp.sum(x, axis=0, keepdims=True), alpha * acc + y))
    return out


def _first_tiles(s, v_t):
    m = [jnp.max(x, axis=0, keepdims=True) for x in s]
    p = [jnp.exp2(x - mm) for x, mm in zip(s, m)]
    pv = [_dot(v, x.astype(BF16)) for v, x in zip(v_t, p)]
    return [(mm, jnp.sum(x, axis=0, keepdims=True), y) for mm, x, y in zip(m, p, pv)]


def _flatten(carry):
    return tuple(a for c in carry for a in c)


def _unflatten(flat):
    return [tuple(flat[3 * h:3 * h + 3]) for h in range(len(flat) // 3)]


def _store_heads(o_ref, carry):
    o_ref[...] = jnp.concatenate([acc / l for _, l, acc in carry], axis=0).T


F_COL = HEAD_DIM
N_SPLIT = 3
FOX_XW = FOX_W + FF_PAD


def _fox_selectors():
    selk = np.zeros((FOX_XW, FOX_HEADS * LANES), np.float32)
    for h in range(FOX_HEADS):
        for c in range(HEAD_DIM):
            selk[h * HEAD_DIM + c, h * LANES + c] = 1.0
    selq = selk.T.copy()
    for h in range(FOX_HEADS):
        for s in range(N_SPLIT):
            selk[FOX_W + FOX_HEADS * s + h, h * LANES + F_COL + s] = 1.0
            selq[h * LANES + F_COL + N_SPLIT + s, FOX_W + FOX_HEADS * s + h] = 1.0
    return jnp.asarray(selk, BF16), jnp.asarray(selq, BF16)


def _fox_kernel(q_ref, k_ref, v_ref, ff_ref, fb_ref, selk_ref, selq_ref, o_ref, ka, v_t, fp, s_buf, *, tq, tk,
                n_kv):
    qi = pl.program_id(1)
    heads = range(FOX_HEADS)

    @pl.when(qi == 0)
    def _():
        lane = lax.broadcasted_iota(jnp.int32, (1, LANES), 1)
        f = _seg_cumsum_rows(_log_sigmoid(ff_ref[...] + fb_ref[...]), n_kv * tk) * LOG2E
        f = jnp.where(lane < FOX_HEADS, f, 0.0)
        hi = f.astype(BF16).astype(F32)
        mid = (f - hi).astype(BF16).astype(F32)
        lo = (f - hi - mid).astype(BF16).astype(F32)
        fp[...] = (hi + pltpu.roll(mid, FOX_HEADS, 1) + pltpu.roll(lo, 2 * FOX_HEADS, 1)).astype(BF16)
        x = jnp.concatenate([k_ref[...].astype(BF16), fp[...]], axis=1)
        lane4 = lax.broadcasted_iota(jnp.int32, (1, FOX_HEADS * LANES), 1) % LANES
        ones = jnp.where((lane4 >= F_COL + N_SPLIT) & (lane4 < F_COL + 2 * N_SPLIT), 1.0, 0.0)
        ka[...] = (_dot(x, selk_ref[...]) + ones).astype(BF16)
        for j in range(n_kv):
            v_t[j] = v_ref[j * tk:(j + 1) * tk, :].T.astype(BF16)

    q0 = pl.multiple_of(qi * tq, tq)
    xq = jnp.concatenate([(q_ref[...] * ATTN_SCALE).astype(BF16), fp[pl.ds(q0, tq), :]], axis=1)
    sub = lax.broadcasted_iota(jnp.int32, (FOX_HEADS * LANES, tq), 0) % LANES
    neg = jnp.where((sub >= F_COL) & (sub < F_COL + N_SPLIT), -1.0, 0.0)
    q_all = (_dot_nt(selq_ref[...], xq) + neg).astype(BF16)
    q_t = [q_all[h * LANES:(h + 1) * LANES, :] for h in heads]
    lag = (lax.broadcasted_iota(jnp.int32, (tk, tq), 0) - lax.broadcasted_iota(jnp.int32, (tk, tq), 1))
    n_diag = tq // tk

    def logits(j):
        k0 = pl.multiple_of(j * tk, tk)
        return [_dot(ka[pl.ds(k0, tk), h * LANES:(h + 1) * LANES], q_t[h]) for h in heads]

    def values(j):
        vj = v_t[j]
        return [vj[h * HEAD_DIM:(h + 1) * HEAD_DIM, :] for h in heads]

    def put(slot, s):
        for h in heads:
            s_buf[slot * FOX_HEADS + h] = s[h]

    j0 = qi * n_diag
    last = jnp.maximum(j0 - 1, 0)
    s_diag = [logits(j0 + dd) for dd in range(n_diag)]
    put(0, logits(0))
    carry = _first_tiles([jnp.where(lag <= 0, x, -jnp.inf) for x in s_diag[0]], values(j0))
    for dd in range(1, n_diag):
        carry = _softmax_tiles([jnp.where(lag <= -dd * tk, x, -jnp.inf) for x in s_diag[dd]], carry,
                               values(j0 + dd))

    def step(j, slot, carry):
        put(1 - slot, logits(jnp.minimum(j + 1, last)))
        s_cur = [s_buf[slot * FOX_HEADS + h] for h in heads]
        return _softmax_tiles(s_cur, carry, values(j))

    def body(jj, flat):
        carry = _unflatten(flat)
        for slot in range(2):
            carry = step(2 * jj + slot, slot, carry)
        return _flatten(carry)

    assert n_diag % 2 == 0
    _store_heads(o_ref, _unflatten(lax.fori_loop(0, j0 // 2, body, _flatten(carry))))


def _fox_call(p, f_bias_row, batch, seq):
    tq = 512
    tk = 256
    nq = seq // tq
    selk, selq = _fox_selectors()
    kern = functools.partial(_fox_kernel, tq=tq, tk=tk, n_kv=seq // tk)
    return pl.pallas_call(
        kern,
        grid=(batch, nq),
        in_specs=[
            pl.BlockSpec((tq, FOX_W), lambda b, i: (b * nq + i, C_FQ // FOX_W)),
            pl.BlockSpec((seq, FOX_W), lambda b, i: (b, C_FK // FOX_W)),
            pl.BlockSpec((seq, FOX_W), lambda b, i: (b, C_FV // FOX_W)),
            pl.BlockSpec((seq, FF_PAD), lambda b, i: (b, C_FF // FF_PAD)),
            pl.BlockSpec((1, FF_PAD), lambda b, i: (0, 0)),
            pl.BlockSpec(selk.shape, lambda b, i: (0, 0)),
            pl.BlockSpec(selq.shape, lambda b, i: (0, 0)),
        ],
        out_specs=pl.BlockSpec((tq, FOX_W), lambda b, i: (b * nq + i, 0)),
        out_shape=jax.ShapeDtypeStruct((batch * seq, FOX_W), F32),
        scratch_shapes=[
            pltpu.VMEM((seq, FOX_HEADS * LANES), BF16),
            pltpu.VMEM((seq // tk, FOX_W, tk), BF16),
            pltpu.VMEM((seq, FF_PAD), BF16),
            pltpu.VMEM((2 * FOX_HEADS, tk, tq), F32),
        ],
        compiler_params=_cparams(("arbitrary", "arbitrary")),
        name="fox_attention",
    )(p, p, p, p, f_bias_row, selk, selq)


def _moba_selectors():
    selk = np.zeros((MOBA_W, MOBA_HEADS * LANES), np.float32)
    for h in range(MOBA_HEADS):
        for c in range(HEAD_DIM):
            selk[h * HEAD_DIM + c, h * LANES + c] = 1.0
    return jnp.asarray(selk, BF16), jnp.asarray(selk.T, BF16)


def _moba_kernel(q_ref, k_ref, v_ref, selk_ref, selq_ref, o_ref, ka, v_t, kmean, s_buf, *, blk, n_kb):
    own = pl.program_id(1)
    heads = range(MOBA_HEADS)

    @pl.when(own == 0)
    def _():
        kf = k_ref[...]
        ka[...] = _dot(kf.astype(BF16), selk_ref[...]).astype(BF16)
        kmean[...] = jnp.zeros(kmean.shape, F32)
        for n in range(n_kb):
            kmean[n:n + 1, :] = jnp.mean(kf[n * blk:(n + 1) * blk, :], axis=0, keepdims=True)
            v_t[n] = v_ref[n * blk:(n + 1) * blk, :].T.astype(BF16)

    qf = q_ref[...]
    qs = (qf * ATTN_SCALE).astype(BF16)
    q_all = _dot_nt(selq_ref[...], qs).astype(BF16)
    q_t = [q_all[h * LANES:(h + 1) * LANES, :] for h in heads]
    sub = lax.broadcasted_iota(jnp.int32, (SUBLANES, blk), 0)
    past = sub < own
    causal = (lax.broadcasted_iota(jnp.int32, (blk, blk), 0) <= lax.broadcasted_iota(jnp.int32, (blk, blk), 1))

    sel = []
    for h in heads:
        hs = slice(h * HEAD_DIM, (h + 1) * HEAD_DIM)
        gate = _dot_nt_hi(kmean[:, hs], qf[:, hs])
        selm = jnp.zeros((SUBLANES, blk), F32)
        for n in range(n_kb):
            gn = gate[n:n + 1, :]
            beats = past & ((gate > gn) | ((gate == gn) & (sub < n)))
            rank = jnp.sum(beats.astype(F32), axis=0, keepdims=True)
            selm = jnp.where(sub == n, (rank < MOBA_TOPK).astype(F32), selm)
        sel.append(jnp.where(past, selm, 0.0))

    def logits(j):
        k0 = pl.multiple_of(j * blk, blk)
        return [_dot(ka[pl.ds(k0, blk), h * LANES:(h + 1) * LANES], q_t[h]) for h in heads]

    def values(j):
        vj = v_t[j]
        return [vj[h * HEAD_DIM:(h + 1) * HEAD_DIM, :] for h in heads]

    def put(slot, s):
        for h in heads:
            s_buf[slot * MOBA_HEADS + h] = s[h]

    def step(j, slot, carry, live, prefetch):
        if prefetch:
            put(1 - slot, logits(jnp.minimum(j + 1, last)))
        picked = [(jnp.sum(jnp.where(sub == j, sel[h], 0.0), axis=0, keepdims=True) > 0.5) & live for h in heads]
        s = [jnp.where(pk, s_buf[slot * MOBA_HEADS + h], -jnp.inf) for h, pk in zip(heads, picked)]
        return _softmax_tiles(s, carry, values(j))

    last = jnp.maximum(own - 1, 0)
    s_own = logits(own)
    put(0, logits(0))
    carry = _first_tiles([jnp.where(causal, x, -jnp.inf) for x in s_own], values(own))

    def body(jj, flat):
        carry = _unflatten(flat)
        for slot in range(2):
            carry = step(2 * jj + slot, slot, carry, True, True)
        return _flatten(carry)

    carry = _unflatten(lax.fori_loop(0, own // 2, body, _flatten(carry)))
    carry = step(last, 0, carry, own % 2 == 1, False)
    _store_heads(o_ref, carry)


def _moba_call(p, batch, seq):
    blk = MOBA_BLOCK
    n_kb = seq // blk
    assert n_kb <= SUBLANES
    selk, selq = _moba_selectors()
    kern = functools.partial(_moba_kernel, blk=blk, n_kb=n_kb)
    return pl.pallas_call(
        kern,
        grid=(batch, n_kb),
        in_specs=[
            pl.BlockSpec((blk, MOBA_W), lambda b, i: (b * n_kb + i, C_MQ // MOBA_W)),
            pl.BlockSpec((seq, MOBA_W), lambda b, i: (b, C_MK // MOBA_W)),
            pl.BlockSpec((seq, MOBA_W), lambda b, i: (b, C_MV // MOBA_W)),
            pl.BlockSpec(selk.shape, lambda b, i: (0, 0)),
            pl.BlockSpec(selq.shape, lambda b, i: (0, 0)),
        ],
        out_specs=pl.BlockSpec((blk, MOBA_W), lambda b, i: (b * n_kb + i, 0)),
        out_shape=jax.ShapeDtypeStruct((batch * seq, MOBA_W), F32),
        scratch_shapes=[
            pltpu.VMEM((seq, MOBA_HEADS * LANES), BF16),
            pltpu.VMEM((n_kb, MOBA_W, blk), BF16),
            pltpu.VMEM((SUBLANES, MOBA_W), F32),
            pltpu.VMEM((2 * MOBA_HEADS, blk, blk), F32),
        ],
        compiler_params=_cparams(("arbitrary", "arbitrary")),
        name="moba_attention",
    )(p, p, p, selk, selq)


DECAY_SCALE = float(np.exp(-0.5))


def _rwkv_kernel(r_ref, k_ref, v_ref, lo_ref, w0, w2, a0, a2, g2, kkw, kaw, rkw, lnw, lnb, o_ref,
                 state, a_t, r_t, k_t, b_t, k_b, b_b, v_s, w_c, y_s, *, tm, sub):
    i = pl.program_id(1)
    ch = RWKV_CHUNK
    n_ch = sub // ch

    @pl.when(i == 0)
    def _():
        state[...] = jnp.zeros(state.shape, F32)

    gi = lax.broadcasted_iota(jnp.int32, (RWKV_W, RWKV_W), 0) // HEAD_DIM
    gj = lax.broadcasted_iota(jnp.int32, (RWKV_W, RWKV_W), 1) // HEAD_DIM
    group = (gi == gj).astype(BF16)
    ti = lax.broadcasted_iota(jnp.int32, (ch, ch), 0)
    tj = lax.broadcasted_iota(jnp.int32, (ch, ch), 1)
    strict = tj < ti
    incl = tj <= ti

    def prepare(r0):
        span = slice(r0, r0 + sub)
        r = r_ref[span, :]
        k = k_ref[span, :]
        v = v_ref[span, :]
        lo = lo_ref[span, :]
        w_lo = lo[:, 0:DECAY_LORA]
        a_lo = lo[:, DECAY_LORA:DECAY_LORA + AAA_LORA]
        g_lo = lo[:, DECAY_LORA + AAA_LORA:LORA_W]

        lw = -DECAY_SCALE * _sigmoid(w0[...] + _dot3(jnp.tanh(w_lo), w2[...]))
        eta = _sigmoid(a0[...] + _dot3(a_lo, a2[...]))
        gate = _dot(_sigmoid(g_lo).astype(BF16), g2[...])

        kk = k * kkw[...]
        kk = kk * jnp.minimum(lax.rsqrt(_dot_split_lhs(kk * kk, group)), 1e12)
        kp = k * (1.0 + (eta - 1.0) * kaw[...])
        bb = kk * eta

        lc = _seg_cumsum_rows(lw, ch)
        a_t[span, :] = -kk * jnp.exp(lc - lw)
        r_t[span, :] = r * jnp.exp(lc)
        einv = jnp.exp(-lc)
        k_t[span, :] = kp * einv
        b_t[span, :] = bb * einv
        v_s[span, :] = v
        for c in range(n_ch):
            rows = slice(c * ch, (c + 1) * ch)
            dst = slice(r0 + c * ch, r0 + (c + 1) * ch)
            last = lc[(c + 1) * ch - 1:(c + 1) * ch, :]
            e = jnp.exp(last - lc[rows, :])
            k_b[dst, :] = kp[rows, :] * e
            b_b[dst, :] = bb[rows, :] * e
            w_c[r0 // ch + c] = jnp.broadcast_to(jnp.exp(last), (8, RWKV_W))
        return r * kp * rkw[...], v, gate

    def mix(r0, s_cur):
        chains = [(c, h) for c in range(n_ch) for h in range(RWKV_HEADS)]
        ids = range(len(chains))

        def tile(ref):
            return [ref[r0 + c * ch:r0 + (c + 1) * ch, h * HEAD_DIM:(h + 1) * HEAD_DIM] for c, h in chains]

        at, rt, vv, kbar, bbar = tile(a_t), tile(r_t), tile(v_s), tile(k_b), tile(b_b)
        btl, ktl = tile(b_t), tile(k_t)
        return _rwkv_chains(at, rt, vv, kbar, bbar, btl, ktl, s_cur, strict, incl, w_c, y_s, r0, n_ch)

    def finish(r0, bonus_arg, v, gate):
        span = slice(r0, r0 + sub)
        y = y_s[span, :]
        inv_d = 1.0 / HEAD_DIM
        mean = _dot_split_lhs(y, group) * inv_d
        d = y - mean
        var = _dot_split_lhs(d * d, group) * inv_d
        yn = d * lax.rsqrt(var + GN_EPS) * lnw[...] + lnb[...]
        bonus = _dot_split_lhs(bonus_arg, group) * v
        o_ref[span, :] = (yn + bonus) * gate

    starts = range(0, tm, sub)
    prepared = [prepare(r0) for r0 in starts]
    s_cur = [state[h] for h in range(RWKV_HEADS)]
    for r0 in starts:
        s_cur = mix(r0, s_cur)
    for h in range(RWKV_HEADS):
        state[h] = s_cur[h]
    for r0, vals in zip(starts, prepared):
        finish(r0, *vals)


def _rwkv_chains(at, rt, vv, kbar, bbar, btl, ktl, s_cur, strict, incl, w_c, y_s, r0, n_ch):
    ch = RWKV_CHUNK
    ids = range(len(at))
    pad = jnp.zeros((LANES - ch, HEAD_DIM), BF16)
    m4 = [_dot_nt(jnp.concatenate([at[i], rt[i]], axis=0).astype(BF16),
                  jnp.concatenate([btl[i].astype(BF16), pad, ktl[i].astype(BF16), pad], axis=0)) for i in ids]
    a_ab = [jnp.where(strict, m4[i][0:ch, 0:ch], 0.0) for i in ids]
    a_ak = [jnp.where(strict, m4[i][0:ch, LANES:LANES + ch], 0.0) for i in ids]
    a_rb = [jnp.where(incl, m4[i][ch:2 * ch, 0:ch], 0.0) for i in ids]
    a_rk = [jnp.where(incl, m4[i][ch:2 * ch, LANES:LANES + ch], 0.0) for i in ids]
    avk = [_mm(jnp.concatenate([a_ak[i], a_rk[i]], axis=0), vv[i]) for i in ids]
    pw = a_ab
    tx = [jnp.concatenate([at[i], avk[i][0:ch, :]], axis=1) for i in ids]
    span = 1
    while 2 * span < ch:
        x = [_mm(pw[i], jnp.concatenate([tx[i], pw[i]], axis=1)) for i in ids]
        pw = [x[i][:, 2 * ch:3 * ch] for i in ids]
        tx = [tx[i] + x[i][:, 0:2 * ch] for i in ids]
        span *= 2
    tx = [tx[i] + _mm(pw[i], tx[i]) for i in ids]
    ry = [_mm(a_rb[i], tx[i]) for i in ids]
    rhat = [rt[i] + ry[i][:, 0:ch] for i in ids]
    yhat = [avk[i][ch:2 * ch, :] + ry[i][:, ch:2 * ch] for i in ids]
    z = [_mm(tx[i][:, 0:ch].T, bbar[i]) for i in ids]
    kv = [_mm(jnp.concatenate([vv[i], tx[i][:, ch:2 * ch]], axis=0).T,
              jnp.concatenate([kbar[i], bbar[i]], axis=0)) for i in ids]

    s_cur = list(s_cur)
    for c in range(n_ch):
        base = c * RWKV_HEADS
        ys = [_mm_nt(rhat[base + h], s_cur[h]) for h in range(RWKV_HEADS)]
        sz = [_mm(s_cur[h], z[base + h]) for h in range(RWKV_HEADS)]
        for h in range(RWKV_HEADS):
            hs = slice(h * HEAD_DIM, (h + 1) * HEAD_DIM)
            y_s[r0 + c * ch:r0 + (c + 1) * ch, hs] = ys[h] + yhat[base + h]
            s_cur[h] = s_cur[h] * w_c[r0 // ch + c][0:1, hs] + sz[h] + kv[base + h]
    return s_cur


def _rwkv_call(p, prm, batch, seq):
    tm = 512
    sub = 256
    nt = seq // tm
    kern = functools.partial(_rwkv_kernel, tm=tm, sub=sub)

    def rows(width, cstart):
        return pl.BlockSpec((tm, width), lambda b, i: (b * nt + i, cstart // width))

    def full(a):
        return pl.BlockSpec(a.shape, lambda b, i: (0,) * a.ndim)

    params = [prm[n] for n in ("w0", "w2", "a0", "a2", "g2", "k_k", "k_a", "r_k", "ln_w", "ln_b")]
    big = pltpu.VMEM((tm, RWKV_W), F32)
    return pl.pallas_call(
        kern,
        grid=(batch, nt),
        in_specs=[rows(RWKV_W, C_RR), rows(RWKV_W, C_RK), rows(RWKV_W, C_RV), rows(LORA_W, C_LORA)]
        + [full(a) for a in params],
        out_specs=pl.BlockSpec((tm, RWKV_W), lambda b, i: (b * nt + i, 0)),
        out_shape=jax.ShapeDtypeStruct((batch * seq, RWKV_W), F32),
        scratch_shapes=[
            pltpu.VMEM((RWKV_HEADS, HEAD_DIM, HEAD_DIM), F32),
            big, big, big, big, big, big, big,
            pltpu.VMEM((tm // RWKV_CHUNK, 8, RWKV_W), F32),
            big,
        ],
        compiler_params=_cparams(("arbitrary", "arbitrary")),
        name="rwkv7_mix",
    )(p, p, p, p, *params)


FFN_HALO = 16


MXU_K = 256


def _ffn_chunks(d_ff):
    cut = (d_ff // MXU_K + 1) // 2 * MXU_K
    return ((0, cut), (cut, d_ff))


def _ffn_kernel(yf_ref, yfp_ref, yr_ref, yrp_ref, ym_ref, ymp_ref, x_ref, xp_ref, mod_ref, nw_ref, wo_ref,
                wu_ref, cw_ref, cb_ref, wd_ref, nf_ref, o_ref, *, nt, d_ff, final):
    i = pl.program_id(0)
    m = mod_ref[0]

    def ext(prev_ref, cur_ref):
        return jnp.concatenate([prev_ref[...], cur_ref[...]], axis=0)

    z = _dot(ext(yfp_ref, yf_ref).astype(BF16), wo_ref[0:FOX_W, :])
    z = z + _dot(ext(yrp_ref, yr_ref).astype(BF16), wo_ref[FOX_W:FOX_W + RWKV_W, :])
    z = z + _dot(ext(ymp_ref, ym_ref).astype(BF16), wo_ref[FOX_W + RWKV_W:, :])
    x1 = ext(xp_ref, x_ref) + m[2:3, :] * z
    h = _rmsnorm(x1, nw_ref[...]) * (1.0 + m[4:5, :]) + m[3:4, :]
    keep = jnp.where(i % nt == 0, 0.0, 1.0)
    he = jnp.concatenate([h[0:FFN_HALO, :] * keep, h[FFN_HALO:, :]], axis=0).astype(BF16)

    def conv(u, lo, hi):
        cw = cw_ref[:, lo:hi]
        return (cb_ref[:, lo:hi] + cw[0:1, :] * pltpu.roll(u, 2, 0)[FFN_HALO:, :]
                + cw[1:2, :] * pltpu.roll(u, 1, 0)[FFN_HALO:, :] + cw[2:3, :] * u[FFN_HALO:, :])

    chunks = _ffn_chunks(d_ff)
    ups = [(_dot(he, wu_ref[:, lo:hi]), _dot(he, wu_ref[:, d_ff + lo:d_ff + hi])) for lo, hi in chunks]
    y = None
    for (lo, hi), (ug, uv) in zip(chunks, ups):
        g = conv(ug, lo, hi)
        act = g * _sigmoid(g) * conv(uv, d_ff + lo, d_ff + hi)
        part = _dot(act.astype(BF16), wd_ref[lo:hi, :])
        y = part if y is None else y + part
    out = x1[FFN_HALO:, :] + m[5:6, :] * y
    if final:
        out = _rmsnorm(out, nf_ref[...])
    o_ref[...] = out


def _ffn_call(y_fox, y_rwkv, y_moba, xf, mod_l, norm_w, w_out_b, w_up_b, conv_w, conv_b, w_down_b, norm_final,
              layer, seq, final):
    rows, d = xf.shape
    d_ff = w_down_b.shape[1]
    tm = 512
    nt = seq // tm
    hb = tm // FFN_HALO
    kern = functools.partial(_ffn_kernel, nt=nt, d_ff=d_ff, final=final)

    def tile_and_halo(width):
        return [pl.BlockSpec((tm, width), lambda i: (i, 0)),
                pl.BlockSpec((FFN_HALO, width), lambda i: (jnp.maximum(i * hb - 1, 0), 0))]

    def resident(a):
        return pl.BlockSpec((None,) + a.shape[1:], lambda i: (layer, 0, 0), pipeline_mode=pl.Buffered(1))

    return pl.pallas_call(
        kern,
        grid=(rows // tm,),
        in_specs=tile_and_halo(FOX_W) + tile_and_halo(RWKV_W) + tile_and_halo(MOBA_W) + tile_and_halo(d) + [
            pl.BlockSpec((1, 6, d), lambda i: (i // nt, 0, 0)),
            pl.BlockSpec((1, d), lambda i: (0, 0)),
            resident(w_out_b), resident(w_up_b),
            pl.BlockSpec((CONV_W, 2 * d_ff), lambda i: (0, 0)),
            pl.BlockSpec((1, 2 * d_ff), lambda i: (0, 0)),
            resident(w_down_b),
            pl.BlockSpec((1, d), lambda i: (0, 0)),
        ],
        out_specs=pl.BlockSpec((tm, d), lambda i: (i, 0)),
        out_shape=jax.ShapeDtypeStruct((rows, d), F32),
        compiler_params=_cparams(("arbitrary",)),
        name="out_proj_conv_ffn",
    )(y_fox, y_fox, y_rwkv, y_rwkv, y_moba, y_moba, xf, xf, mod_l, norm_w, w_out_b, w_up_b, conv_w, conv_b,
      w_down_b, norm_final)


def kernel(x, c, w_mod, b_mod, norm_mix, w_in, fox_f_bias, rwkv_mu, rwkv_w0, rwkv_w2, rwkv_a0, rwkv_a2,
           rwkv_g2, rwkv_k_k, rwkv_k_a, rwkv_r_k, rwkv_ln_w, rwkv_ln_b, w_out, norm_ffn, w_up, conv_w,
           conv_b, w_down, norm_final):
    batch, seq, d = x.shape
    n_layers = w_mod.shape[0]

    mod = _mod_call(c, w_mod, b_mod).reshape(n_layers, batch, 6, d)
    w_in_p = _win_layout_call(w_in)
    f_bias = jnp.pad(fox_f_bias, ((0, 0), (0, FF_PAD - FOX_HEADS)))
    w_out_b = w_out.astype(BF16)
    w_up_b = w_up.astype(BF16)
    w_down_b = w_down.astype(BF16)
    g2_b = rwkv_g2.astype(BF16)

    xf = x.reshape(batch * seq, d)
    for l in range(n_layers):
        row = lambda a: a[l].reshape(1, -1)
        prm = {
            "w0": row(rwkv_w0), "w2": rwkv_w2[l], "a0": row(rwkv_a0), "a2": rwkv_a2[l], "g2": g2_b[l],
            "k_k": row(rwkv_k_k), "k_a": row(rwkv_k_a), "r_k": row(rwkv_r_k), "ln_w": row(rwkv_ln_w),
            "ln_b": row(rwkv_ln_b),
        }
        p = _inproj_call(xf, mod[l], row(norm_mix), w_in_p, row(rwkv_mu), l, seq)
        y_fox = _fox_call(p, f_bias[l:l + 1], batch, seq)
        y_moba = _moba_call(p, batch, seq)
        y_rwkv = _rwkv_call(p, prm, batch, seq)
        xf = _ffn_call(y_fox, y_rwkv, y_moba, xf, mod[l], row(norm_ffn), w_out_b, w_up_b, conv_w[l],
                       conv_b[l].reshape(1, -1), w_down_b, norm_final.reshape(1, -1), l, seq,
                       final=(l == n_layers - 1))
    return xf.reshape(batch, seq, d)
```

```python
import functools

import jax
import jax.numpy as jnp
import numpy as np
from jax import lax
from jax.experimental import pallas as pl
from jax.experimental.pallas import tpu as pltpu

F32 = jnp.float32
BF16 = jnp.bfloat16
HI = lax.Precision.HIGHEST

HEAD_DIM = 64
FOX_HEADS = 4
RWKV_HEADS = 8
MOBA_HEADS = 4
FOX_W = FOX_HEADS * HEAD_DIM
RWKV_W = RWKV_HEADS * HEAD_DIM
MOBA_W = MOBA_HEADS * HEAD_DIM
DECAY_LORA = 64
AAA_LORA = 64
GATE_LORA = 128
LORA_W = DECAY_LORA + AAA_LORA + GATE_LORA
MOBA_BLOCK = 256
MOBA_TOPK = 3
CONV_W = 3
NORM_EPS = 1e-6
GN_EPS = 64e-5
LOG2E = float(np.log2(np.e))
ATTN_SCALE = HEAD_DIM ** -0.5 * LOG2E

C_FQ, C_FK, C_FV = 0, 256, 512
C_MQ, C_MK, C_MV = 768, 1024, 1280
C_RR, C_RK, C_RV = 1536, 2048, 2560
C_LORA = 3072
C_FF = 3328
FF_PAD = 128
NP_COLS = C_FF + FF_PAD

RWKV_CHUNK = 64
LANES = 128
SUBLANES = 8
MXU_N = 256
VMEM_LIMIT = 56 * 1024 * 1024


def _cparams(sem):
    return pltpu.CompilerParams(dimension_semantics=sem, vmem_limit_bytes=VMEM_LIMIT)


def _dot(a, b):
    return jnp.dot(a, b, preferred_element_type=F32)


def _dot_hi(a, b):
    return jnp.dot(a, b, precision=HI, preferred_element_type=F32)


def _dot_nt(a, b):
    return lax.dot_general(a, b, (((1,), (1,)), ((), ())), preferred_element_type=F32)


def _dot_nt_hi(a, b):
    return lax.dot_general(a, b, (((1,), (1,)), ((), ())), precision=HI, preferred_element_type=F32)


def _mm(a, b):
    return jnp.dot(a.astype(BF16), b.astype(BF16), preferred_element_type=F32)


def _mm_nt(a, b):
    return _dot_nt(a.astype(BF16), b.astype(BF16))


def _split2(x):
    hi = x.astype(BF16)
    return hi, (x - hi.astype(F32)).astype(BF16)


def _dot_split_lhs(x, w_bf16):
    m = x.shape[0]
    r = _dot(jnp.concatenate(_split2(x), axis=0), w_bf16)
    return r[0:m, :] + r[m:2 * m, :]


def _head_sums(x, ones_blk, exact):
    parts = []
    for c0 in range(0, x.shape[1], MXU_N):
        xc = x[:, c0:c0 + MXU_N]
        parts.append(_dot_split_lhs(xc, ones_blk) if exact else _dot(xc.astype(BF16), ones_blk))
    return jnp.concatenate(parts, axis=1)


def _dot3(a, b):
    m = a.shape[0]
    a_hi, a_lo = _split2(a)
    b_hi, b_lo = _split2(b)
    r = _dot(jnp.concatenate([a_hi, a_lo], axis=0), b_hi)
    return r[0:m, :] + r[m:2 * m, :] + _dot(a_hi, b_lo)


def _sigmoid(x):
    return 0.5 * jnp.tanh(0.5 * x) + 0.5


def _log_sigmoid(x):
    return jnp.minimum(x, 0.0) - jnp.log1p(jnp.exp(-jnp.abs(x)))


def _rmsnorm(x, w):
    ms = jnp.mean(x * x, axis=-1, keepdims=True)
    return x * lax.rsqrt(ms + NORM_EPS) * w


def _seg_cumsum_rows(x, seg):
    row = lax.broadcasted_iota(jnp.int32, x.shape, 0) & (seg - 1)
    s = 1
    while s < seg:
        x = x + jnp.where(row >= s, pltpu.roll(x, s, 0), 0.0)
        s *= 2
    return x


def _mod_kernel(c_ref, w_ref, b_ref, o_ref):
    c = c_ref[...]
    o_ref[0] = _dot_hi(c * _sigmoid(c), w_ref[0]) + b_ref[0]


def _mod_call(c, w_mod, b_mod):
    n_layers, d, n = w_mod.shape
    b = c.shape[0]
    tn = 1536
    return pl.pallas_call(
        _mod_kernel,
        grid=(n_layers, n // tn),
        in_specs=[
            pl.BlockSpec((b, d), lambda l, j: (0, 0)),
            pl.BlockSpec((1, d, tn), lambda l, j: (l, 0, j)),
            pl.BlockSpec((1, 1, tn), lambda l, j: (l, 0, j)),
        ],
        out_specs=pl.BlockSpec((1, b, tn), lambda l, j: (l, 0, j)),
        out_shape=jax.ShapeDtypeStruct((n_layers, b, n), F32),
        compiler_params=_cparams(("arbitrary", "arbitrary")),
        name="adaln_mod",
    )(c, w_mod, b_mod.reshape(n_layers, 1, n))


def _win_layout_kernel(w_ref, o_ref):
    fox_cols = 3 * FOX_W + FOX_HEADS
    rwkv_cols = 3 * RWKV_W + LORA_W
    w = w_ref[...]
    o_ref[:, C_FQ:C_MQ] = w[:, 0:3 * FOX_W].astype(BF16)
    o_ref[:, C_MQ:C_RR] = w[:, fox_cols + rwkv_cols:fox_cols + rwkv_cols + 3 * MOBA_W].astype(BF16)
    o_ref[:, C_RR:C_FF] = w[:, fox_cols:fox_cols + rwkv_cols].astype(BF16)
    lane = lax.broadcasted_iota(jnp.int32, (w.shape[0], FF_PAD), 1)
    o_ref[:, C_FF:NP_COLS] = jnp.where(lane < FOX_HEADS, w[:, 3 * FOX_W:3 * FOX_W + FF_PAD], 0.0).astype(BF16)


def _win_layout_call(w_in):
    n_layers, d, n = w_in.shape
    tr = 256
    return pl.pallas_call(
        _win_layout_kernel,
        grid=(n_layers, d // tr),
        in_specs=[pl.BlockSpec((None, tr, n), lambda l, i: (l, i, 0))],
        out_specs=pl.BlockSpec((None, tr, NP_COLS), lambda l, i: (l, i, 0)),
        out_shape=jax.ShapeDtypeStruct((n_layers, d, NP_COLS), BF16),
        compiler_params=_cparams(("arbitrary", "arbitrary")),
        name="w_in_layout",
    )(w_in)


def _inproj_kernel(x_ref, mod_ref, nw_ref, w_ref, mu_ref, o_ref, prev, *, tm, nt):
    i = pl.program_id(0)
    m = mod_ref[0]
    h = _rmsnorm(x_ref[...], nw_ref[...]) * (1.0 + m[1:2, :]) + m[0:1, :]
    p = _dot(h.astype(BF16), w_ref[...])
    o_ref[:, 0:C_RR] = p[:, 0:C_RR]
    o_ref[:, C_FF:NP_COLS] = p[:, C_FF:NP_COLS]
    feat = p[:, C_RR:C_FF]

    @pl.when(i % nt == 0)
    def _():
        prev[...] = jnp.zeros(prev.shape, F32)

    rolled = pltpu.roll(feat, 1, 0)
    first = lax.broadcasted_iota(jnp.int32, (SUBLANES, C_FF - C_RR), 0) == 0
    top = jnp.where(first, prev[...], rolled[0:SUBLANES, :])
    shifted = jnp.concatenate([top, rolled[SUBLANES:, :]], axis=0)
    prev[...] = feat[tm - 1:tm, :]
    o_ref[:, C_RR:C_FF] = feat + (shifted - feat) * mu_ref[...]


def _inproj_call(xf, mod_l, norm_w, w_in_p, mu_row, layer, seq):
    rows, d = xf.shape
    tm = 512
    nt = seq // tm
    kern = functools.partial(_inproj_kernel, tm=tm, nt=nt)
    return pl.pallas_call(
        kern,
        grid=(rows // tm,),
        in_specs=[
            pl.BlockSpec((tm, d), lambda i: (i, 0)),
            pl.BlockSpec((1, 6, d), lambda i: (i // nt, 0, 0)),
            pl.BlockSpec((1, d), lambda i: (0, 0)),
            pl.BlockSpec((None, d, NP_COLS), lambda i: (layer, 0, 0)),
            pl.BlockSpec((1, C_FF - C_RR), lambda i: (0, 0)),
        ],
        out_specs=pl.BlockSpec((tm, NP_COLS), lambda i: (i, 0)),
        out_shape=jax.ShapeDtypeStruct((rows, NP_COLS), F32),
        scratch_shapes=[pltpu.VMEM((1, C_FF - C_RR), F32)],
        compiler_params=_cparams(("arbitrary",)),
        name="in_proj",
    )(xf, mod_l, norm_w, w_in_p, mu_row)


def _softmax_tiles(s, carry, v_t):
    m_new = [jnp.maximum(c[0], jnp.max(x, axis=0, keepdims=True)) for x, c in zip(s, carry)]
    p = [jnp.exp2(x - m) for x, m in zip(s, m_new)]
    pv = [_dot(v, x.astype(BF16)) for v, x in zip(v_t, p)]
    out = []
    for (m, l, acc), mn, x, y in zip(carry, m_new, p, pv):
        alpha = jnp.exp2(m - mn)
        out.append((mn, alpha * l + jnp.sum(x, axis=0, keepdims=True), alpha * acc + y))
    return out


def _first_tiles(s, v_t):
    m = [jnp.max(x, axis=0, keepdims=True) for x in s]
    p = [jnp.exp2(x - mm) for x, mm in zip(s, m)]
    pv = [_dot(v, x.astype(BF16)) for v, x in zip(v_t, p)]
    return [(mm, jnp.sum(x, axis=0, keepdims=True), y) for mm, x, y in zip(m, p, pv)]


def _flatten(carry):
    return tuple(a for c in carry for a in c)


def _unflatten(flat):
    return [tuple(flat[3 * h:3 * h + 3]) for h in range(len(flat) // 3)]


def _store_heads(o_ref, carry):
    o_ref[...] = jnp.concatenate([acc / l for _, l, acc in carry], axis=0).T


F_COL = HEAD_DIM
N_SPLIT = 3
FOX_XW = FOX_W + FF_PAD


def _fox_selectors():
    selk = np.zeros((FOX_XW, FOX_HEADS * LANES), np.float32)
    for h in range(FOX_HEADS):
        for c in range(HEAD_DIM):
            selk[h * HEAD_DIM + c, h * LANES + c] = 1.0
    selq = selk.T.copy()
    for h in range(FOX_HEADS):
        for s in range(N_SPLIT):
            selk[FOX_W + FOX_HEADS * s + h, h * LANES + F_COL + s] = 1.0
            selq[h * LANES + F_COL + N_SPLIT + s, FOX_W + FOX_HEADS * s + h] = 1.0
    return jnp.asarray(selk, BF16), jnp.asarray(selq, BF16)


def _fox_kernel(q_ref, k_ref, v_ref, ff_ref, fb_ref, selk_ref, selq_ref, o_ref, ka, v_t, fp, s_buf, *, tq, tk,
                n_kv):
    qi = pl.program_id(1)
    heads = range(FOX_HEADS)

    @pl.when(qi == 0)
    def _():
        lane = lax.broadcasted_iota(jnp.int32, (1, LANES), 1)
        f = _seg_cumsum_rows(_log_sigmoid(ff_ref[...] + fb_ref[...]), n_kv * tk) * LOG2E
        f = jnp.where(lane < FOX_HEADS, f, 0.0)
        hi = f.astype(BF16).astype(F32)
        mid = (f - hi).astype(BF16).astype(F32)
        lo = (f - hi - mid).astype(BF16).astype(F32)
        fp[...] = (hi + pltpu.roll(mid, FOX_HEADS, 1) + pltpu.roll(lo, 2 * FOX_HEADS, 1)).astype(BF16)
        x = jnp.concatenate([k_ref[...].astype(BF16), fp[...]], axis=1)
        lane4 = lax.broadcasted_iota(jnp.int32, (1, FOX_HEADS * LANES), 1) % LANES
        ones = jnp.where((lane4 >= F_COL + N_SPLIT) & (lane4 < F_COL + 2 * N_SPLIT), 1.0, 0.0)
        ka[...] = (_dot(x, selk_ref[...]) + ones).astype(BF16)
        for j in range(n_kv):
            v_t[j] = v_ref[j * tk:(j + 1) * tk, :].T.astype(BF16)

    q0 = pl.multiple_of(qi * tq, tq)
    xq = jnp.concatenate([(q_ref[...] * ATTN_SCALE).astype(BF16), fp[pl.ds(q0, tq), :]], axis=1)
    sub = lax.broadcasted_iota(jnp.int32, (FOX_HEADS * LANES, tq), 0) % LANES
    neg = jnp.where((sub >= F_COL) & (sub < F_COL + N_SPLIT), -1.0, 0.0)
    q_all = (_dot_nt(selq_ref[...], xq) + neg).astype(BF16)
    q_t = [q_all[h * LANES:(h + 1) * LANES, :] for h in heads]
    lag = (lax.broadcasted_iota(jnp.int32, (tk, tq), 0) - lax.broadcasted_iota(jnp.int32, (tk, tq), 1))
    n_diag = tq // tk

    def logits(j):
        k0 = pl.multiple_of(j * tk, tk)
        return [_dot(ka[pl.ds(k0, tk), h * LANES:(h + 1) * LANES], q_t[h]) for h in heads]

    def values(j):
        vj = v_t[j]
        return [vj[h * HEAD_DIM:(h + 1) * HEAD_DIM, :] for h in heads]

    def put(slot, s):
        for h in heads:
            s_buf[slot * FOX_HEADS + h] = s[h]

    j0 = qi * n_diag
    last = jnp.maximum(j0 - 1, 0)
    s_diag = [logits(j0 + dd) for dd in range(n_diag)]
    put(0, logits(0))
    carry = _first_tiles([jnp.where(lag <= 0, x, -jnp.inf) for x in s_diag[0]], values(j0))
    for dd in range(1, n_diag):
        carry = _softmax_tiles([jnp.where(lag <= -dd * tk, x, -jnp.inf) for x in s_diag[dd]], carry,
                               values(j0 + dd))

    def step(j, slot, carry):
        put(1 - slot, logits(jnp.minimum(j + 1, last)))
        s_cur = [s_buf[slot * FOX_HEADS + h] for h in heads]
        return _softmax_tiles(s_cur, carry, values(j))

    def body(jj, flat):
        carry = _unflatten(flat)
        for slot in range(2):
            carry = step(2 * jj + slot, slot, carry)
        return _flatten(carry)

    assert n_diag % 2 == 0
    _store_heads(o_ref, _unflatten(lax.fori_loop(0, j0 // 2, body, _flatten(carry))))


def _fox_call(p, f_bias_row, batch, seq):
    tq = 512
    tk = 256
    nq = seq // tq
    selk, selq = _fox_selectors()
    kern = functools.partial(_fox_kernel, tq=tq, tk=tk, n_kv=seq // tk)
    return pl.pallas_call(
        kern,
        grid=(batch, nq),
        in_specs=[
            pl.BlockSpec((tq, FOX_W), lambda b, i: (b * nq + i, C_FQ // FOX_W)),
            pl.BlockSpec((seq, FOX_W), lambda b, i: (b, C_FK // FOX_W)),
            pl.BlockSpec((seq, FOX_W), lambda b, i: (b, C_FV // FOX_W)),
            pl.BlockSpec((seq, FF_PAD), lambda b, i: (b, C_FF // FF_PAD)),
            pl.BlockSpec((1, FF_PAD), lambda b, i: (0, 0)),
            pl.BlockSpec(selk.shape, lambda b, i: (0, 0)),
            pl.BlockSpec(selq.shape, lambda b, i: (0, 0)),
        ],
        out_specs=pl.BlockSpec((tq, FOX_W), lambda b, i: (b * nq + i, 0)),
        out_shape=jax.ShapeDtypeStruct((batch * seq, FOX_W), F32),
        scratch_shapes=[
            pltpu.VMEM((seq, FOX_HEADS * LANES), BF16),
            pltpu.VMEM((seq // tk, FOX_W, tk), BF16),
            pltpu.VMEM((seq, FF_PAD), BF16),
            pltpu.VMEM((2 * FOX_HEADS, tk, tq), F32),
        ],
        compiler_params=_cparams(("arbitrary", "arbitrary")),
        name="fox_attention",
    )(p, p, p, p, f_bias_row, selk, selq)


def _moba_selectors():
    selk = np.zeros((MOBA_W, MOBA_HEADS * LANES), np.float32)
    for h in range(MOBA_HEADS):
        for c in range(HEAD_DIM):
            selk[h * HEAD_DIM + c, h * LANES + c] = 1.0
    return jnp.asarray(selk, BF16), jnp.asarray(selk.T, BF16)


def _moba_kernel(q_ref, k_ref, v_ref, selk_ref, selq_ref, o_ref, ka, v_t, kmean, s_buf, *, blk, n_kb):
    own = pl.program_id(1)
    heads = range(MOBA_HEADS)

    @pl.when(own == 0)
    def _():
        kf = k_ref[...]
        ka[...] = _dot(kf.astype(BF16), selk_ref[...]).astype(BF16)
        kmean[...] = jnp.zeros(kmean.shape, F32)
        for n in range(n_kb):
            kmean[n:n + 1, :] = jnp.mean(kf[n * blk:(n + 1) * blk, :], axis=0, keepdims=True)
            v_t[n] = v_ref[n * blk:(n + 1) * blk, :].T.astype(BF16)

    qf = q_ref[...]
    qs = (qf * ATTN_SCALE).astype(BF16)
    q_all = _dot_nt(selq_ref[...], qs).astype(BF16)
    q_t = [q_all[h * LANES:(h + 1) * LANES, :] for h in heads]
    sub = lax.broadcasted_iota(jnp.int32, (SUBLANES, blk), 0)
    past = sub < own
    causal = (lax.broadcasted_iota(jnp.int32, (blk, blk), 0) <= lax.broadcasted_iota(jnp.int32, (blk, blk), 1))

    sel = []
    for h in heads:
        hs = slice(h * HEAD_DIM, (h + 1) * HEAD_DIM)
        gate = _dot_nt_hi(kmean[:, hs], qf[:, hs])
        selm = jnp.zeros((SUBLANES, blk), F32)
        for n in range(n_kb):
            gn = gate[n:n + 1, :]
            beats = past & ((gate > gn) | ((gate == gn) & (sub < n)))
            rank = jnp.sum(beats.astype(F32), axis=0, keepdims=True)
            selm = jnp.where(sub == n, (rank < MOBA_TOPK).astype(F32), selm)
        sel.append(jnp.where(past, selm, 0.0))

    def logits(j):
        k0 = pl.multiple_of(j * blk, blk)
        return [_dot(ka[pl.ds(k0, blk), h * LANES:(h + 1) * LANES], q_t[h]) for h in heads]

    def values(j):
        vj = v_t[j]
        return [vj[h * HEAD_DIM:(h + 1) * HEAD_DIM, :] for h in heads]

    def put(slot, s):
        for h in heads:
            s_buf[slot * MOBA_HEADS + h] = s[h]

    def step(j, slot, carry, live, prefetch):
        if prefetch:
            put(1 - slot, logits(jnp.minimum(j + 1, last)))
        picked = [(jnp.sum(jnp.where(sub == j, sel[h], 0.0), axis=0, keepdims=True) > 0.5) & live for h in heads]
        s = [jnp.where(pk, s_buf[slot * MOBA_HEADS + h], -jnp.inf) for h, pk in zip(heads, picked)]
        return _softmax_tiles(s, carry, values(j))

    last = jnp.maximum(own - 1, 0)
    s_own = logits(own)
    put(0, logits(0))
    carry = _first_tiles([jnp.where(causal, x, -jnp.inf) for x in s_own], values(own))

    def body(jj, flat):
        carry = _unflatten(flat)
        for slot in range(2):
            carry = step(2 * jj + slot, slot, carry, True, True)
        return _flatten(carry)

    carry = _unflatten(lax.fori_loop(0, own // 2, body, _flatten(carry)))
    carry = step(last, 0, carry, own % 2 == 1, False)
    _store_heads(o_ref, carry)


def _moba_call(p, batch, seq):
    blk = MOBA_BLOCK
    n_kb = seq // blk
    assert n_kb <= SUBLANES
    selk, selq = _moba_selectors()
    kern = functools.partial(_moba_kernel, blk=blk, n_kb=n_kb)
    return pl.pallas_call(
        kern,
        grid=(batch, n_kb),
        in_specs=[
            pl.BlockSpec((blk, MOBA_W), lambda b, i: (b * n_kb + i, C_MQ // MOBA_W)),
            pl.BlockSpec((seq, MOBA_W), lambda b, i: (b, C_MK // MOBA_W)),
            pl.BlockSpec((seq, MOBA_W), lambda b, i: (b, C_MV // MOBA_W)),
            pl.BlockSpec(selk.shape, lambda b, i: (0, 0)),
            pl.BlockSpec(selq.shape, lambda b, i: (0, 0)),
        ],
        out_specs=pl.BlockSpec((blk, MOBA_W), lambda b, i: (b * n_kb + i, 0)),
        out_shape=jax.ShapeDtypeStruct((batch * seq, MOBA_W), F32),
        scratch_shapes=[
            pltpu.VMEM((seq, MOBA_HEADS * LANES), BF16),
            pltpu.VMEM((n_kb, MOBA_W, blk), BF16),
            pltpu.VMEM((SUBLANES, MOBA_W), F32),
            pltpu.VMEM((2 * MOBA_HEADS, blk, blk), F32),
        ],
        compiler_params=_cparams(("arbitrary", "arbitrary")),
        name="moba_attention",
    )(p, p, p, selk, selq)


DECAY_SCALE = float(np.exp(-0.5))


def _rwkv_kernel(r_ref, k_ref, v_ref, lo_ref, w0, w2, a0, a2, g2, kkw, kaw, rkw, lnw, lnb, o_ref,
                 state, a_t, r_t, k_t, b_t, k_b, b_b, v_s, w_c, y_s, *, tm, sub):
    i = pl.program_id(1)
    ch = RWKV_CHUNK
    n_ch = sub // ch

    @pl.when(i == 0)
    def _():
        state[...] = jnp.zeros(state.shape, F32)

    gi = lax.broadcasted_iota(jnp.int32, (MXU_N, MXU_N), 0) // HEAD_DIM
    gj = lax.broadcasted_iota(jnp.int32, (MXU_N, MXU_N), 1) // HEAD_DIM
    group = (gi == gj).astype(BF16)
    ti = lax.broadcasted_iota(jnp.int32, (ch, ch), 0)
    tj = lax.broadcasted_iota(jnp.int32, (ch, ch), 1)
    strict = tj < ti
    incl = tj <= ti

    def project(r0):
        lo = lo_ref[r0:r0 + sub, :]
        w_lo = lo[:, 0:DECAY_LORA]
        a_lo = lo[:, DECAY_LORA:DECAY_LORA + AAA_LORA]
        g_lo = lo[:, DECAY_LORA + AAA_LORA:LORA_W]
        kk = k_ref[r0:r0 + sub, :] * kkw[...]
        return (_dot3(jnp.tanh(w_lo), w2[...]), _dot3(a_lo, a2[...]), _dot(_sigmoid(g_lo).astype(BF16), g2[...]),
                _head_sums(kk * kk, group, exact=False))

    def prepare(r0, zw, za, gate, ss):
        span = slice(r0, r0 + sub)
        r = r_ref[span, :]
        k = k_ref[span, :]
        v = v_ref[span, :]

        lw = -DECAY_SCALE * _sigmoid(w0[...] + zw)
        eta = _sigmoid(a0[...] + za)

        kk = k * kkw[...] * jnp.minimum(lax.rsqrt(ss), 1e12)
        kp = k * (1.0 + (eta - 1.0) * kaw[...])
        bb = kk * eta

        lc = _seg_cumsum_rows(lw, ch)
        a_t[span, :] = -kk * jnp.exp(lc - lw)
        r_t[span, :] = r * jnp.exp(lc)
        einv = jnp.exp(-lc)
        k_t[span, :] = kp * einv
        b_t[span, :] = bb * einv
        v_s[span, :] = v
        for c in range(n_ch):
            rows = slice(c * ch, (c + 1) * ch)
            dst = slice(r0 + c * ch, r0 + (c + 1) * ch)
            last = lc[(c + 1) * ch - 1:(c + 1) * ch, :]
            e = jnp.exp(last - lc[rows, :])
            k_b[dst, :] = kp[rows, :] * e
            b_b[dst, :] = bb[rows, :] * e
            w_c[r0 // ch + c] = jnp.broadcast_to(jnp.exp(last), (8, RWKV_W))
        return r * kp * rkw[...], v, gate

    def state_free(r0):
        chains = [(c, h) for c in range(n_ch) for h in range(RWKV_HEADS)]

        def tile(ref):
            return [ref[r0 + c * ch:r0 + (c + 1) * ch, h * HEAD_DIM:(h + 1) * HEAD_DIM] for c, h in chains]

        return _rwkv_state_free(tile(a_t), tile(r_t), tile(v_s), tile(k_b), tile(b_b), tile(b_t), tile(k_t),
                                strict, incl)

    def finish(r0, bonus_arg, v, gate):
        span = slice(r0, r0 + sub)
        y = y_s[span, :]
        inv_d = 1.0 / HEAD_DIM
        mean = _head_sums(y, group, exact=False) * inv_d
        d = y - mean
        var = _head_sums(d * d, group, exact=False) * inv_d
        yn = d * lax.rsqrt(var + GN_EPS) * lnw[...] + lnb[...]
        bonus = _head_sums(bonus_arg, group, exact=True) * v
        o_ref[span, :] = (yn + bonus) * gate

    starts = list(range(0, tm, sub))
    s_cur = [state[h] for h in range(RWKV_HEADS)]
    prepared = [prepare(r0, *project(r0)) for r0 in starts]
    pending = []
    for r0 in starts:
        stages = state_free(r0)
        while True:
            try:
                next(stages)
            except StopIteration as done:
                free = done.value
                break
            if pending:
                s_cur = pending.pop(0)(s_cur)
        while pending:
            s_cur = pending.pop(0)(s_cur)
        pending = [functools.partial(_rwkv_state_step, c, r0, free, w_c=w_c, y_s=y_s) for c in range(n_ch)]
    while pending:
        s_cur = pending.pop(0)(s_cur)
    for h in range(RWKV_HEADS):
        state[h] = s_cur[h]
    for r0, vals in zip(starts, prepared):
        finish(r0, *vals)


def _rwkv_state_free(at, rt, vv, kbar, bbar, btl, ktl, strict, incl):
    ch = RWKV_CHUNK
    ids = range(len(at))
    pad = jnp.zeros((LANES - ch, HEAD_DIM), BF16)
    m4 = [_dot_nt(jnp.concatenate([at[i], rt[i]], axis=0).astype(BF16),
                  jnp.concatenate([btl[i].astype(BF16), pad, ktl[i].astype(BF16), pad], axis=0)) for i in ids]
    yield
    a_ab = [jnp.where(strict, m4[i][0:ch, 0:ch], 0.0) for i in ids]
    a_ak = [jnp.where(strict, m4[i][0:ch, LANES:LANES + ch], 0.0) for i in ids]
    a_rb = [jnp.where(incl, m4[i][ch:2 * ch, 0:ch], 0.0) for i in ids]
    a_rk = [jnp.where(incl, m4[i][ch:2 * ch, LANES:LANES + ch], 0.0) for i in ids]
    avk = [_mm(jnp.concatenate([a_ak[i], a_rk[i]], axis=0), vv[i]) for i in ids]
    yield
    pw = a_ab
    tx = [jnp.concatenate([at[i], avk[i][0:ch, :]], axis=1) for i in ids]
    span = 1
    while 2 * span < ch:
        x = [_mm(pw[i], jnp.concatenate([tx[i], pw[i]], axis=1)) for i in ids]
        pw = [x[i][:, 2 * ch:3 * ch] for i in ids]
        tx = [tx[i] + x[i][:, 0:2 * ch] for i in ids]
        span *= 2
        yield
    tx = [tx[i] + _mm(pw[i], tx[i]) for i in ids]
    yield
    ry = [_mm(a_rb[i], tx[i]) for i in ids]
    rhat = [rt[i] + ry[i][:, 0:ch] for i in ids]
    yhat = [avk[i][ch:2 * ch, :] + ry[i][:, ch:2 * ch] for i in ids]
    yield
    z = [_mm(tx[i][:, 0:ch].T, bbar[i]) for i in ids]
    yield
    kv = [_mm(jnp.concatenate([vv[i], tx[i][:, ch:2 * ch]], axis=0).T,
              jnp.concatenate([kbar[i], bbar[i]], axis=0)) for i in ids]
    return rhat, yhat, z, kv


def _rwkv_state_step(c, r0, free, s_cur, w_c, y_s):
    ch = RWKV_CHUNK
    rhat, yhat, z, kv = free
    base = c * RWKV_HEADS
    ys = [_mm_nt(rhat[base + h], s_cur[h]) for h in range(RWKV_HEADS)]
    sz = [_mm(s_cur[h], z[base + h]) for h in range(RWKV_HEADS)]
    out = []
    for h in range(RWKV_HEADS):
        hs = slice(h * HEAD_DIM, (h + 1) * HEAD_DIM)
        y_s[r0 + c * ch:r0 + (c + 1) * ch, hs] = ys[h] + yhat[base + h]
        out.append(s_cur[h] * w_c[r0 // ch + c][0:1, hs] + sz[h] + kv[base + h])
    return out


def _rwkv_call(p, prm, batch, seq):
    tm = 512
    sub = 256
    nt = seq // tm
    kern = functools.partial(_rwkv_kernel, tm=tm, sub=sub)

    def rows(width, cstart):
        return pl.BlockSpec((tm, width), lambda b, i: (b * nt + i, cstart // width))

    def full(a):
        return pl.BlockSpec(a.shape, lambda b, i: (0,) * a.ndim)

    params = [prm[n] for n in ("w0", "w2", "a0", "a2", "g2", "k_k", "k_a", "r_k", "ln_w", "ln_b")]
    big = pltpu.VMEM((tm, RWKV_W), F32)
    return pl.pallas_call(
        kern,
        grid=(batch, nt),
        in_specs=[rows(RWKV_W, C_RR), rows(RWKV_W, C_RK), rows(RWKV_W, C_RV), rows(LORA_W, C_LORA)]
        + [full(a) for a in params],
        out_specs=pl.BlockSpec((tm, RWKV_W), lambda b, i: (b * nt + i, 0)),
        out_shape=jax.ShapeDtypeStruct((batch * seq, RWKV_W), F32),
        scratch_shapes=[
            pltpu.VMEM((RWKV_HEADS, HEAD_DIM, HEAD_DIM), F32),
            big, big, big, big, big, big, big,
            pltpu.VMEM((tm // RWKV_CHUNK, 8, RWKV_W), F32),
            big,
        ],
        compiler_params=_cparams(("arbitrary", "arbitrary")),
        name="rwkv7_mix",
    )(p, p, p, p, *params)


FFN_HALO = 16


MXU_K = 256


def _ffn_chunks(d_ff):
    cut = (d_ff // MXU_K + 1) // 2 * MXU_K
    return ((0, cut), (cut, d_ff))


def _ffn_kernel(yf_ref, yfp_ref, yr_ref, yrp_ref, ym_ref, ymp_ref, x_ref, xp_ref, mod_ref, nw_ref, wo_ref,
                wu_ref, cw_ref, cb_ref, wd_ref, nf_ref, o_ref, *, nt, d_ff, final):
    i = pl.program_id(0)
    m = mod_ref[0]

    def ext(prev_ref, cur_ref):
        return jnp.concatenate([prev_ref[...], cur_ref[...]], axis=0)

    z = _dot(ext(yfp_ref, yf_ref).astype(BF16), wo_ref[0:FOX_W, :])
    z = z + _dot(ext(yrp_ref, yr_ref).astype(BF16), wo_ref[FOX_W:FOX_W + RWKV_W, :])
    z = z + _dot(ext(ymp_ref, ym_ref).astype(BF16), wo_ref[FOX_W + RWKV_W:, :])
    x1 = ext(xp_ref, x_ref) + m[2:3, :] * z
    h = _rmsnorm(x1, nw_ref[...]) * (1.0 + m[4:5, :]) + m[3:4, :]
    keep = jnp.where(i % nt == 0, 0.0, 1.0)
    he = jnp.concatenate([h[0:FFN_HALO, :] * keep, h[FFN_HALO:, :]], axis=0).astype(BF16)

    def conv(u, lo, hi):
        cw = cw_ref[:, lo:hi]
        return (cb_ref[:, lo:hi] + cw[0:1, :] * pltpu.roll(u, 2, 0)[FFN_HALO:, :]
                + cw[1:2, :] * pltpu.roll(u, 1, 0)[FFN_HALO:, :] + cw[2:3, :] * u[FFN_HALO:, :])

    chunks = _ffn_chunks(d_ff)
    ups = [(_dot(he, wu_ref[:, lo:hi]), _dot(he, wu_ref[:, d_ff + lo:d_ff + hi])) for lo, hi in chunks]
    y = None
    for (lo, hi), (ug, uv) in zip(chunks, ups):
        g = conv(ug, lo, hi)
        act = g * _sigmoid(g) * conv(uv, d_ff + lo, d_ff + hi)
        part = _dot(act.astype(BF16), wd_ref[lo:hi, :])
        y = part if y is None else y + part
    out = x1[FFN_HALO:, :] + m[5:6, :] * y
    if final:
        out = _rmsnorm(out, nf_ref[...])
    o_ref[...] = out


def _ffn_call(y_fox, y_rwkv, y_moba, xf, mod_l, norm_w, w_out_b, w_up_b, conv_w, conv_b, w_down_b, norm_final,
              layer, seq, final):
    rows, d = xf.shape
    d_ff = w_down_b.shape[1]
    tm = 512
    nt = seq // tm
    hb = tm // FFN_HALO
    kern = functools.partial(_ffn_kernel, nt=nt, d_ff=d_ff, final=final)

    def tile_and_halo(width):
        return [pl.BlockSpec((tm, width), lambda i: (i, 0)),
                pl.BlockSpec((FFN_HALO, width), lambda i: (jnp.maximum(i * hb - 1, 0), 0))]

    def resident(a):
        return pl.BlockSpec((None,) + a.shape[1:], lambda i: (layer, 0, 0), pipeline_mode=pl.Buffered(1))

    return pl.pallas_call(
        kern,
        grid=(rows // tm,),
        in_specs=tile_and_halo(FOX_W) + tile_and_halo(RWKV_W) + tile_and_halo(MOBA_W) + tile_and_halo(d) + [
            pl.BlockSpec((1, 6, d), lambda i: (i // nt, 0, 0)),
            pl.BlockSpec((1, d), lambda i: (0, 0)),
            resident(w_out_b), resident(w_up_b),
            pl.BlockSpec((CONV_W, 2 * d_ff), lambda i: (0, 0)),
            pl.BlockSpec((1, 2 * d_ff), lambda i: (0, 0)),
            resident(w_down_b),
            pl.BlockSpec((1, d), lambda i: (0, 0)),
        ],
        out_specs=pl.BlockSpec((tm, d), lambda i: (i, 0)),
        out_shape=jax.ShapeDtypeStruct((rows, d), F32),
        compiler_params=_cparams(("arbitrary",)),
        name="out_proj_conv_ffn",
    )(y_fox, y_fox, y_rwkv, y_rwkv, y_moba, y_moba, xf, xf, mod_l, norm_w, w_out_b, w_up_b, conv_w, conv_b,
      w_down_b, norm_final)


def kernel(x, c, w_mod, b_mod, norm_mix, w_in, fox_f_bias, rwkv_mu, rwkv_w0, rwkv_w2, rwkv_a0, rwkv_a2,
           rwkv_g2, rwkv_k_k, rwkv_k_a, rwkv_r_k, rwkv_ln_w, rwkv_ln_b, w_out, norm_ffn, w_up, conv_w,
           conv_b, w_down, norm_final):
    batch, seq, d = x.shape
    n_layers = w_mod.shape[0]

    mod = _mod_call(c, w_mod, b_mod).reshape(n_layers, batch, 6, d)
    w_in_p = _win_layout_call(w_in)
    f_bias = jnp.pad(fox_f_bias, ((0, 0), (0, FF_PAD - FOX_HEADS)))
    w_out_b = w_out.astype(BF16)
    w_up_b = w_up.astype(BF16)
    w_down_b = w_down.astype(BF16)
    g2_b = rwkv_g2.astype(BF16)

    xf = x.reshape(batch * seq, d)
    for l in range(n_layers):
        row = lambda a: a[l].reshape(1, -1)
        prm = {
            "w0": row(rwkv_w0), "w2": rwkv_w2[l], "a0": row(rwkv_a0), "a2": rwkv_a2[l], "g2": g2_b[l],
            "k_k": row(rwkv_k_k), "k_a": row(rwkv_k_a), "r_k": row(rwkv_r_k), "ln_w": row(rwkv_ln_w),
            "ln_b": row(rwkv_ln_b),
        }
        p = _inproj_call(xf, mod[l], row(norm_mix), w_in_p, row(rwkv_mu), l, seq)
        y_fox = _fox_call(p, f_bias[l:l + 1], batch, seq)
        y_moba = _moba_call(p, batch, seq)
        y_rwkv = _rwkv_call(p, prm, batch, seq)
        xf = _ffn_call(y_fox, y_rwkv, y_moba, xf, mod[l], row(norm_ffn), w_out_b, w_up_b, conv_w[l],
                       conv_b[l].reshape(1, -1), w_down_b, norm_final.reshape(1, -1), l, seq,
                       final=(l == n_layers - 1))
    return xf.reshape(batch, seq, d)
```

```python
import functools

import jax
import jax.numpy as jnp
import numpy as np
from jax import lax
from jax.experimental import pallas as pl
from jax.experimental.pallas import tpu as pltpu

F32 = jnp.float32
BF16 = jnp.bfloat16

HEAD_DIM = 64
FOX_HEADS = 4
RWKV_HEADS = 8
MOBA_HEADS = 4
FOX_W = FOX_HEADS * HEAD_DIM
RWKV_W = RWKV_HEADS * HEAD_DIM
MOBA_W = MOBA_HEADS * HEAD_DIM
DECAY_LORA = 64
AAA_LORA = 64
GATE_LORA = 128
LORA_W = DECAY_LORA + AAA_LORA + GATE_LORA
MOBA_BLOCK = 256
MOBA_TOPK = 3
CONV_W = 3
NORM_EPS = 1e-6
GN_EPS = 64e-5
LOG2E = float(np.log2(np.e))
ATTN_SCALE = HEAD_DIM ** -0.5 * LOG2E

C_FQ, C_FK, C_FV = 0, 256, 512
C_MQ, C_MK, C_MV = 768, 1024, 1280
C_RR, C_RK, C_RV = 1536, 2048, 2560
C_LORA = 3072
C_FF = 3328
FF_PAD = 128
NP_COLS = C_FF + FF_PAD

RWKV_CHUNK = 64
LANES = 128
SUBLANES = 8
MXU_N = 256
VMEM_LIMIT = 56 * 1024 * 1024


def _cparams(sem):
    return pltpu.CompilerParams(dimension_semantics=sem, vmem_limit_bytes=VMEM_LIMIT)


def _dot(a, b):
    return jnp.dot(a, b, preferred_element_type=F32)


def _dot_nt(a, b):
    return lax.dot_general(a, b, (((1,), (1,)), ((), ())), preferred_element_type=F32)


def _mm(a, b):
    return jnp.dot(a.astype(BF16), b.astype(BF16), preferred_element_type=F32)


def _mm_nt(a, b):
    return _dot_nt(a.astype(BF16), b.astype(BF16))


def _split2(x):
    hi = x.astype(BF16)
    return hi, (x - hi.astype(F32)).astype(BF16)


def _dot_split_lhs(x, w_bf16):
    m = x.shape[0]
    r = _dot(jnp.concatenate(_split2(x), axis=0), w_bf16)
    return r[0:m, :] + r[m:2 * m, :]


def _head_sums(x, ones_blk, exact):
    parts = []
    for c0 in range(0, x.shape[1], MXU_N):
        xc = x[:, c0:c0 + MXU_N]
        parts.append(_dot_split_lhs(xc, ones_blk) if exact else _dot(xc.astype(BF16), ones_blk))
    return jnp.concatenate(parts, axis=1)


def _dot3(a, b):
    m = a.shape[0]
    a_hi, a_lo = _split2(a)
    b_hi, b_lo = _split2(b)
    r = _dot(jnp.concatenate([a_hi, a_lo], axis=0), b_hi)
    return r[0:m, :] + r[m:2 * m, :] + _dot(a_hi, b_lo)


def _dot3_nt(a, b):
    m = a.shape[0]
    a_hi, a_lo = _split2(a)
    b_hi, b_lo = _split2(b)
    r = _dot_nt(jnp.concatenate([a_hi, a_lo], axis=0), b_hi)
    return r[0:m, :] + r[m:2 * m, :] + _dot_nt(a_hi, b_lo)


def _sigmoid(x):
    return 0.5 * jnp.tanh(0.5 * x) + 0.5


def _log_sigmoid(x):
    return jnp.minimum(x, 0.0) - jnp.log1p(jnp.exp(-jnp.abs(x)))


def _rmsnorm(x, w):
    ms = jnp.mean(x * x, axis=-1, keepdims=True)
    return x * lax.rsqrt(ms + NORM_EPS) * w


def _seg_cumsum_rows(x, seg):
    row = lax.broadcasted_iota(jnp.int32, x.shape, 0) & (seg - 1)
    s = 1
    while s < seg:
        x = x + jnp.where(row >= s, pltpu.roll(x, s, 0), 0.0)
        s *= 2
    return x


def _mod_kernel(c_ref, w_ref, b_ref, o_ref):
    c = c_ref[...]
    o_ref[0] = _dot3(c * _sigmoid(c), w_ref[0]) + b_ref[0]


def _mod_call(c, w_mod, b_mod):
    n_layers, d, n = w_mod.shape
    b = c.shape[0]
    tn = 1536
    return pl.pallas_call(
        _mod_kernel,
        grid=(n_layers, n // tn),
        in_specs=[
            pl.BlockSpec((b, d), lambda l, j: (0, 0)),
            pl.BlockSpec((1, d, tn), lambda l, j: (l, 0, j)),
            pl.BlockSpec((1, 1, tn), lambda l, j: (l, 0, j)),
        ],
        out_specs=pl.BlockSpec((1, b, tn), lambda l, j: (l, 0, j)),
        out_shape=jax.ShapeDtypeStruct((n_layers, b, n), F32),
        compiler_params=_cparams(("arbitrary", "arbitrary")),
        name="adaln_mod",
    )(c, w_mod, b_mod.reshape(n_layers, 1, n))


def _win_layout_kernel(w_ref, o_ref):
    fox_cols = 3 * FOX_W + FOX_HEADS
    rwkv_cols = 3 * RWKV_W + LORA_W
    w = w_ref[...]
    o_ref[:, C_FQ:C_MQ] = w[:, 0:3 * FOX_W].astype(BF16)
    o_ref[:, C_MQ:C_RR] = w[:, fox_cols + rwkv_cols:fox_cols + rwkv_cols + 3 * MOBA_W].astype(BF16)
    o_ref[:, C_RR:C_FF] = w[:, fox_cols:fox_cols + rwkv_cols].astype(BF16)
    lane = lax.broadcasted_iota(jnp.int32, (w.shape[0], FF_PAD), 1)
    o_ref[:, C_FF:NP_COLS] = jnp.where(lane < FOX_HEADS, w[:, 3 * FOX_W:3 * FOX_W + FF_PAD], 0.0).astype(BF16)


def _win_layout_call(w_in):
    n_layers, d, n = w_in.shape
    tr = 256
    return pl.pallas_call(
        _win_layout_kernel,
        grid=(n_layers, d // tr),
        in_specs=[pl.BlockSpec((None, tr, n), lambda l, i: (l, i, 0))],
        out_specs=pl.BlockSpec((None, tr, NP_COLS), lambda l, i: (l, i, 0)),
        out_shape=jax.ShapeDtypeStruct((n_layers, d, NP_COLS), BF16),
        compiler_params=_cparams(("arbitrary", "arbitrary")),
        name="w_in_layout",
    )(w_in)


def _inproj_kernel(x_ref, mod_ref, nw_ref, w_ref, mu_ref, o_ref, prev, *, tm, nt):
    i = pl.program_id(0)
    m = mod_ref[0]
    h = _rmsnorm(x_ref[...], nw_ref[...]) * (1.0 + m[1:2, :]) + m[0:1, :]
    p = _dot(h.astype(BF16), w_ref[...])
    o_ref[:, 0:C_RR] = p[:, 0:C_RR]
    o_ref[:, C_FF:NP_COLS] = p[:, C_FF:NP_COLS]
    feat = p[:, C_RR:C_FF]

    @pl.when(i % nt == 0)
    def _():
        prev[...] = jnp.zeros(prev.shape, F32)

    rolled = pltpu.roll(feat, 1, 0)
    first = lax.broadcasted_iota(jnp.int32, (SUBLANES, C_FF - C_RR), 0) == 0
    top = jnp.where(first, prev[...], rolled[0:SUBLANES, :])
    shifted = jnp.concatenate([top, rolled[SUBLANES:, :]], axis=0)
    prev[...] = feat[tm - 1:tm, :]
    o_ref[:, C_RR:C_FF] = feat + (shifted - feat) * mu_ref[...]


def _inproj_call(xf, mod_l, norm_w, w_in_p, mu_row, layer, seq):
    rows, d = xf.shape
    tm = 512
    nt = seq // tm
    kern = functools.partial(_inproj_kernel, tm=tm, nt=nt)
    return pl.pallas_call(
        kern,
        grid=(rows // tm,),
        in_specs=[
            pl.BlockSpec((tm, d), lambda i: (i, 0)),
            pl.BlockSpec((1, 6, d), lambda i: (i // nt, 0, 0)),
            pl.BlockSpec((1, d), lambda i: (0, 0)),
            pl.BlockSpec((None, d, NP_COLS), lambda i: (layer, 0, 0)),
            pl.BlockSpec((1, C_FF - C_RR), lambda i: (0, 0)),
        ],
        out_specs=pl.BlockSpec((tm, NP_COLS), lambda i: (i, 0)),
        out_shape=jax.ShapeDtypeStruct((rows, NP_COLS), F32),
        scratch_shapes=[pltpu.VMEM((1, C_FF - C_RR), F32)],
        compiler_params=_cparams(("arbitrary",)),
        name="in_proj",
    )(xf, mod_l, norm_w, w_in_p, mu_row)


def _softmax_tiles(s, carry, v_t):
    m_new = [jnp.maximum(c[0], jnp.max(x, axis=0, keepdims=True)) for x, c in zip(s, carry)]
    p = [jnp.exp2(x - m) for x, m in zip(s, m_new)]
    pv = [_dot(v, x.astype(BF16)) for v, x in zip(v_t, p)]
    out = []
    for (m, l, acc), mn, x, y in zip(carry, m_new, p, pv):
        alpha = jnp.exp2(m - mn)
        out.append((mn, alpha * l + jnp.sum(x, axis=0, keepdims=True), alpha * acc + y))
    return out


def _first_tiles(s, v_t):
    m = [jnp.max(x, axis=0, keepdims=True) for x in s]
    p = [jnp.exp2(x - mm) for x, mm in zip(s, m)]
    pv = [_dot(v, x.astype(BF16)) for v, x in zip(v_t, p)]
    return [(mm, jnp.sum(x, axis=0, keepdims=True), y) for mm, x, y in zip(m, p, pv)]


def _flatten(carry):
    return tuple(a for c in carry for a in c)


def _unflatten(flat):
    return [tuple(flat[3 * h:3 * h + 3]) for h in range(len(flat) // 3)]


def _store_heads(o_ref, carry):
    o_ref[...] = jnp.concatenate([acc / l for _, l, acc in carry], axis=0).T


F_COL = HEAD_DIM
N_SPLIT = 3
FOX_XW = FOX_W + FF_PAD


def _fox_selectors():
    selk = np.zeros((FOX_XW, FOX_HEADS * LANES), np.float32)
    for h in range(FOX_HEADS):
        for c in range(HEAD_DIM):
            selk[h * HEAD_DIM + c, h * LANES + c] = 1.0
    selq = selk.T.copy()
    for h in range(FOX_HEADS):
        for s in range(N_SPLIT):
            selk[FOX_W + FOX_HEADS * s + h, h * LANES + F_COL + s] = 1.0
            selq[h * LANES + F_COL + N_SPLIT + s, FOX_W + FOX_HEADS * s + h] = 1.0
    return jnp.asarray(selk, BF16), jnp.asarray(selq, BF16)


def _fox_kernel(q_ref, k_ref, v_ref, ff_ref, fb_ref, selk_ref, selq_ref, o_ref, ka, v_t, fp, s_buf, *, tq, tk,
                n_kv):
    qi = pl.program_id(1)
    heads = range(FOX_HEADS)

    @pl.when(qi == 0)
    def _():
        lane = lax.broadcasted_iota(jnp.int32, (1, LANES), 1)
        f = _seg_cumsum_rows(_log_sigmoid(ff_ref[...] + fb_ref[...]), n_kv * tk) * LOG2E
        f = jnp.where(lane < FOX_HEADS, f, 0.0)
        hi = f.astype(BF16).astype(F32)
        mid = (f - hi).astype(BF16).astype(F32)
        lo = (f - hi - mid).astype(BF16).astype(F32)
        fp[...] = (hi + pltpu.roll(mid, FOX_HEADS, 1) + pltpu.roll(lo, 2 * FOX_HEADS, 1)).astype(BF16)
        x = jnp.concatenate([k_ref[...].astype(BF16), fp[...]], axis=1)
        lane4 = lax.broadcasted_iota(jnp.int32, (1, FOX_HEADS * LANES), 1) % LANES
        ones = jnp.where((lane4 >= F_COL + N_SPLIT) & (lane4 < F_COL + 2 * N_SPLIT), 1.0, 0.0)
        ka[...] = (_dot(x, selk_ref[...]) + ones).astype(BF16)
        for j in range(n_kv):
            v_t[j] = v_ref[j * tk:(j + 1) * tk, :].T.astype(BF16)

    q0 = pl.multiple_of(qi * tq, tq)
    xq = jnp.concatenate([(q_ref[...] * ATTN_SCALE).astype(BF16), fp[pl.ds(q0, tq), :]], axis=1)
    sub = lax.broadcasted_iota(jnp.int32, (FOX_HEADS * LANES, tq), 0) % LANES
    neg = jnp.where((sub >= F_COL) & (sub < F_COL + N_SPLIT), -1.0, 0.0)
    q_all = (_dot_nt(selq_ref[...], xq) + neg).astype(BF16)
    q_t = [q_all[h * LANES:(h + 1) * LANES, :] for h in heads]
    lag = (lax.broadcasted_iota(jnp.int32, (tk, tq), 0) - lax.broadcasted_iota(jnp.int32, (tk, tq), 1))
    n_diag = tq // tk

    def logits(j):
        k0 = pl.multiple_of(j * tk, tk)
        return [_dot(ka[pl.ds(k0, tk), h * LANES:(h + 1) * LANES], q_t[h]) for h in heads]

    def values(j):
        vj = v_t[j]
        return [vj[h * HEAD_DIM:(h + 1) * HEAD_DIM, :] for h in heads]

    def put(slot, s):
        for h in heads:
            s_buf[slot * FOX_HEADS + h] = s[h]

    j0 = qi * n_diag
    last = jnp.maximum(j0 - 1, 0)
    s_diag = [logits(j0 + dd) for dd in range(n_diag)]
    put(0, logits(0))
    carry = _first_tiles([jnp.where(lag <= 0, x, -jnp.inf) for x in s_diag[0]], values(j0))
    for dd in range(1, n_diag):
        carry = _softmax_tiles([jnp.where(lag <= -dd * tk, x, -jnp.inf) for x in s_diag[dd]], carry,
                               values(j0 + dd))

    def step(j, slot, carry):
        put(1 - slot, logits(jnp.minimum(j + 1, last)))
        s_cur = [s_buf[slot * FOX_HEADS + h] for h in heads]
        return _softmax_tiles(s_cur, carry, values(j))

    def body(jj, flat):
        carry = _unflatten(flat)
        for slot in range(2):
            carry = step(2 * jj + slot, slot, carry)
        return _flatten(carry)

    assert n_diag % 2 == 0
    _store_heads(o_ref, _unflatten(lax.fori_loop(0, j0 // 2, body, _flatten(carry))))


def _fox_call(p, f_bias_row, batch, seq):
    tq = 512
    tk = 256
    nq = seq // tq
    selk, selq = _fox_selectors()
    kern = functools.partial(_fox_kernel, tq=tq, tk=tk, n_kv=seq // tk)
    return pl.pallas_call(
        kern,
        grid=(batch, nq),
        in_specs=[
            pl.BlockSpec((tq, FOX_W), lambda b, i: (b * nq + i, C_FQ // FOX_W)),
            pl.BlockSpec((seq, FOX_W), lambda b, i: (b, C_FK // FOX_W)),
            pl.BlockSpec((seq, FOX_W), lambda b, i: (b, C_FV // FOX_W)),
            pl.BlockSpec((seq, FF_PAD), lambda b, i: (b, C_FF // FF_PAD)),
            pl.BlockSpec((1, FF_PAD), lambda b, i: (0, 0)),
            pl.BlockSpec(selk.shape, lambda b, i: (0, 0)),
            pl.BlockSpec(selq.shape, lambda b, i: (0, 0)),
        ],
        out_specs=pl.BlockSpec((tq, FOX_W), lambda b, i: (b * nq + i, 0)),
        out_shape=jax.ShapeDtypeStruct((batch * seq, FOX_W), F32),
        scratch_shapes=[
            pltpu.VMEM((seq, FOX_HEADS * LANES), BF16),
            pltpu.VMEM((seq // tk, FOX_W, tk), BF16),
            pltpu.VMEM((seq, FF_PAD), BF16),
            pltpu.VMEM((2 * FOX_HEADS, tk, tq), F32),
        ],
        compiler_params=_cparams(("arbitrary", "arbitrary")),
        name="fox_attention",
    )(p, p, p, p, f_bias_row, selk, selq)


def _moba_selectors():
    selk = np.zeros((MOBA_W, MOBA_HEADS * LANES), np.float32)
    for h in range(MOBA_HEADS):
        for c in range(HEAD_DIM):
            selk[h * HEAD_DIM + c, h * LANES + c] = 1.0
    return jnp.asarray(selk, BF16), jnp.asarray(selk.T, BF16)


def _moba_kernel(q_ref, k_ref, v_ref, selk_ref, selq_ref, o_ref, ka, v_t, kmean, s_buf, *, blk, n_kb):
    own = pl.program_id(1)
    heads = range(MOBA_HEADS)

    @pl.when(own == 0)
    def _():
        kf = k_ref[...]
        ka[...] = _dot(kf.astype(BF16), selk_ref[...]).astype(BF16)
        means = [jnp.mean(kf[n * blk:(n + 1) * blk, :], axis=0, keepdims=True) for n in range(n_kb)]
        if n_kb < SUBLANES:
            means.append(jnp.zeros((SUBLANES - n_kb, MOBA_W), F32))
        means = jnp.concatenate(means, axis=0)
        head_of_lane = lax.broadcasted_iota(jnp.int32, (SUBLANES, MOBA_W), 1) // HEAD_DIM
        for h in heads:
            kmean[h * SUBLANES:(h + 1) * SUBLANES, :] = jnp.where(head_of_lane == h, means, 0.0)
        for n in range(n_kb):
            v_t[n] = v_ref[n * blk:(n + 1) * blk, :].T.astype(BF16)

    qf = q_ref[...]
    qs = (qf * ATTN_SCALE).astype(BF16)
    q_all = _dot_nt(selq_ref[...], qs).astype(BF16)
    q_t = [q_all[h * LANES:(h + 1) * LANES, :] for h in heads]
    sub = lax.broadcasted_iota(jnp.int32, (SUBLANES, blk), 0)
    past = sub < own
    causal = (lax.broadcasted_iota(jnp.int32, (blk, blk), 0) <= lax.broadcasted_iota(jnp.int32, (blk, blk), 1))

    sel = []
    gates = _dot3_nt(kmean[...], qf)
    for h in heads:
        gate = gates[h * SUBLANES:(h + 1) * SUBLANES, :]
        selm = jnp.zeros((SUBLANES, blk), F32)
        for n in range(n_kb):
            gn = gate[n:n + 1, :]
            beats = past & ((gate > gn) | ((gate == gn) & (sub < n)))
            rank = jnp.sum(beats.astype(F32), axis=0, keepdims=True)
            selm = jnp.where(sub == n, (rank < MOBA_TOPK).astype(F32), selm)
        sel.append(jnp.where(past, selm, 0.0))

    def logits(j):
        k0 = pl.multiple_of(j * blk, blk)
        return [_dot(ka[pl.ds(k0, blk), h * LANES:(h + 1) * LANES], q_t[h]) for h in heads]

    def values(j):
        vj = v_t[j]
        return [vj[h * HEAD_DIM:(h + 1) * HEAD_DIM, :] for h in heads]

    def put(slot, s):
        for h in heads:
            s_buf[slot * MOBA_HEADS + h] = s[h]

    def step(j, slot, carry, live, prefetch):
        if prefetch:
            put(1 - slot, logits(jnp.minimum(j + 1, last)))
        picked = [(jnp.sum(jnp.where(sub == j, sel[h], 0.0), axis=0, keepdims=True) > 0.5) & live for h in heads]
        s = [jnp.where(pk, s_buf[slot * MOBA_HEADS + h], -jnp.inf) for h, pk in zip(heads, picked)]
        return _softmax_tiles(s, carry, values(j))

    last = jnp.maximum(own - 1, 0)
    s_own = logits(own)
    put(0, logits(0))
    carry = _first_tiles([jnp.where(causal, x, -jnp.inf) for x in s_own], values(own))

    def body(jj, flat):
        carry = _unflatten(flat)
        for slot in range(2):
            carry = step(2 * jj + slot, slot, carry, True, True)
        return _flatten(carry)

    carry = _unflatten(lax.fori_loop(0, own // 2, body, _flatten(carry)))
    carry = step(last, 0, carry, own % 2 == 1, False)
    _store_heads(o_ref, carry)


def _moba_call(p, batch, seq):
    blk = MOBA_BLOCK
    n_kb = seq // blk
    assert n_kb <= SUBLANES
    selk, selq = _moba_selectors()
    kern = functools.partial(_moba_kernel, blk=blk, n_kb=n_kb)
    return pl.pallas_call(
        kern,
        grid=(batch, n_kb),
        in_specs=[
            pl.BlockSpec((blk, MOBA_W), lambda b, i: (b * n_kb + i, C_MQ // MOBA_W)),
            pl.BlockSpec((seq, MOBA_W), lambda b, i: (b, C_MK // MOBA_W)),
            pl.BlockSpec((seq, MOBA_W), lambda b, i: (b, C_MV // MOBA_W)),
            pl.BlockSpec(selk.shape, lambda b, i: (0, 0)),
            pl.BlockSpec(selq.shape, lambda b, i: (0, 0)),
        ],
        out_specs=pl.BlockSpec((blk, MOBA_W), lambda b, i: (b * n_kb + i, 0)),
        out_shape=jax.ShapeDtypeStruct((batch * seq, MOBA_W), F32),
        scratch_shapes=[
            pltpu.VMEM((seq, MOBA_HEADS * LANES), BF16),
            pltpu.VMEM((n_kb, MOBA_W, blk), BF16),
            pltpu.VMEM((MOBA_HEADS * SUBLANES, MOBA_W), F32),
            pltpu.VMEM((2 * MOBA_HEADS, blk, blk), F32),
        ],
        compiler_params=_cparams(("arbitrary", "arbitrary")),
        name="moba_attention",
    )(p, p, p, selk, selq)


DECAY_SCALE = float(np.exp(-0.5))


def _rwkv_kernel(r_ref, k_ref, v_ref, lo_ref, w0, w2, a0, a2, g2, kkw, kaw, rkw, lnw, lnb, o_ref,
                 state, a_t, r_t, k_t, b_t, k_b, b_b, v_s, w_c, y_s, *, tm, sub):
    i = pl.program_id(1)
    ch = RWKV_CHUNK
    n_ch = sub // ch

    @pl.when(i == 0)
    def _():
        state[...] = jnp.zeros(state.shape, F32)

    gi = lax.broadcasted_iota(jnp.int32, (MXU_N, MXU_N), 0) // HEAD_DIM
    gj = lax.broadcasted_iota(jnp.int32, (MXU_N, MXU_N), 1) // HEAD_DIM
    group = (gi == gj).astype(BF16)
    ti = lax.broadcasted_iota(jnp.int32, (ch, ch), 0)
    tj = lax.broadcasted_iota(jnp.int32, (ch, ch), 1)
    strict = tj < ti
    incl = tj <= ti

    def project(r0):
        lo = lo_ref[r0:r0 + sub, :]
        w_lo = lo[:, 0:DECAY_LORA]
        a_lo = lo[:, DECAY_LORA:DECAY_LORA + AAA_LORA]
        g_lo = lo[:, DECAY_LORA + AAA_LORA:LORA_W]
        kk = k_ref[r0:r0 + sub, :] * kkw[...]
        return (_dot3(jnp.tanh(w_lo), w2[...]), _dot3(a_lo, a2[...]), _dot(_sigmoid(g_lo).astype(BF16), g2[...]),
                _head_sums(kk * kk, group, exact=False))

    def prepare(r0, zw, za, gate, ss):
        span = slice(r0, r0 + sub)
        r = r_ref[span, :]
        k = k_ref[span, :]
        v = v_ref[span, :]

        lw = -DECAY_SCALE * _sigmoid(w0[...] + zw)
        eta = _sigmoid(a0[...] + za)

        kk = k * kkw[...] * jnp.minimum(lax.rsqrt(ss), 1e12)
        kp = k * (1.0 + (eta - 1.0) * kaw[...])
        bb = kk * eta

        lc = _seg_cumsum_rows(lw, ch)
        a_t[span, :] = -kk * jnp.exp(lc - lw)
        r_t[span, :] = r * jnp.exp(lc)
        einv = jnp.exp(-lc)
        k_t[span, :] = kp * einv
        b_t[span, :] = bb * einv
        v_s[span, :] = v
        for c in range(n_ch):
            rows = slice(c * ch, (c + 1) * ch)
            dst = slice(r0 + c * ch, r0 + (c + 1) * ch)
            last = lc[(c + 1) * ch - 1:(c + 1) * ch, :]
            e = jnp.exp(last - lc[rows, :])
            k_b[dst, :] = kp[rows, :] * e
            b_b[dst, :] = bb[rows, :] * e
            w_c[r0 // ch + c] = jnp.broadcast_to(jnp.exp(last), (8, RWKV_W))
        return r * kp * rkw[...], v, gate

    def state_free(r0):
        chains = [(c, h) for c in range(n_ch) for h in range(RWKV_HEADS)]

        def tile(ref):
            return [ref[r0 + c * ch:r0 + (c + 1) * ch, h * HEAD_DIM:(h + 1) * HEAD_DIM] for c, h in chains]

        return _rwkv_state_free(tile(a_t), tile(r_t), tile(v_s), tile(k_b), tile(b_b), tile(b_t), tile(k_t),
                                strict, incl)

    def finish(r0, bonus_arg, v, gate):
        span = slice(r0, r0 + sub)
        y = y_s[span, :]
        inv_d = 1.0 / HEAD_DIM
        mean = _head_sums(y, group, exact=False) * inv_d
        d = y - mean
        var = _head_sums(d * d, group, exact=False) * inv_d
        yn = d * lax.rsqrt(var + GN_EPS) * lnw[...] + lnb[...]
        bonus = _head_sums(bonus_arg, group, exact=True) * v
        o_ref[span, :] = (yn + bonus) * gate

    starts = list(range(0, tm, sub))
    s_cur = [state[h] for h in range(RWKV_HEADS)]
    prepared = [prepare(r0, *project(r0)) for r0 in starts]
    pending = []
    for r0 in starts:
        stages = state_free(r0)
        while True:
            try:
                next(stages)
            except StopIteration as done:
                free = done.value
                break
            if pending:
                s_cur = pending.pop(0)(s_cur)
        while pending:
            s_cur = pending.pop(0)(s_cur)
        pending = [functools.partial(_rwkv_state_step, c, r0, free, w_c=w_c, y_s=y_s) for c in range(n_ch)]
    while pending:
        s_cur = pending.pop(0)(s_cur)
    for h in range(RWKV_HEADS):
        state[h] = s_cur[h]
    for r0, vals in zip(starts, prepared):
        finish(r0, *vals)


def _rwkv_state_free(at, rt, vv, kbar, bbar, btl, ktl, strict, incl):
    ch = RWKV_CHUNK
    ids = range(len(at))
    pad = jnp.zeros((LANES - ch, HEAD_DIM), BF16)
    m4 = [_dot_nt(jnp.concatenate([at[i], rt[i]], axis=0).astype(BF16),
                  jnp.concatenate([btl[i].astype(BF16), pad, ktl[i].astype(BF16), pad], axis=0)) for i in ids]
    yield
    a_ab = [jnp.where(strict, m4[i][0:ch, 0:ch], 0.0) for i in ids]
    a_ak = [jnp.where(strict, m4[i][0:ch, LANES:LANES + ch], 0.0) for i in ids]
    a_rb = [jnp.where(incl, m4[i][ch:2 * ch, 0:ch], 0.0) for i in ids]
    a_rk = [jnp.where(incl, m4[i][ch:2 * ch, LANES:LANES + ch], 0.0) for i in ids]
    avk = [_mm(jnp.concatenate([a_ak[i], a_rk[i]], axis=0), vv[i]) for i in ids]
    yield
    pw = a_ab
    tx = [jnp.concatenate([at[i], avk[i][0:ch, :]], axis=1) for i in ids]
    span = 1
    while 2 * span < ch:
        x = [_mm(pw[i], jnp.concatenate([tx[i], pw[i]], axis=1)) for i in ids]
        pw = [x[i][:, 2 * ch:3 * ch] for i in ids]
        tx = [tx[i] + x[i][:, 0:2 * ch] for i in ids]
        span *= 2
        yield
    tx = [tx[i] + _mm(pw[i], tx[i]) for i in ids]
    yield
    ry = [_mm(a_rb[i], tx[i]) for i in ids]
    rhat = [rt[i] + ry[i][:, 0:ch] for i in ids]
    yhat = [avk[i][ch:2 * ch, :] + ry[i][:, ch:2 * ch] for i in ids]
    yield
    z = [_mm(tx[i][:, 0:ch].T, bbar[i]) for i in ids]
    yield
    kv = [_mm(jnp.concatenate([vv[i], tx[i][:, ch:2 * ch]], axis=0).T,
              jnp.concatenate([kbar[i], bbar[i]], axis=0)) for i in ids]
    return rhat, yhat, z, kv


def _rwkv_state_step(c, r0, free, s_cur, w_c, y_s):
    ch = RWKV_CHUNK
    rhat, yhat, z, kv = free
    base = c * RWKV_HEADS
    ys = [_mm_nt(rhat[base + h], s_cur[h]) for h in range(RWKV_HEADS)]
    sz = [_mm(s_cur[h], z[base + h]) for h in range(RWKV_HEADS)]
    out = []
    for h in range(RWKV_HEADS):
        hs = slice(h * HEAD_DIM, (h + 1) * HEAD_DIM)
        y_s[r0 + c * ch:r0 + (c + 1) * ch, hs] = ys[h] + yhat[base + h]
        out.append(s_cur[h] * w_c[r0 // ch + c][0:1, hs] + sz[h] + kv[base + h])
    return out


def _rwkv_call(p, prm, batch, seq):
    tm = 512
    sub = 256
    nt = seq // tm
    kern = functools.partial(_rwkv_kernel, tm=tm, sub=sub)

    def rows(width, cstart):
        return pl.BlockSpec((tm, width), lambda b, i: (b * nt + i, cstart // width))

    def full(a):
        return pl.BlockSpec(a.shape, lambda b, i: (0,) * a.ndim)

    params = [prm[n] for n in ("w0", "w2", "a0", "a2", "g2", "k_k", "k_a", "r_k", "ln_w", "ln_b")]
    big = pltpu.VMEM((tm, RWKV_W), F32)
    return pl.pallas_call(
        kern,
        grid=(batch, nt),
        in_specs=[rows(RWKV_W, C_RR), rows(RWKV_W, C_RK), rows(RWKV_W, C_RV), rows(LORA_W, C_LORA)]
        + [full(a) for a in params],
        out_specs=pl.BlockSpec((tm, RWKV_W), lambda b, i: (b * nt + i, 0)),
        out_shape=jax.ShapeDtypeStruct((batch * seq, RWKV_W), F32),
        scratch_shapes=[
            pltpu.VMEM((RWKV_HEADS, HEAD_DIM, HEAD_DIM), F32),
            big, big, big, big, big, big, big,
            pltpu.VMEM((tm // RWKV_CHUNK, 8, RWKV_W), F32),
            big,
        ],
        compiler_params=_cparams(("arbitrary", "arbitrary")),
        name="rwkv7_mix",
    )(p, p, p, p, *params)


FFN_HALO = 16


MXU_K = 256


def _ffn_chunks(d_ff):
    cut = (d_ff // MXU_K + 1) // 2 * MXU_K
    return ((0, cut), (cut, d_ff))


def _ffn_kernel(yf_ref, yfp_ref, yr_ref, yrp_ref, ym_ref, ymp_ref, x_ref, xp_ref, mod_ref, nw_ref, wo_ref,
                wu_ref, cw_ref, cb_ref, wd_ref, nf_ref, o_ref, *, nt, d_ff, final):
    i = pl.program_id(0)
    m = mod_ref[0]

    def ext(prev_ref, cur_ref):
        return jnp.concatenate([prev_ref[...], cur_ref[...]], axis=0)

    z = _dot(ext(yfp_ref, yf_ref).astype(BF16), wo_ref[0:FOX_W, :])
    z = z + _dot(ext(yrp_ref, yr_ref).astype(BF16), wo_ref[FOX_W:FOX_W + RWKV_W, :])
    z = z + _dot(ext(ymp_ref, ym_ref).astype(BF16), wo_ref[FOX_W + RWKV_W:, :])
    x1 = ext(xp_ref, x_ref) + m[2:3, :] * z
    h = _rmsnorm(x1, nw_ref[...]) * (1.0 + m[4:5, :]) + m[3:4, :]
    keep = jnp.where(i % nt == 0, 0.0, 1.0)
    he = jnp.concatenate([h[0:FFN_HALO, :] * keep, h[FFN_HALO:, :]], axis=0).astype(BF16)

    def conv(u, lo, hi):
        cw = cw_ref[:, lo:hi]
        return (cb_ref[:, lo:hi] + cw[0:1, :] * pltpu.roll(u, 2, 0)[FFN_HALO:, :]
                + cw[1:2, :] * pltpu.roll(u, 1, 0)[FFN_HALO:, :] + cw[2:3, :] * u[FFN_HALO:, :])

    chunks = _ffn_chunks(d_ff)
    ups = [(_dot(he, wu_ref[:, lo:hi]), _dot(he, wu_ref[:, d_ff + lo:d_ff + hi])) for lo, hi in chunks]
    y = None
    for (lo, hi), (ug, uv) in zip(chunks, ups):
        g = conv(ug, lo, hi)
        act = g * _sigmoid(g) * conv(uv, d_ff + lo, d_ff + hi)
        part = _dot(act.astype(BF16), wd_ref[lo:hi, :])
        y = part if y is None else y + part
    out = x1[FFN_HALO:, :] + m[5:6, :] * y
    if final:
        out = _rmsnorm(out, nf_ref[...])
    o_ref[...] = out


def _ffn_call(y_fox, y_rwkv, y_moba, xf, mod_l, norm_w, w_out_b, w_up_b, conv_w, conv_b, w_down_b, norm_final,
              layer, seq, final):
    rows, d = xf.shape
    d_ff = w_down_b.shape[1]
    tm = 512
    nt = seq // tm
    hb = tm // FFN_HALO
    kern = functools.partial(_ffn_kernel, nt=nt, d_ff=d_ff, final=final)

    def tile_and_halo(width):
        return [pl.BlockSpec((tm, width), lambda i: (i, 0)),
                pl.BlockSpec((FFN_HALO, width), lambda i: (jnp.maximum(i * hb - 1, 0), 0))]

    def resident(a):
        return pl.BlockSpec((None,) + a.shape[1:], lambda i: (layer, 0, 0), pipeline_mode=pl.Buffered(1))

    return pl.pallas_call(
        kern,
        grid=(rows // tm,),
        in_specs=tile_and_halo(FOX_W) + tile_and_halo(RWKV_W) + tile_and_halo(MOBA_W) + tile_and_halo(d) + [
            pl.BlockSpec((1, 6, d), lambda i: (i // nt, 0, 0)),
            pl.BlockSpec((1, d), lambda i: (0, 0)),
            resident(w_out_b), resident(w_up_b),
            pl.BlockSpec((CONV_W, 2 * d_ff), lambda i: (0, 0)),
            pl.BlockSpec((1, 2 * d_ff), lambda i: (0, 0)),
            resident(w_down_b),
            pl.BlockSpec((1, d), lambda i: (0, 0)),
        ],
        out_specs=pl.BlockSpec((tm, d), lambda i: (i, 0)),
        out_shape=jax.ShapeDtypeStruct((rows, d), F32),
        compiler_params=_cparams(("arbitrary",)),
        name="out_proj_conv_ffn",
    )(y_fox, y_fox, y_rwkv, y_rwkv, y_moba, y_moba, xf, xf, mod_l, norm_w, w_out_b, w_up_b, conv_w, conv_b,
      w_down_b, norm_final)


def kernel(x, c, w_mod, b_mod, norm_mix, w_in, fox_f_bias, rwkv_mu, rwkv_w0, rwkv_w2, rwkv_a0, rwkv_a2,
           rwkv_g2, rwkv_k_k, rwkv_k_a, rwkv_r_k, rwkv_ln_w, rwkv_ln_b, w_out, norm_ffn, w_up, conv_w,
           conv_b, w_down, norm_final):
    batch, seq, d = x.shape
    n_layers = w_mod.shape[0]

    mod = _mod_call(c, w_mod, b_mod).reshape(n_layers, batch, 6, d)
    w_in_p = _win_layout_call(w_in)
    f_bias = jnp.pad(fox_f_bias, ((0, 0), (0, FF_PAD - FOX_HEADS)))
    w_out_b = w_out.astype(BF16)
    w_up_b = w_up.astype(BF16)
    w_down_b = w_down.astype(BF16)
    g2_b = rwkv_g2.astype(BF16)

    xf = x.reshape(batch * seq, d)
    for l in range(n_layers):
        row = lambda a: a[l].reshape(1, -1)
        prm = {
            "w0": row(rwkv_w0), "w2": rwkv_w2[l], "a0": row(rwkv_a0), "a2": rwkv_a2[l], "g2": g2_b[l],
            "k_k": row(rwkv_k_k), "k_a": row(rwkv_k_a), "r_k": row(rwkv_r_k), "ln_w": row(rwkv_ln_w),
            "ln_b": row(rwkv_ln_b),
        }
        p = _inproj_call(xf, mod[l], row(norm_mix), w_in_p, row(rwkv_mu), l, seq)
        y_fox = _fox_call(p, f_bias[l:l + 1], batch, seq)
        y_moba = _moba_call(p, batch, seq)
        y_rwkv = _rwkv_call(p, prm, batch, seq)
        xf = _ffn_call(y_fox, y_rwkv, y_moba, xf, mod[l], row(norm_ffn), w_out_b, w_up_b, conv_w[l],
                       conv_b[l].reshape(1, -1), w_down_b, norm_final.reshape(1, -1), l, seq,
                       final=(l == n_layers - 1))
    return xf.reshape(batch, seq, d)
```

```python
import functools

import jax
import jax.numpy as jnp
import numpy as np
from jax import lax
from jax.experimental import pallas as pl
from jax.experimental.pallas import tpu as pltpu

F32 = jnp.float32
BF16 = jnp.bfloat16

HEAD_DIM = 64
FOX_HEADS = 4
RWKV_HEADS = 8
MOBA_HEADS = 4
FOX_W = FOX_HEADS * HEAD_DIM
RWKV_W = RWKV_HEADS * HEAD_DIM
MOBA_W = MOBA_HEADS * HEAD_DIM
DECAY_LORA = 64
AAA_LORA = 64
GATE_LORA = 128
LORA_W = DECAY_LORA + AAA_LORA + GATE_LORA
MOBA_BLOCK = 256
MOBA_TOPK = 3
CONV_W = 3
NORM_EPS = 1e-6
GN_EPS = 64e-5
LOG2E = float(np.log2(np.e))
ATTN_SCALE = HEAD_DIM ** -0.5 * LOG2E

C_FQ, C_FK, C_FV = 0, 256, 512
C_MQ, C_MK, C_MV = 768, 1024, 1280
C_RR, C_RK, C_RV = 1536, 2048, 2560
C_LORA = 3072
C_FF = 3328
FF_PAD = 128
NP_COLS = C_FF + FF_PAD

RWKV_CHUNK = 64
LANES = 128
SUBLANES = 8
MXU_N = 256
VMEM_LIMIT = 56 * 1024 * 1024


def _cparams(sem):
    return pltpu.CompilerParams(dimension_semantics=sem, vmem_limit_bytes=VMEM_LIMIT)


def _dot(a, b):
    return jnp.dot(a, b, preferred_element_type=F32)


def _dot_nt(a, b):
    return lax.dot_general(a, b, (((1,), (1,)), ((), ())), preferred_element_type=F32)


def _mm(a, b):
    return jnp.dot(a.astype(BF16), b.astype(BF16), preferred_element_type=F32)


def _mm_nt(a, b):
    return _dot_nt(a.astype(BF16), b.astype(BF16))


def _split2(x):
    hi = x.astype(BF16)
    return hi, (x - hi.astype(F32)).astype(BF16)


def _dot_split_lhs(x, w_bf16):
    m = x.shape[0]
    r = _dot(jnp.concatenate(_split2(x), axis=0), w_bf16)
    return r[0:m, :] + r[m:2 * m, :]


def _head_sums(x, ones_blk, exact):
    parts = []
    for c0 in range(0, x.shape[1], MXU_N):
        xc = x[:, c0:c0 + MXU_N]
        parts.append(_dot_split_lhs(xc, ones_blk) if exact else _dot(xc.astype(BF16), ones_blk))
    return jnp.concatenate(parts, axis=1)


def _dot3(a, b):
    m = a.shape[0]
    a_hi, a_lo = _split2(a)
    b_hi, b_lo = _split2(b)
    r = _dot(jnp.concatenate([a_hi, a_lo], axis=0), b_hi)
    return r[0:m, :] + r[m:2 * m, :] + _dot(a_hi, b_lo)


def _dot3_nt(a, b):
    m = a.shape[0]
    a_hi, a_lo = _split2(a)
    b_hi, b_lo = _split2(b)
    r = _dot_nt(jnp.concatenate([a_hi, a_lo], axis=0), b_hi)
    return r[0:m, :] + r[m:2 * m, :] + _dot_nt(a_hi, b_lo)


def _sigmoid(x):
    return 0.5 * jnp.tanh(0.5 * x) + 0.5


def _log_sigmoid(x):
    return jnp.minimum(x, 0.0) - jnp.log1p(jnp.exp(-jnp.abs(x)))


def _rmsnorm(x, w):
    ms = jnp.mean(x * x, axis=-1, keepdims=True)
    return x * lax.rsqrt(ms + NORM_EPS) * w


def _seg_cumsum_rows(x, seg):
    row = lax.broadcasted_iota(jnp.int32, x.shape, 0) & (seg - 1)
    s = 1
    while s < seg:
        x = x + jnp.where(row >= s, pltpu.roll(x, s, 0), 0.0)
        s *= 2
    return x


def _mod_kernel(c_ref, w_ref, b_ref, o_ref):
    c = c_ref[...]
    o_ref[0] = _dot3(c * _sigmoid(c), w_ref[0]) + b_ref[0]


def _mod_call(c, w_mod, b_mod):
    n_layers, d, n = w_mod.shape
    b = c.shape[0]
    tn = 1536
    return pl.pallas_call(
        _mod_kernel,
        grid=(n_layers, n // tn),
        in_specs=[
            pl.BlockSpec((b, d), lambda l, j: (0, 0)),
            pl.BlockSpec((1, d, tn), lambda l, j: (l, 0, j)),
            pl.BlockSpec((1, 1, tn), lambda l, j: (l, 0, j)),
        ],
        out_specs=pl.BlockSpec((1, b, tn), lambda l, j: (l, 0, j)),
        out_shape=jax.ShapeDtypeStruct((n_layers, b, n), F32),
        compiler_params=_cparams(("arbitrary", "arbitrary")),
        name="adaln_mod",
    )(c, w_mod, b_mod.reshape(n_layers, 1, n))


def _win_layout_kernel(w_ref, o_ref):
    fox_cols = 3 * FOX_W + FOX_HEADS
    rwkv_cols = 3 * RWKV_W + LORA_W
    w = w_ref[...]
    o_ref[:, C_FQ:C_MQ] = w[:, 0:3 * FOX_W].astype(BF16)
    o_ref[:, C_MQ:C_RR] = w[:, fox_cols + rwkv_cols:fox_cols + rwkv_cols + 3 * MOBA_W].astype(BF16)
    o_ref[:, C_RR:C_FF] = w[:, fox_cols:fox_cols + rwkv_cols].astype(BF16)
    lane = lax.broadcasted_iota(jnp.int32, (w.shape[0], FF_PAD), 1)
    o_ref[:, C_FF:NP_COLS] = jnp.where(lane < FOX_HEADS, w[:, 3 * FOX_W:3 * FOX_W + FF_PAD], 0.0).astype(BF16)


def _win_layout_call(w_in):
    n_layers, d, n = w_in.shape
    tr = 256
    return pl.pallas_call(
        _win_layout_kernel,
        grid=(n_layers, d // tr),
        in_specs=[pl.BlockSpec((None, tr, n), lambda l, i: (l, i, 0))],
        out_specs=pl.BlockSpec((None, tr, NP_COLS), lambda l, i: (l, i, 0)),
        out_shape=jax.ShapeDtypeStruct((n_layers, d, NP_COLS), BF16),
        compiler_params=_cparams(("arbitrary", "arbitrary")),
        name="w_in_layout",
    )(w_in)


def _inproj_kernel(x_ref, mod_ref, nw_ref, w_ref, mu_ref, o_ref, prev, *, tm, nt):
    i = pl.program_id(0)
    m = mod_ref[0]
    h = _rmsnorm(x_ref[...], nw_ref[...]) * (1.0 + m[1:2, :]) + m[0:1, :]
    p = _dot(h.astype(BF16), w_ref[...])
    o_ref[:, 0:C_RR] = p[:, 0:C_RR]
    o_ref[:, C_FF:NP_COLS] = p[:, C_FF:NP_COLS]
    feat = p[:, C_RR:C_FF]

    @pl.when(i % nt == 0)
    def _():
        prev[...] = jnp.zeros(prev.shape, F32)

    rolled = pltpu.roll(feat, 1, 0)
    first = lax.broadcasted_iota(jnp.int32, (SUBLANES, C_FF - C_RR), 0) == 0
    top = jnp.where(first, prev[...], rolled[0:SUBLANES, :])
    shifted = jnp.concatenate([top, rolled[SUBLANES:, :]], axis=0)
    prev[...] = feat[tm - 1:tm, :]
    o_ref[:, C_RR:C_FF] = feat + (shifted - feat) * mu_ref[...]


def _inproj_call(xf, mod_l, norm_w, w_in_p, mu_row, layer, seq):
    rows, d = xf.shape
    tm = 512
    nt = seq // tm
    kern = functools.partial(_inproj_kernel, tm=tm, nt=nt)
    return pl.pallas_call(
        kern,
        grid=(rows // tm,),
        in_specs=[
            pl.BlockSpec((tm, d), lambda i: (i, 0)),
            pl.BlockSpec((1, 6, d), lambda i: (i // nt, 0, 0)),
            pl.BlockSpec((1, d), lambda i: (0, 0)),
            pl.BlockSpec((None, d, NP_COLS), lambda i: (layer, 0, 0)),
            pl.BlockSpec((1, C_FF - C_RR), lambda i: (0, 0)),
        ],
        out_specs=pl.BlockSpec((tm, NP_COLS), lambda i: (i, 0)),
        out_shape=jax.ShapeDtypeStruct((rows, NP_COLS), F32),
        scratch_shapes=[pltpu.VMEM((1, C_FF - C_RR), F32)],
        compiler_params=_cparams(("arbitrary",)),
        name="in_proj",
    )(xf, mod_l, norm_w, w_in_p, mu_row)


def _softmax_tiles(s, carry, v_t):
    m_new = [jnp.maximum(c[0], jnp.max(x, axis=0, keepdims=True)) for x, c in zip(s, carry)]
    p = [jnp.exp2(x - m) for x, m in zip(s, m_new)]
    pv = [_dot(v, x.astype(BF16)) for v, x in zip(v_t, p)]
    out = []
    for (m, l, acc), mn, x, y in zip(carry, m_new, p, pv):
        alpha = jnp.exp2(m - mn)
        out.append((mn, alpha * l + jnp.sum(x, axis=0, keepdims=True), alpha * acc + y))
    return out


def _first_tiles(s, v_t):
    m = [jnp.max(x, axis=0, keepdims=True) for x in s]
    p = [jnp.exp2(x - mm) for x, mm in zip(s, m)]
    pv = [_dot(v, x.astype(BF16)) for v, x in zip(v_t, p)]
    return [(mm, jnp.sum(x, axis=0, keepdims=True), y) for mm, x, y in zip(m, p, pv)]


def _flatten(carry):
    return tuple(a for c in carry for a in c)


def _unflatten(flat):
    return [tuple(flat[3 * h:3 * h + 3]) for h in range(len(flat) // 3)]


def _store_heads(o_ref, carry):
    o_ref[...] = jnp.concatenate([acc / l for _, l, acc in carry], axis=0).T.astype(o_ref.dtype)


F_COL = HEAD_DIM
N_SPLIT = 3
FOX_XW = FOX_W + FF_PAD


def _fox_selectors():
    selk = np.zeros((FOX_XW, FOX_HEADS * LANES), np.float32)
    for h in range(FOX_HEADS):
        for c in range(HEAD_DIM):
            selk[h * HEAD_DIM + c, h * LANES + c] = 1.0
    selq = selk.T.copy()
    for h in range(FOX_HEADS):
        for s in range(N_SPLIT):
            selk[FOX_W + FOX_HEADS * s + h, h * LANES + F_COL + s] = 1.0
            selq[h * LANES + F_COL + N_SPLIT + s, FOX_W + FOX_HEADS * s + h] = 1.0
    return jnp.asarray(selk, BF16), jnp.asarray(selq, BF16)


def _fox_kernel(q_ref, k_ref, v_ref, ff_ref, fb_ref, selk_ref, selq_ref, o_ref, ka, v_t, fp, s_buf, *, tq, tk,
                n_kv):
    qi = pl.program_id(1)
    heads = range(FOX_HEADS)

    @pl.when(qi == 0)
    def _():
        lane = lax.broadcasted_iota(jnp.int32, (1, LANES), 1)
        f = _seg_cumsum_rows(_log_sigmoid(ff_ref[...] + fb_ref[...]), n_kv * tk) * LOG2E
        f = jnp.where(lane < FOX_HEADS, f, 0.0)
        hi = f.astype(BF16).astype(F32)
        mid = (f - hi).astype(BF16).astype(F32)
        lo = (f - hi - mid).astype(BF16).astype(F32)
        fp[...] = (hi + pltpu.roll(mid, FOX_HEADS, 1) + pltpu.roll(lo, 2 * FOX_HEADS, 1)).astype(BF16)
        x = jnp.concatenate([k_ref[...].astype(BF16), fp[...]], axis=1)
        lane4 = lax.broadcasted_iota(jnp.int32, (1, FOX_HEADS * LANES), 1) % LANES
        ones = jnp.where((lane4 >= F_COL + N_SPLIT) & (lane4 < F_COL + 2 * N_SPLIT), 1.0, 0.0)
        ka[...] = (_dot(x, selk_ref[...]) + ones).astype(BF16)
        for j in range(n_kv):
            v_t[j] = v_ref[j * tk:(j + 1) * tk, :].T.astype(BF16)

    q0 = pl.multiple_of(qi * tq, tq)
    xq = jnp.concatenate([(q_ref[...] * ATTN_SCALE).astype(BF16), fp[pl.ds(q0, tq), :]], axis=1)
    sub = lax.broadcasted_iota(jnp.int32, (FOX_HEADS * LANES, tq), 0) % LANES
    neg = jnp.where((sub >= F_COL) & (sub < F_COL + N_SPLIT), -1.0, 0.0)
    q_all = (_dot_nt(selq_ref[...], xq) + neg).astype(BF16)
    q_t = [q_all[h * LANES:(h + 1) * LANES, :] for h in heads]
    lag = (lax.broadcasted_iota(jnp.int32, (tk, tq), 0) - lax.broadcasted_iota(jnp.int32, (tk, tq), 1))
    n_diag = tq // tk

    def logits(j):
        k0 = pl.multiple_of(j * tk, tk)
        return [_dot(ka[pl.ds(k0, tk), h * LANES:(h + 1) * LANES], q_t[h]) for h in heads]

    def values(j):
        vj = v_t[j]
        return [vj[h * HEAD_DIM:(h + 1) * HEAD_DIM, :] for h in heads]

    def put(slot, s):
        for h in heads:
            s_buf[slot * FOX_HEADS + h] = s[h]

    j0 = qi * n_diag
    last = jnp.maximum(j0 - 1, 0)
    s_diag = [logits(j0 + dd) for dd in range(n_diag)]
    put(0, logits(0))
    carry = _first_tiles([jnp.where(lag <= 0, x, -jnp.inf) for x in s_diag[0]], values(j0))
    for dd in range(1, n_diag):
        carry = _softmax_tiles([jnp.where(lag <= -dd * tk, x, -jnp.inf) for x in s_diag[dd]], carry,
                               values(j0 + dd))

    def step(j, slot, carry):
        put(1 - slot, logits(jnp.minimum(j + 1, last)))
        s_cur = [s_buf[slot * FOX_HEADS + h] for h in heads]
        return _softmax_tiles(s_cur, carry, values(j))

    def body(jj, flat):
        carry = _unflatten(flat)
        for slot in range(2):
            carry = step(2 * jj + slot, slot, carry)
        return _flatten(carry)

    assert n_diag % 2 == 0
    _store_heads(o_ref, _unflatten(lax.fori_loop(0, j0 // 2, body, _flatten(carry))))


def _fox_call(p, f_bias_row, batch, seq):
    tq = 512
    tk = 256
    nq = seq // tq
    selk, selq = _fox_selectors()
    kern = functools.partial(_fox_kernel, tq=tq, tk=tk, n_kv=seq // tk)
    return pl.pallas_call(
        kern,
        grid=(batch, nq),
        in_specs=[
            pl.BlockSpec((tq, FOX_W), lambda b, i: (b * nq + i, C_FQ // FOX_W)),
            pl.BlockSpec((seq, FOX_W), lambda b, i: (b, C_FK // FOX_W)),
            pl.BlockSpec((seq, FOX_W), lambda b, i: (b, C_FV // FOX_W)),
            pl.BlockSpec((seq, FF_PAD), lambda b, i: (b, C_FF // FF_PAD)),
            pl.BlockSpec((1, FF_PAD), lambda b, i: (0, 0)),
            pl.BlockSpec(selk.shape, lambda b, i: (0, 0)),
            pl.BlockSpec(selq.shape, lambda b, i: (0, 0)),
        ],
        out_specs=pl.BlockSpec((tq, FOX_W), lambda b, i: (b * nq + i, 0)),
        out_shape=jax.ShapeDtypeStruct((batch * seq, FOX_W), BF16),
        scratch_shapes=[
            pltpu.VMEM((seq, FOX_HEADS * LANES), BF16),
            pltpu.VMEM((seq // tk, FOX_W, tk), BF16),
            pltpu.VMEM((seq, FF_PAD), BF16),
            pltpu.VMEM((2 * FOX_HEADS, tk, tq), F32),
        ],
        compiler_params=_cparams(("arbitrary", "arbitrary")),
        name="fox_attention",
    )(p, p, p, p, f_bias_row, selk, selq)


def _moba_selectors():
    selk = np.zeros((MOBA_W, MOBA_HEADS * LANES), np.float32)
    for h in range(MOBA_HEADS):
        for c in range(HEAD_DIM):
            selk[h * HEAD_DIM + c, h * LANES + c] = 1.0
    return jnp.asarray(selk, BF16), jnp.asarray(selk.T, BF16)


def _moba_kernel(q_ref, k_ref, v_ref, selk_ref, selq_ref, o_ref, ka, v_t, kmean, s_buf, *, blk, n_kb):
    own = pl.program_id(1)
    heads = range(MOBA_HEADS)

    @pl.when(own == 0)
    def _():
        kf = k_ref[...]
        ka[...] = _dot(kf.astype(BF16), selk_ref[...]).astype(BF16)
        means = [jnp.mean(kf[n * blk:(n + 1) * blk, :], axis=0, keepdims=True) for n in range(n_kb)]
        if n_kb < SUBLANES:
            means.append(jnp.zeros((SUBLANES - n_kb, MOBA_W), F32))
        means = jnp.concatenate(means, axis=0)
        head_of_lane = lax.broadcasted_iota(jnp.int32, (SUBLANES, MOBA_W), 1) // HEAD_DIM
        for h in heads:
            kmean[h * SUBLANES:(h + 1) * SUBLANES, :] = jnp.where(head_of_lane == h, means, 0.0)
        for n in range(n_kb):
            v_t[n] = v_ref[n * blk:(n + 1) * blk, :].T.astype(BF16)

    qf = q_ref[...]
    qs = (qf * ATTN_SCALE).astype(BF16)
    q_all = _dot_nt(selq_ref[...], qs).astype(BF16)
    q_t = [q_all[h * LANES:(h + 1) * LANES, :] for h in heads]
    sub = lax.broadcasted_iota(jnp.int32, (SUBLANES, blk), 0)
    past = sub < own
    causal = (lax.broadcasted_iota(jnp.int32, (blk, blk), 0) <= lax.broadcasted_iota(jnp.int32, (blk, blk), 1))

    sel = []
    gates = _dot3_nt(kmean[...], qf)
    for h in heads:
        gate = gates[h * SUBLANES:(h + 1) * SUBLANES, :]
        selm = jnp.zeros((SUBLANES, blk), F32)
        for n in range(n_kb):
            gn = gate[n:n + 1, :]
            beats = past & ((gate > gn) | ((gate == gn) & (sub < n)))
            rank = jnp.sum(beats.astype(F32), axis=0, keepdims=True)
            selm = jnp.where(sub == n, (rank < MOBA_TOPK).astype(F32), selm)
        sel.append(jnp.where(past, selm, 0.0))

    def logits(j):
        k0 = pl.multiple_of(j * blk, blk)
        return [_dot(ka[pl.ds(k0, blk), h * LANES:(h + 1) * LANES], q_t[h]) for h in heads]

    def values(j):
        vj = v_t[j]
        return [vj[h * HEAD_DIM:(h + 1) * HEAD_DIM, :] for h in heads]

    def put(slot, s):
        for h in heads:
            s_buf[slot * MOBA_HEADS + h] = s[h]

    def step(j, slot, carry, live, prefetch):
        if prefetch:
            put(1 - slot, logits(jnp.minimum(j + 1, last)))
        picked = [(jnp.sum(jnp.where(sub == j, sel[h], 0.0), axis=0, keepdims=True) > 0.5) & live for h in heads]
        s = [jnp.where(pk, s_buf[slot * MOBA_HEADS + h], -jnp.inf) for h, pk in zip(heads, picked)]
        return _softmax_tiles(s, carry, values(j))

    last = jnp.maximum(own - 1, 0)
    s_own = logits(own)
    put(0, logits(0))
    carry = _first_tiles([jnp.where(causal, x, -jnp.inf) for x in s_own], values(own))

    def body(jj, flat):
        carry = _unflatten(flat)
        for slot in range(2):
            carry = step(2 * jj + slot, slot, carry, True, True)
        return _flatten(carry)

    carry = _unflatten(lax.fori_loop(0, own // 2, body, _flatten(carry)))
    carry = step(last, 0, carry, own % 2 == 1, False)
    _store_heads(o_ref, carry)


def _moba_call(p, batch, seq):
    blk = MOBA_BLOCK
    n_kb = seq // blk
    assert n_kb <= SUBLANES
    selk, selq = _moba_selectors()
    kern = functools.partial(_moba_kernel, blk=blk, n_kb=n_kb)
    return pl.pallas_call(
        kern,
        grid=(batch, n_kb),
        in_specs=[
            pl.BlockSpec((blk, MOBA_W), lambda b, i: (b * n_kb + i, C_MQ // MOBA_W)),
            pl.BlockSpec((seq, MOBA_W), lambda b, i: (b, C_MK // MOBA_W)),
            pl.BlockSpec((seq, MOBA_W), lambda b, i: (b, C_MV // MOBA_W)),
            pl.BlockSpec(selk.shape, lambda b, i: (0, 0)),
            pl.BlockSpec(selq.shape, lambda b, i: (0, 0)),
        ],
        out_specs=pl.BlockSpec((blk, MOBA_W), lambda b, i: (b * n_kb + i, 0)),
        out_shape=jax.ShapeDtypeStruct((batch * seq, MOBA_W), BF16),
        scratch_shapes=[
            pltpu.VMEM((seq, MOBA_HEADS * LANES), BF16),
            pltpu.VMEM((n_kb, MOBA_W, blk), BF16),
            pltpu.VMEM((MOBA_HEADS * SUBLANES, MOBA_W), F32),
            pltpu.VMEM((2 * MOBA_HEADS, blk, blk), F32),
        ],
        compiler_params=_cparams(("arbitrary", "arbitrary")),
        name="moba_attention",
    )(p, p, p, selk, selq)


DECAY_SCALE = float(np.exp(-0.5))


def _rwkv_kernel(r_ref, k_ref, v_ref, lo_ref, w0, w2, a0, a2, g2, kkw, kaw, rkw, lnw, lnb, o_ref,
                 state, a_t, r_t, k_t, b_t, k_b, b_b, v_s, w_c, y_s, *, tm, sub):
    i = pl.program_id(1)
    ch = RWKV_CHUNK
    n_ch = sub // ch

    @pl.when(i == 0)
    def _():
        state[...] = jnp.zeros(state.shape, F32)

    gi = lax.broadcasted_iota(jnp.int32, (MXU_N, MXU_N), 0) // HEAD_DIM
    gj = lax.broadcasted_iota(jnp.int32, (MXU_N, MXU_N), 1) // HEAD_DIM
    group = (gi == gj).astype(BF16)
    ti = lax.broadcasted_iota(jnp.int32, (ch, ch), 0)
    tj = lax.broadcasted_iota(jnp.int32, (ch, ch), 1)
    strict = tj < ti
    incl = tj <= ti

    def project(r0):
        lo = lo_ref[r0:r0 + sub, :]
        w_lo = lo[:, 0:DECAY_LORA]
        a_lo = lo[:, DECAY_LORA:DECAY_LORA + AAA_LORA]
        g_lo = lo[:, DECAY_LORA + AAA_LORA:LORA_W]
        kk = k_ref[r0:r0 + sub, :] * kkw[...]
        return (_dot3(jnp.tanh(w_lo), w2[...]), _dot3(a_lo, a2[...]), _dot(_sigmoid(g_lo).astype(BF16), g2[...]),
                _head_sums(kk * kk, group, exact=False))

    def prepare(r0, zw, za, gate, ss):
        span = slice(r0, r0 + sub)
        r = r_ref[span, :]
        k = k_ref[span, :]
        v = v_ref[span, :]

        lw = -DECAY_SCALE * _sigmoid(w0[...] + zw)
        eta = _sigmoid(a0[...] + za)

        kk = k * kkw[...] * jnp.minimum(lax.rsqrt(ss), 1e12)
        kp = k * (1.0 + (eta - 1.0) * kaw[...])
        bb = kk * eta

        lc = _seg_cumsum_rows(lw, ch)
        a_t[span, :] = -kk * jnp.exp(lc - lw)
        r_t[span, :] = r * jnp.exp(lc)
        einv = jnp.exp(-lc)
        k_t[span, :] = kp * einv
        b_t[span, :] = bb * einv
        v_s[span, :] = v
        for c in range(n_ch):
            rows = slice(c * ch, (c + 1) * ch)
            dst = slice(r0 + c * ch, r0 + (c + 1) * ch)
            last = lc[(c + 1) * ch - 1:(c + 1) * ch, :]
            e = jnp.exp(last - lc[rows, :])
            k_b[dst, :] = kp[rows, :] * e
            b_b[dst, :] = bb[rows, :] * e
            w_c[r0 // ch + c] = jnp.broadcast_to(jnp.exp(last), (8, RWKV_W))
        return r * kp * rkw[...], v, gate

    def state_free(r0):
        chains = [(c, h) for c in range(n_ch) for h in range(RWKV_HEADS)]

        def tile(ref):
            return [ref[r0 + c * ch:r0 + (c + 1) * ch, h * HEAD_DIM:(h + 1) * HEAD_DIM] for c, h in chains]

        return _rwkv_state_free(tile(a_t), tile(r_t), tile(v_s), tile(k_b), tile(b_b), tile(b_t), tile(k_t),
                                strict, incl)

    def finish(r0, bonus_arg, v, gate):
        span = slice(r0, r0 + sub)
        y = y_s[span, :]
        inv_d = 1.0 / HEAD_DIM
        mean = _head_sums(y, group, exact=False) * inv_d
        d = y - mean
        var = _head_sums(d * d, group, exact=False) * inv_d
        yn = d * lax.rsqrt(var + GN_EPS) * lnw[...] + lnb[...]
        bonus = _head_sums(bonus_arg, group, exact=True) * v
        o_ref[span, :] = ((yn + bonus) * gate).astype(o_ref.dtype)

    starts = list(range(0, tm, sub))
    s_cur = [state[h] for h in range(RWKV_HEADS)]
    prepared = [prepare(r0, *project(r0)) for r0 in starts]
    pending = []
    for r0 in starts:
        stages = state_free(r0)
        while True:
            try:
                next(stages)
            except StopIteration as done:
                free = done.value
                break
            if pending:
                s_cur = pending.pop(0)(s_cur)
        while pending:
            s_cur = pending.pop(0)(s_cur)
        pending = [functools.partial(_rwkv_state_step, c, r0, free, w_c=w_c, y_s=y_s) for c in range(n_ch)]
    while pending:
        s_cur = pending.pop(0)(s_cur)
    for h in range(RWKV_HEADS):
        state[h] = s_cur[h]
    for r0, vals in zip(starts, prepared):
        finish(r0, *vals)


def _rwkv_state_free(at, rt, vv, kbar, bbar, btl, ktl, strict, incl):
    ch = RWKV_CHUNK
    ids = range(len(at))
    pad = jnp.zeros((LANES - ch, HEAD_DIM), BF16)
    m4 = [_dot_nt(jnp.concatenate([at[i], rt[i]], axis=0).astype(BF16),
                  jnp.concatenate([btl[i].astype(BF16), pad, ktl[i].astype(BF16), pad], axis=0)) for i in ids]
    yield
    a_ab = [jnp.where(strict, m4[i][0:ch, 0:ch], 0.0) for i in ids]
    a_ak = [jnp.where(strict, m4[i][0:ch, LANES:LANES + ch], 0.0) for i in ids]
    a_rb = [jnp.where(incl, m4[i][ch:2 * ch, 0:ch], 0.0) for i in ids]
    a_rk = [jnp.where(incl, m4[i][ch:2 * ch, LANES:LANES + ch], 0.0) for i in ids]
    avk = [_mm(jnp.concatenate([a_ak[i], a_rk[i]], axis=0), vv[i]) for i in ids]
    yield
    pw = a_ab
    tx = [jnp.concatenate([at[i], avk[i][0:ch, :]], axis=1) for i in ids]
    span = 1
    while 2 * span < ch:
        x = [_mm(pw[i], jnp.concatenate([tx[i], pw[i]], axis=1)) for i in ids]
        pw = [x[i][:, 2 * ch:3 * ch] for i in ids]
        tx = [tx[i] + x[i][:, 0:2 * ch] for i in ids]
        span *= 2
        yield
    tx = [tx[i] + _mm(pw[i], tx[i]) for i in ids]
    yield
    ry = [_mm(a_rb[i], tx[i]) for i in ids]
    rhat = [rt[i] + ry[i][:, 0:ch] for i in ids]
    yhat = [avk[i][ch:2 * ch, :] + ry[i][:, ch:2 * ch] for i in ids]
    yield
    z = [_mm(tx[i][:, 0:ch].T, bbar[i]) for i in ids]
    yield
    kv = [_mm(jnp.concatenate([vv[i], tx[i][:, ch:2 * ch]], axis=0).T,
              jnp.concatenate([kbar[i], bbar[i]], axis=0)) for i in ids]
    return rhat, yhat, z, kv


def _rwkv_state_step(c, r0, free, s_cur, w_c, y_s):
    ch = RWKV_CHUNK
    rhat, yhat, z, kv = free
    base = c * RWKV_HEADS
    ys = [_mm_nt(rhat[base + h], s_cur[h]) for h in range(RWKV_HEADS)]
    sz = [_mm(s_cur[h], z[base + h]) for h in range(RWKV_HEADS)]
    out = []
    for h in range(RWKV_HEADS):
        hs = slice(h * HEAD_DIM, (h + 1) * HEAD_DIM)
        y_s[r0 + c * ch:r0 + (c + 1) * ch, hs] = ys[h] + yhat[base + h]
        out.append(s_cur[h] * w_c[r0 // ch + c][0:1, hs] + sz[h] + kv[base + h])
    return out


def _rwkv_call(p, prm, batch, seq):
    tm = 512
    sub = 256
    nt = seq // tm
    kern = functools.partial(_rwkv_kernel, tm=tm, sub=sub)

    def rows(width, cstart):
        return pl.BlockSpec((tm, width), lambda b, i: (b * nt + i, cstart // width))

    def full(a):
        return pl.BlockSpec(a.shape, lambda b, i: (0,) * a.ndim)

    params = [prm[n] for n in ("w0", "w2", "a0", "a2", "g2", "k_k", "k_a", "r_k", "ln_w", "ln_b")]
    big = pltpu.VMEM((tm, RWKV_W), F32)
    return pl.pallas_call(
        kern,
        grid=(batch, nt),
        in_specs=[rows(RWKV_W, C_RR), rows(RWKV_W, C_RK), rows(RWKV_W, C_RV), rows(LORA_W, C_LORA)]
        + [full(a) for a in params],
        out_specs=pl.BlockSpec((tm, RWKV_W), lambda b, i: (b * nt + i, 0)),
        out_shape=jax.ShapeDtypeStruct((batch * seq, RWKV_W), BF16),
        scratch_shapes=[
            pltpu.VMEM((RWKV_HEADS, HEAD_DIM, HEAD_DIM), F32),
            big, big, big, big, big, big, big,
            pltpu.VMEM((tm // RWKV_CHUNK, 8, RWKV_W), F32),
            big,
        ],
        compiler_params=_cparams(("arbitrary", "arbitrary")),
        name="rwkv7_mix",
    )(p, p, p, p, *params)


FFN_HALO = 16


MXU_K = 256


def _ffn_chunks(d_ff):
    cut = (d_ff // MXU_K + 1) // 2 * MXU_K
    return ((0, cut), (cut, d_ff))


def _ffn_kernel(yf_ref, yfp_ref, yr_ref, yrp_ref, ym_ref, ymp_ref, x_ref, xp_ref, mod_ref, nw_ref, wo_ref,
                wu_ref, cw_ref, cb_ref, wd_ref, nf_ref, o_ref, *, nt, d_ff, final):
    i = pl.program_id(0)
    m = mod_ref[0]

    def ext(prev_ref, cur_ref):
        return jnp.concatenate([prev_ref[...], cur_ref[...]], axis=0)

    z = _dot(ext(yfp_ref, yf_ref).astype(BF16), wo_ref[0:FOX_W, :])
    z = z + _dot(ext(yrp_ref, yr_ref).astype(BF16), wo_ref[FOX_W:FOX_W + RWKV_W, :])
    z = z + _dot(ext(ymp_ref, ym_ref).astype(BF16), wo_ref[FOX_W + RWKV_W:, :])
    x1 = ext(xp_ref, x_ref) + m[2:3, :] * z
    h = _rmsnorm(x1, nw_ref[...]) * (1.0 + m[4:5, :]) + m[3:4, :]
    keep = jnp.where(i % nt == 0, 0.0, 1.0)
    he = jnp.concatenate([h[0:FFN_HALO, :] * keep, h[FFN_HALO:, :]], axis=0).astype(BF16)

    def conv(u, lo, hi):
        cw = cw_ref[:, lo:hi]
        return (cb_ref[:, lo:hi] + cw[0:1, :] * pltpu.roll(u, 2, 0)[FFN_HALO:, :]
                + cw[1:2, :] * pltpu.roll(u, 1, 0)[FFN_HALO:, :] + cw[2:3, :] * u[FFN_HALO:, :])

    chunks = _ffn_chunks(d_ff)
    ups = [(_dot(he, wu_ref[:, lo:hi]), _dot(he, wu_ref[:, d_ff + lo:d_ff + hi])) for lo, hi in chunks]
    y = None
    for (lo, hi), (ug, uv) in zip(chunks, ups):
        g = conv(ug, lo, hi)
        act = g * _sigmoid(g) * conv(uv, d_ff + lo, d_ff + hi)
        part = _dot(act.astype(BF16), wd_ref[lo:hi, :])
        y = part if y is None else y + part
    out = x1[FFN_HALO:, :] + m[5:6, :] * y
    if final:
        out = _rmsnorm(out, nf_ref[...])
    o_ref[...] = out


def _ffn_call(y_fox, y_rwkv, y_moba, xf, mod_l, norm_w, w_out_b, w_up_b, conv_w, conv_b, w_down_b, norm_final,
              layer, seq, final):
    rows, d = xf.shape
    d_ff = w_down_b.shape[1]
    tm = 512
    nt = seq // tm
    hb = tm // FFN_HALO
    kern = functools.partial(_ffn_kernel, nt=nt, d_ff=d_ff, final=final)

    def tile_and_halo(width):
        return [pl.BlockSpec((tm, width), lambda i: (i, 0)),
                pl.BlockSpec((FFN_HALO, width), lambda i: (jnp.maximum(i * hb - 1, 0), 0))]

    def resident(a):
        return pl.BlockSpec((None,) + a.shape[1:], lambda i: (layer, 0, 0), pipeline_mode=pl.Buffered(1))

    return pl.pallas_call(
        kern,
        grid=(rows // tm,),
        in_specs=tile_and_halo(FOX_W) + tile_and_halo(RWKV_W) + tile_and_halo(MOBA_W) + tile_and_halo(d) + [
            pl.BlockSpec((1, 6, d), lambda i: (i // nt, 0, 0)),
            pl.BlockSpec((1, d), lambda i: (0, 0)),
            resident(w_out_b), resident(w_up_b),
            pl.BlockSpec((CONV_W, 2 * d_ff), lambda i: (0, 0)),
            pl.BlockSpec((1, 2 * d_ff), lambda i: (0, 0)),
            resident(w_down_b),
            pl.BlockSpec((1, d), lambda i: (0, 0)),
        ],
        out_specs=pl.BlockSpec((tm, d), lambda i: (i, 0)),
        out_shape=jax.ShapeDtypeStruct((rows, d), F32),
        compiler_params=_cparams(("arbitrary",)),
        name="out_proj_conv_ffn",
    )(y_fox, y_fox, y_rwkv, y_rwkv, y_moba, y_moba, xf, xf, mod_l, norm_w, w_out_b, w_up_b, conv_w, conv_b,
      w_down_b, norm_final)


def kernel(x, c, w_mod, b_mod, norm_mix, w_in, fox_f_bias, rwkv_mu, rwkv_w0, rwkv_w2, rwkv_a0, rwkv_a2,
           rwkv_g2, rwkv_k_k, rwkv_k_a, rwkv_r_k, rwkv_ln_w, rwkv_ln_b, w_out, norm_ffn, w_up, conv_w,
           conv_b, w_down, norm_final):
    batch, seq, d = x.shape
    n_layers = w_mod.shape[0]

    mod = _mod_call(c, w_mod, b_mod).reshape(n_layers, batch, 6, d)
    w_in_p = _win_layout_call(w_in)
    f_bias = jnp.pad(fox_f_bias, ((0, 0), (0, FF_PAD - FOX_HEADS)))
    w_out_b = w_out.astype(BF16)
    w_up_b = w_up.astype(BF16)
    w_down_b = w_down.astype(BF16)
    g2_b = rwkv_g2.astype(BF16)

    xf = x.reshape(batch * seq, d)
    for l in range(n_layers):
        row = lambda a: a[l].reshape(1, -1)
        prm = {
            "w0": row(rwkv_w0), "w2": rwkv_w2[l], "a0": row(rwkv_a0), "a2": rwkv_a2[l], "g2": g2_b[l],
            "k_k": row(rwkv_k_k), "k_a": row(rwkv_k_a), "r_k": row(rwkv_r_k), "ln_w": row(rwkv_ln_w),
            "ln_b": row(rwkv_ln_b),
        }
        p = _inproj_call(xf, mod[l], row(norm_mix), w_in_p, row(rwkv_mu), l, seq)
        y_fox = _fox_call(p, f_bias[l:l + 1], batch, seq)
        y_moba = _moba_call(p, batch, seq)
        y_rwkv = _rwkv_call(p, prm, batch, seq)
        xf = _ffn_call(y_fox, y_rwkv, y_moba, xf, mod[l], row(norm_ffn), w_out_b, w_up_b, conv_w[l],
                       conv_b[l].reshape(1, -1), w_down_b, norm_final.reshape(1, -1), l, seq,
                       final=(l == n_layers - 1))
    return xf.reshape(batch, seq, d)
```

```python
import functools

import jax
import jax.numpy as jnp
import numpy as np
from jax import lax
from jax.experimental import pallas as pl
from jax.experimental.pallas import tpu as pltpu

F32 = jnp.float32
BF16 = jnp.bfloat16

HEAD_DIM = 64
FOX_HEADS = 4
RWKV_HEADS = 8
MOBA_HEADS = 4
FOX_W = FOX_HEADS * HEAD_DIM
RWKV_W = RWKV_HEADS * HEAD_DIM
MOBA_W = MOBA_HEADS * HEAD_DIM
DECAY_LORA = 64
AAA_LORA = 64
GATE_LORA = 128
LORA_W = DECAY_LORA + AAA_LORA + GATE_LORA
MOBA_BLOCK = 256
MOBA_TOPK = 3
CONV_W = 3
NORM_EPS = 1e-6
GN_EPS = 64e-5
LOG2E = float(np.log2(np.e))
ATTN_SCALE = HEAD_DIM ** -0.5 * LOG2E

C_FQ, C_FK, C_FV = 0, 256, 512
C_MQ, C_MK, C_MV = 768, 1024, 1280
C_RR, C_RK, C_RV = 1536, 2048, 2560
C_LORA = 3072
C_FF = 3328
FF_PAD = 128
NP_COLS = C_FF + FF_PAD

RWKV_CHUNK = 64
LANES = 128
SUBLANES = 8
MXU_N = 256
VMEM_LIMIT = 56 * 1024 * 1024


def _cparams(sem):
    return pltpu.CompilerParams(dimension_semantics=sem, vmem_limit_bytes=VMEM_LIMIT)


def _dot(a, b):
    return jnp.dot(a, b, preferred_element_type=F32)


def _dot_nt(a, b):
    return lax.dot_general(a, b, (((1,), (1,)), ((), ())), preferred_element_type=F32)


def _mm(a, b):
    return jnp.dot(a.astype(BF16), b.astype(BF16), preferred_element_type=F32)


def _mm_nt(a, b):
    return _dot_nt(a.astype(BF16), b.astype(BF16))


def _split2(x):
    hi = x.astype(BF16)
    return hi, (x - hi.astype(F32)).astype(BF16)


def _dot_split_lhs(x, w_bf16):
    m = x.shape[0]
    r = _dot(jnp.concatenate(_split2(x), axis=0), w_bf16)
    return r[0:m, :] + r[m:2 * m, :]


def _head_sums(x, ones_blk, exact):
    parts = []
    for c0 in range(0, x.shape[1], MXU_N):
        xc = x[:, c0:c0 + MXU_N]
        parts.append(_dot_split_lhs(xc, ones_blk) if exact else _dot(xc.astype(BF16), ones_blk))
    return jnp.concatenate(parts, axis=1)


def _dot3(a, b):
    m = a.shape[0]
    a_hi, a_lo = _split2(a)
    b_hi, b_lo = _split2(b)
    r = _dot(jnp.concatenate([a_hi, a_lo], axis=0), b_hi)
    return r[0:m, :] + r[m:2 * m, :] + _dot(a_hi, b_lo)


def _dot3_nt(a, b):
    m = a.shape[0]
    a_hi, a_lo = _split2(a)
    b_hi, b_lo = _split2(b)
    r = _dot_nt(jnp.concatenate([a_hi, a_lo], axis=0), b_hi)
    return r[0:m, :] + r[m:2 * m, :] + _dot_nt(a_hi, b_lo)


def _sigmoid(x):
    return 0.5 * jnp.tanh(0.5 * x) + 0.5


def _log_sigmoid(x):
    return jnp.minimum(x, 0.0) - jnp.log1p(jnp.exp(-jnp.abs(x)))


def _rmsnorm(x, w):
    ms = jnp.mean(x * x, axis=-1, keepdims=True)
    return x * lax.rsqrt(ms + NORM_EPS) * w


def _seg_cumsum_rows(x, seg):
    row = lax.broadcasted_iota(jnp.int32, x.shape, 0) & (seg - 1)
    s = 1
    while s < seg:
        x = x + jnp.where(row >= s, pltpu.roll(x, s, 0), 0.0)
        s *= 2
    return x


def _mod_kernel(c_ref, w_ref, b_ref, o_ref):
    c = c_ref[...]
    o_ref[0] = _dot3(c * _sigmoid(c), w_ref[0]) + b_ref[0]


def _mod_call(c, w_mod, b_mod):
    n_layers, d, n = w_mod.shape
    b = c.shape[0]
    tn = 1536
    return pl.pallas_call(
        _mod_kernel,
        grid=(n_layers, n // tn),
        in_specs=[
            pl.BlockSpec((b, d), lambda l, j: (0, 0)),
            pl.BlockSpec((1, d, tn), lambda l, j: (l, 0, j)),
            pl.BlockSpec((1, 1, tn), lambda l, j: (l, 0, j)),
        ],
        out_specs=pl.BlockSpec((1, b, tn), lambda l, j: (l, 0, j)),
        out_shape=jax.ShapeDtypeStruct((n_layers, b, n), F32),
        compiler_params=_cparams(("arbitrary", "arbitrary")),
        name="adaln_mod",
    )(c, w_mod, b_mod.reshape(n_layers, 1, n))


def _win_layout_kernel(w_ref, o_ref):
    fox_cols = 3 * FOX_W + FOX_HEADS
    rwkv_cols = 3 * RWKV_W + LORA_W
    w = w_ref[...]
    o_ref[:, C_FQ:C_MQ] = w[:, 0:3 * FOX_W].astype(BF16)
    o_ref[:, C_MQ:C_RR] = w[:, fox_cols + rwkv_cols:fox_cols + rwkv_cols + 3 * MOBA_W].astype(BF16)
    o_ref[:, C_RR:C_FF] = w[:, fox_cols:fox_cols + rwkv_cols].astype(BF16)
    lane = lax.broadcasted_iota(jnp.int32, (w.shape[0], FF_PAD), 1)
    o_ref[:, C_FF:NP_COLS] = jnp.where(lane < FOX_HEADS, w[:, 3 * FOX_W:3 * FOX_W + FF_PAD], 0.0).astype(BF16)


def _win_layout_call(w_in):
    n_layers, d, n = w_in.shape
    tr = 256
    rows = n_layers * d
    out = pl.pallas_call(
        _win_layout_kernel,
        grid=(rows // tr,),
        in_specs=[pl.BlockSpec((tr, n), lambda i: (i, 0))],
        out_specs=pl.BlockSpec((tr, NP_COLS), lambda i: (i, 0)),
        out_shape=jax.ShapeDtypeStruct((rows, NP_COLS), BF16),
        compiler_params=_cparams(("arbitrary",)),
        name="w_in_layout",
    )(w_in.reshape(rows, n))
    return out.reshape(n_layers, d, NP_COLS)


def _inproj_kernel(x_ref, mod_ref, nw_ref, w_ref, mu_ref, o_ref, prev, *, tm, nt):
    i = pl.program_id(0)
    m = mod_ref[0]
    h = _rmsnorm(x_ref[...], nw_ref[...]) * (1.0 + m[1:2, :]) + m[0:1, :]
    p = _dot(h.astype(BF16), w_ref[...])
    o_ref[:, 0:C_RR] = p[:, 0:C_RR]
    o_ref[:, C_FF:NP_COLS] = p[:, C_FF:NP_COLS]
    feat = p[:, C_RR:C_FF]

    @pl.when(i % nt == 0)
    def _():
        prev[...] = jnp.zeros(prev.shape, F32)

    rolled = pltpu.roll(feat, 1, 0)
    first = lax.broadcasted_iota(jnp.int32, (SUBLANES, C_FF - C_RR), 0) == 0
    top = jnp.where(first, prev[...], rolled[0:SUBLANES, :])
    shifted = jnp.concatenate([top, rolled[SUBLANES:, :]], axis=0)
    prev[...] = feat[tm - 1:tm, :]
    o_ref[:, C_RR:C_FF] = feat + (shifted - feat) * mu_ref[...]


def _inproj_call(xf, mod_l, norm_w, w_in_p, mu_row, layer, seq):
    rows, d = xf.shape
    tm = 512
    nt = seq // tm
    kern = functools.partial(_inproj_kernel, tm=tm, nt=nt)
    return pl.pallas_call(
        kern,
        grid=(rows // tm,),
        in_specs=[
            pl.BlockSpec((tm, d), lambda i: (i, 0)),
            pl.BlockSpec((1, 6, d), lambda i: (i // nt, 0, 0)),
            pl.BlockSpec((1, d), lambda i: (0, 0)),
            pl.BlockSpec((None, d, NP_COLS), lambda i: (layer, 0, 0)),
            pl.BlockSpec((1, C_FF - C_RR), lambda i: (0, 0)),
        ],
        out_specs=pl.BlockSpec((tm, NP_COLS), lambda i: (i, 0)),
        out_shape=jax.ShapeDtypeStruct((rows, NP_COLS), F32),
        scratch_shapes=[pltpu.VMEM((1, C_FF - C_RR), F32)],
        compiler_params=_cparams(("arbitrary",)),
        name="in_proj",
    )(xf, mod_l, norm_w, w_in_p, mu_row)


def _softmax_tiles(s, carry, v_t):
    m_new = [jnp.maximum(c[0], jnp.max(x, axis=0, keepdims=True)) for x, c in zip(s, carry)]
    p = [jnp.exp2(x - m) for x, m in zip(s, m_new)]
    pv = [_dot(v, x.astype(BF16)) for v, x in zip(v_t, p)]
    out = []
    for (m, l, acc), mn, x, y in zip(carry, m_new, p, pv):
        alpha = jnp.exp2(m - mn)
        out.append((mn, alpha * l + jnp.sum(x, axis=0, keepdims=True), alpha * acc + y))
    return out


def _first_tiles(s, v_t):
    m = [jnp.max(x, axis=0, keepdims=True) for x in s]
    p = [jnp.exp2(x - mm) for x, mm in zip(s, m)]
    pv = [_dot(v, x.astype(BF16)) for v, x in zip(v_t, p)]
    return [(mm, jnp.sum(x, axis=0, keepdims=True), y) for mm, x, y in zip(m, p, pv)]


def _flatten(carry):
    return tuple(a for c in carry for a in c)


def _unflatten(flat):
    return [tuple(flat[3 * h:3 * h + 3]) for h in range(len(flat) // 3)]


def _store_heads(o_ref, carry):
    o_ref[...] = jnp.concatenate([acc / l for _, l, acc in carry], axis=0).T.astype(o_ref.dtype)


F_COL = HEAD_DIM
N_SPLIT = 3
FOX_XW = FOX_W + FF_PAD


def _fox_selectors():
    selk = np.zeros((FOX_XW, FOX_HEADS * LANES), np.float32)
    for h in range(FOX_HEADS):
        for c in range(HEAD_DIM):
            selk[h * HEAD_DIM + c, h * LANES + c] = 1.0
    selq = selk.T.copy()
    for h in range(FOX_HEADS):
        for s in range(N_SPLIT):
            selk[FOX_W + FOX_HEADS * s + h, h * LANES + F_COL + s] = 1.0
            selq[h * LANES + F_COL + N_SPLIT + s, FOX_W + FOX_HEADS * s + h] = 1.0
    return jnp.asarray(selk, BF16), jnp.asarray(selq, BF16)


def _fox_kernel(q_ref, k_ref, v_ref, ff_ref, fb_ref, selk_ref, selq_ref, o_ref, ka, v_t, fp, s_buf, *, tq, tk,
                n_kv):
    qi = pl.program_id(1)
    heads = range(FOX_HEADS)

    @pl.when(qi == 0)
    def _():
        lane = lax.broadcasted_iota(jnp.int32, (1, LANES), 1)
        f = _seg_cumsum_rows(_log_sigmoid(ff_ref[...] + fb_ref[...]), n_kv * tk) * LOG2E
        f = jnp.where(lane < FOX_HEADS, f, 0.0)
        hi = f.astype(BF16).astype(F32)
        mid = (f - hi).astype(BF16).astype(F32)
        lo = (f - hi - mid).astype(BF16).astype(F32)
        fp[...] = (hi + pltpu.roll(mid, FOX_HEADS, 1) + pltpu.roll(lo, 2 * FOX_HEADS, 1)).astype(BF16)
        x = jnp.concatenate([k_ref[...].astype(BF16), fp[...]], axis=1)
        lane4 = lax.broadcasted_iota(jnp.int32, (1, FOX_HEADS * LANES), 1) % LANES
        ones = jnp.where((lane4 >= F_COL + N_SPLIT) & (lane4 < F_COL + 2 * N_SPLIT), 1.0, 0.0)
        ka[...] = (_dot(x, selk_ref[...]) + ones).astype(BF16)
        for j in range(n_kv):
            v_t[j] = v_ref[j * tk:(j + 1) * tk, :].T.astype(BF16)

    q0 = pl.multiple_of(qi * tq, tq)
    xq = jnp.concatenate([(q_ref[...] * ATTN_SCALE).astype(BF16), fp[pl.ds(q0, tq), :]], axis=1)
    sub = lax.broadcasted_iota(jnp.int32, (FOX_HEADS * LANES, tq), 0) % LANES
    neg = jnp.where((sub >= F_COL) & (sub < F_COL + N_SPLIT), -1.0, 0.0)
    q_all = (_dot_nt(selq_ref[...], xq) + neg).astype(BF16)
    q_t = [q_all[h * LANES:(h + 1) * LANES, :] for h in heads]
    lag = (lax.broadcasted_iota(jnp.int32, (tk, tq), 0) - lax.broadcasted_iota(jnp.int32, (tk, tq), 1))
    n_diag = tq // tk

    def logits(j):
        k0 = pl.multiple_of(j * tk, tk)
        return [_dot(ka[pl.ds(k0, tk), h * LANES:(h + 1) * LANES], q_t[h]) for h in heads]

    def values(j):
        vj = v_t[j]
        return [vj[h * HEAD_DIM:(h + 1) * HEAD_DIM, :] for h in heads]

    def put(slot, s):
        for h in heads:
            s_buf[slot * FOX_HEADS + h] = s[h]

    j0 = qi * n_diag
    last = jnp.maximum(j0 - 1, 0)
    s_diag = [logits(j0 + dd) for dd in range(n_diag)]
    put(0, logits(0))
    carry = _first_tiles([jnp.where(lag <= 0, x, -jnp.inf) for x in s_diag[0]], values(j0))
    for dd in range(1, n_diag):
        carry = _softmax_tiles([jnp.where(lag <= -dd * tk, x, -jnp.inf) for x in s_diag[dd]], carry,
                               values(j0 + dd))

    def step(j, slot, carry):
        put(1 - slot, logits(jnp.minimum(j + 1, last)))
        s_cur = [s_buf[slot * FOX_HEADS + h] for h in heads]
        return _softmax_tiles(s_cur, carry, values(j))

    def body(jj, flat):
        carry = _unflatten(flat)
        for slot in range(2):
            carry = step(2 * jj + slot, slot, carry)
        return _flatten(carry)

    assert n_diag % 2 == 0
    _store_heads(o_ref, _unflatten(lax.fori_loop(0, j0 // 2, body, _flatten(carry))))


def _fox_call(p, f_bias_row, batch, seq):
    tq = 512
    tk = 256
    nq = seq // tq
    selk, selq = _fox_selectors()
    kern = functools.partial(_fox_kernel, tq=tq, tk=tk, n_kv=seq // tk)
    return pl.pallas_call(
        kern,
        grid=(batch, nq),
        in_specs=[
            pl.BlockSpec((tq, FOX_W), lambda b, i: (b * nq + i, C_FQ // FOX_W)),
            pl.BlockSpec((seq, FOX_W), lambda b, i: (b, C_FK // FOX_W)),
            pl.BlockSpec((seq, FOX_W), lambda b, i: (b, C_FV // FOX_W)),
            pl.BlockSpec((seq, FF_PAD), lambda b, i: (b, C_FF // FF_PAD)),
            pl.BlockSpec((1, FF_PAD), lambda b, i: (0, 0)),
            pl.BlockSpec(selk.shape, lambda b, i: (0, 0)),
            pl.BlockSpec(selq.shape, lambda b, i: (0, 0)),
        ],
        out_specs=pl.BlockSpec((tq, FOX_W), lambda b, i: (b * nq + i, 0)),
        out_shape=jax.ShapeDtypeStruct((batch * seq, FOX_W), BF16),
        scratch_shapes=[
            pltpu.VMEM((seq, FOX_HEADS * LANES), BF16),
            pltpu.VMEM((seq // tk, FOX_W, tk), BF16),
            pltpu.VMEM((seq, FF_PAD), BF16),
            pltpu.VMEM((2 * FOX_HEADS, tk, tq), F32),
        ],
        compiler_params=_cparams(("arbitrary", "arbitrary")),
        name="fox_attention",
    )(p, p, p, p, f_bias_row, selk, selq)


def _moba_selectors():
    selk = np.zeros((MOBA_W, MOBA_HEADS * LANES), np.float32)
    for h in range(MOBA_HEADS):
        for c in range(HEAD_DIM):
            selk[h * HEAD_DIM + c, h * LANES + c] = 1.0
    return jnp.asarray(selk, BF16), jnp.asarray(selk.T, BF16)


def _moba_kernel(q_ref, k_ref, v_ref, selk_ref, selq_ref, o_ref, ka, v_t, kmean, s_buf, *, blk, n_kb):
    own = pl.program_id(1)
    heads = range(MOBA_HEADS)

    @pl.when(own == 0)
    def _():
        kf = k_ref[...]
        ka[...] = _dot(kf.astype(BF16), selk_ref[...]).astype(BF16)
        means = [jnp.mean(kf[n * blk:(n + 1) * blk, :], axis=0, keepdims=True) for n in range(n_kb)]
        if n_kb < SUBLANES:
            means.append(jnp.zeros((SUBLANES - n_kb, MOBA_W), F32))
        means = jnp.concatenate(means, axis=0)
        head_of_lane = lax.broadcasted_iota(jnp.int32, (SUBLANES, MOBA_W), 1) // HEAD_DIM
        for h in heads:
            kmean[h * SUBLANES:(h + 1) * SUBLANES, :] = jnp.where(head_of_lane == h, means, 0.0)
        for n in range(n_kb):
            v_t[n] = v_ref[n * blk:(n + 1) * blk, :].T.astype(BF16)

    qf = q_ref[...]
    qs = (qf * ATTN_SCALE).astype(BF16)
    q_all = _dot_nt(selq_ref[...], qs).astype(BF16)
    q_t = [q_all[h * LANES:(h + 1) * LANES, :] for h in heads]
    sub = lax.broadcasted_iota(jnp.int32, (SUBLANES, blk), 0)
    past = sub < own
    causal = (lax.broadcasted_iota(jnp.int32, (blk, blk), 0) <= lax.broadcasted_iota(jnp.int32, (blk, blk), 1))

    sel = []
    gates = _dot3_nt(kmean[...], qf)
    for h in heads:
        gate = gates[h * SUBLANES:(h + 1) * SUBLANES, :]
        selm = jnp.zeros((SUBLANES, blk), F32)
        for n in range(n_kb):
            gn = gate[n:n + 1, :]
            beats = past & ((gate > gn) | ((gate == gn) & (sub < n)))
            rank = jnp.sum(beats.astype(F32), axis=0, keepdims=True)
            selm = jnp.where(sub == n, (rank < MOBA_TOPK).astype(F32), selm)
        sel.append(jnp.where(past, selm, 0.0))

    def logits(j):
        k0 = pl.multiple_of(j * blk, blk)
        return [_dot(ka[pl.ds(k0, blk), h * LANES:(h + 1) * LANES], q_t[h]) for h in heads]

    def values(j):
        vj = v_t[j]
        return [vj[h * HEAD_DIM:(h + 1) * HEAD_DIM, :] for h in heads]

    def put(slot, s):
        for h in heads:
            s_buf[slot * MOBA_HEADS + h] = s[h]

    def step(j, slot, carry, live, prefetch):
        if prefetch:
            put(1 - slot, logits(jnp.minimum(j + 1, last)))
        picked = [(jnp.sum(jnp.where(sub == j, sel[h], 0.0), axis=0, keepdims=True) > 0.5) & live for h in heads]
        s = [jnp.where(pk, s_buf[slot * MOBA_HEADS + h], -jnp.inf) for h, pk in zip(heads, picked)]
        return _softmax_tiles(s, carry, values(j))

    last = jnp.maximum(own - 1, 0)
    s_own = logits(own)
    put(0, logits(0))
    carry = _first_tiles([jnp.where(causal, x, -jnp.inf) for x in s_own], values(own))

    def body(jj, flat):
        carry = _unflatten(flat)
        for slot in range(2):
            carry = step(2 * jj + slot, slot, carry, True, True)
        return _flatten(carry)

    carry = _unflatten(lax.fori_loop(0, own // 2, body, _flatten(carry)))
    carry = step(last, 0, carry, own % 2 == 1, False)
    _store_heads(o_ref, carry)


def _moba_call(p, batch, seq):
    blk = MOBA_BLOCK
    n_kb = seq // blk
    assert n_kb <= SUBLANES
    selk, selq = _moba_selectors()
    kern = functools.partial(_moba_kernel, blk=blk, n_kb=n_kb)
    return pl.pallas_call(
        kern,
        grid=(batch, n_kb),
        in_specs=[
            pl.BlockSpec((blk, MOBA_W), lambda b, i: (b * n_kb + i, C_MQ // MOBA_W)),
            pl.BlockSpec((seq, MOBA_W), lambda b, i: (b, C_MK // MOBA_W)),
            pl.BlockSpec((seq, MOBA_W), lambda b, i: (b, C_MV // MOBA_W)),
            pl.BlockSpec(selk.shape, lambda b, i: (0, 0)),
            pl.BlockSpec(selq.shape, lambda b, i: (0, 0)),
        ],
        out_specs=pl.BlockSpec((blk, MOBA_W), lambda b, i: (b * n_kb + i, 0)),
        out_shape=jax.ShapeDtypeStruct((batch * seq, MOBA_W), BF16),
        scratch_shapes=[
            pltpu.VMEM((seq, MOBA_HEADS * LANES), BF16),
            pltpu.VMEM((n_kb, MOBA_W, blk), BF16),
            pltpu.VMEM((MOBA_HEADS * SUBLANES, MOBA_W), F32),
            pltpu.VMEM((2 * MOBA_HEADS, blk, blk), F32),
        ],
        compiler_params=_cparams(("arbitrary", "arbitrary")),
        name="moba_attention",
    )(p, p, p, selk, selq)


DECAY_SCALE = float(np.exp(-0.5))


def _rwkv_kernel(r_ref, k_ref, v_ref, lo_ref, w0, w2, a0, a2, g2, kkw, kaw, rkw, lnw, lnb, o_ref,
                 state, a_t, r_t, k_t, b_t, k_b, b_b, v_s, w_c, y_s, *, tm, sub):
    i = pl.program_id(1)
    ch = RWKV_CHUNK
    n_ch = sub // ch

    @pl.when(i == 0)
    def _():
        state[...] = jnp.zeros(state.shape, F32)

    gi = lax.broadcasted_iota(jnp.int32, (MXU_N, MXU_N), 0) // HEAD_DIM
    gj = lax.broadcasted_iota(jnp.int32, (MXU_N, MXU_N), 1) // HEAD_DIM
    group = (gi == gj).astype(BF16)
    ti = lax.broadcasted_iota(jnp.int32, (ch, ch), 0)
    tj = lax.broadcasted_iota(jnp.int32, (ch, ch), 1)
    strict = tj < ti
    incl = tj <= ti

    def project(r0):
        lo = lo_ref[r0:r0 + sub, :]
        w_lo = lo[:, 0:DECAY_LORA]
        a_lo = lo[:, DECAY_LORA:DECAY_LORA + AAA_LORA]
        g_lo = lo[:, DECAY_LORA + AAA_LORA:LORA_W]
        kk = k_ref[r0:r0 + sub, :] * kkw[...]
        return (_dot3(jnp.tanh(w_lo), w2[...]), _dot3(a_lo, a2[...]), _dot(_sigmoid(g_lo).astype(BF16), g2[...]),
                _head_sums(kk * kk, group, exact=False))

    def prepare(r0, zw, za, gate, ss):
        span = slice(r0, r0 + sub)
        r = r_ref[span, :]
        k = k_ref[span, :]
        v = v_ref[span, :]

        lw = -DECAY_SCALE * _sigmoid(w0[...] + zw)
        eta = _sigmoid(a0[...] + za)

        kk = k * kkw[...] * jnp.minimum(lax.rsqrt(ss), 1e12)
        kp = k * (1.0 + (eta - 1.0) * kaw[...])
        bb = kk * eta

        lc = _seg_cumsum_rows(lw, ch)
        a_t[span, :] = -kk * jnp.exp(lc - lw)
        r_t[span, :] = r * jnp.exp(lc)
        einv = jnp.exp(-lc)
        k_t[span, :] = kp * einv
        b_t[span, :] = bb * einv
        v_s[span, :] = v
        for c in range(n_ch):
            rows = slice(c * ch, (c + 1) * ch)
            dst = slice(r0 + c * ch, r0 + (c + 1) * ch)
            last = lc[(c + 1) * ch - 1:(c + 1) * ch, :]
            e = jnp.exp(last - lc[rows, :])
            k_b[dst, :] = kp[rows, :] * e
            b_b[dst, :] = bb[rows, :] * e
            w_c[r0 // ch + c] = jnp.broadcast_to(jnp.exp(last), (8, RWKV_W))
        return r * kp * rkw[...], v, gate

    def state_free(r0):
        chains = [(c, h) for c in range(n_ch) for h in range(RWKV_HEADS)]

        def tile(ref):
            return [ref[r0 + c * ch:r0 + (c + 1) * ch, h * HEAD_DIM:(h + 1) * HEAD_DIM] for c, h in chains]

        return _rwkv_state_free(tile(a_t), tile(r_t), tile(v_s), tile(k_b), tile(b_b), tile(b_t), tile(k_t),
                                strict, incl)

    def finish(r0, bonus_arg, v, gate):
        span = slice(r0, r0 + sub)
        y = y_s[span, :]
        inv_d = 1.0 / HEAD_DIM
        mean = _head_sums(y, group, exact=False) * inv_d
        d = y - mean
        var = _head_sums(d * d, group, exact=False) * inv_d
        yn = d * lax.rsqrt(var + GN_EPS) * lnw[...] + lnb[...]
        bonus = _head_sums(bonus_arg, group, exact=True) * v
        o_ref[span, :] = ((yn + bonus) * gate).astype(o_ref.dtype)

    starts = list(range(0, tm, sub))
    s_cur = [state[h] for h in range(RWKV_HEADS)]
    prepared = [prepare(r0, *project(r0)) for r0 in starts]
    pending = []
    for r0 in starts:
        stages = state_free(r0)
        while True:
            try:
                next(stages)
            except StopIteration as done:
                free = done.value
                break
            if pending:
                s_cur = pending.pop(0)(s_cur)
        while pending:
            s_cur = pending.pop(0)(s_cur)
        pending = [functools.partial(_rwkv_state_step, c, r0, free, w_c=w_c, y_s=y_s) for c in range(n_ch)]
    while pending:
        s_cur = pending.pop(0)(s_cur)
    for h in range(RWKV_HEADS):
        state[h] = s_cur[h]
    for r0, vals in zip(starts, prepared):
        finish(r0, *vals)


def _rwkv_state_free(at, rt, vv, kbar, bbar, btl, ktl, strict, incl):
    ch = RWKV_CHUNK
    ids = range(len(at))
    pad = jnp.zeros((LANES - ch, HEAD_DIM), BF16)
    m4 = [_dot_nt(jnp.concatenate([at[i], rt[i]], axis=0).astype(BF16),
                  jnp.concatenate([btl[i].astype(BF16), pad, ktl[i].astype(BF16), pad], axis=0)) for i in ids]
    yield
    a_ab = [jnp.where(strict, m4[i][0:ch, 0:ch], 0.0) for i in ids]
    a_ak = [jnp.where(strict, m4[i][0:ch, LANES:LANES + ch], 0.0) for i in ids]
    a_rb = [jnp.where(incl, m4[i][ch:2 * ch, 0:ch], 0.0) for i in ids]
    a_rk = [jnp.where(incl, m4[i][ch:2 * ch, LANES:LANES + ch], 0.0) for i in ids]
    avk = [_mm(jnp.concatenate([a_ak[i], a_rk[i]], axis=0), vv[i]) for i in ids]
    yield
    pw = a_ab
    tx = [jnp.concatenate([at[i], avk[i][0:ch, :]], axis=1) for i in ids]
    span = 1
    while 2 * span < ch:
        x = [_mm(pw[i], jnp.concatenate([tx[i], pw[i]], axis=1)) for i in ids]
        pw = [x[i][:, 2 * ch:3 * ch] for i in ids]
        tx = [tx[i] + x[i][:, 0:2 * ch] for i in ids]
        span *= 2
        yield
    tx = [tx[i] + _mm(pw[i], tx[i]) for i in ids]
    yield
    ry = [_mm(a_rb[i], tx[i]) for i in ids]
    rhat = [rt[i] + ry[i][:, 0:ch] for i in ids]
    yhat = [avk[i][ch:2 * ch, :] + ry[i][:, ch:2 * ch] for i in ids]
    yield
    z = [_mm(tx[i][:, 0:ch].T, bbar[i]) for i in ids]
    yield
    kv = [_mm(jnp.concatenate([vv[i], tx[i][:, ch:2 * ch]], axis=0).T,
              jnp.concatenate([kbar[i], bbar[i]], axis=0)) for i in ids]
    return rhat, yhat, z, kv


def _rwkv_state_step(c, r0, free, s_cur, w_c, y_s):
    ch = RWKV_CHUNK
    rhat, yhat, z, kv = free
    base = c * RWKV_HEADS
    ys = [_mm_nt(rhat[base + h], s_cur[h]) for h in range(RWKV_HEADS)]
    sz = [_mm(s_cur[h], z[base + h]) for h in range(RWKV_HEADS)]
    out = []
    for h in range(RWKV_HEADS):
        hs = slice(h * HEAD_DIM, (h + 1) * HEAD_DIM)
        y_s[r0 + c * ch:r0 + (c + 1) * ch, hs] = ys[h] + yhat[base + h]
        out.append(s_cur[h] * w_c[r0 // ch + c][0:1, hs] + sz[h] + kv[base + h])
    return out


def _rwkv_call(p, prm, batch, seq):
    tm = 512
    sub = 128
    nt = seq // tm
    kern = functools.partial(_rwkv_kernel, tm=tm, sub=sub)

    def rows(width, cstart):
        return pl.BlockSpec((tm, width), lambda b, i: (b * nt + i, cstart // width))

    def full(a):
        return pl.BlockSpec(a.shape, lambda b, i: (0,) * a.ndim)

    params = [prm[n] for n in ("w0", "w2", "a0", "a2", "g2", "k_k", "k_a", "r_k", "ln_w", "ln_b")]
    big = pltpu.VMEM((tm, RWKV_W), F32)
    return pl.pallas_call(
        kern,
        grid=(batch, nt),
        in_specs=[rows(RWKV_W, C_RR), rows(RWKV_W, C_RK), rows(RWKV_W, C_RV), rows(LORA_W, C_LORA)]
        + [full(a) for a in params],
        out_specs=pl.BlockSpec((tm, RWKV_W), lambda b, i: (b * nt + i, 0)),
        out_shape=jax.ShapeDtypeStruct((batch * seq, RWKV_W), BF16),
        scratch_shapes=[
            pltpu.VMEM((RWKV_HEADS, HEAD_DIM, HEAD_DIM), F32),
            big, big, big, big, big, big, big,
            pltpu.VMEM((tm // RWKV_CHUNK, 8, RWKV_W), F32),
            big,
        ],
        compiler_params=_cparams(("arbitrary", "arbitrary")),
        name="rwkv7_mix",
    )(p, p, p, p, *params)


FFN_HALO = 16


MXU_K = 256


def _ffn_chunks(d_ff):
    cut = (d_ff // MXU_K + 1) // 2 * MXU_K
    return ((0, cut), (cut, d_ff))


def _ffn_kernel(yf_ref, yfp_ref, yr_ref, yrp_ref, ym_ref, ymp_ref, x_ref, xp_ref, mod_ref, nw_ref, wo_ref,
                wu_ref, cw_ref, cb_ref, wd_ref, nf_ref, o_ref, *, nt, d_ff, final):
    i = pl.program_id(0)
    m = mod_ref[0]

    def ext(prev_ref, cur_ref):
        return jnp.concatenate([prev_ref[...], cur_ref[...]], axis=0)

    z = _dot(ext(yfp_ref, yf_ref).astype(BF16), wo_ref[0:FOX_W, :])
    z = z + _dot(ext(yrp_ref, yr_ref).astype(BF16), wo_ref[FOX_W:FOX_W + RWKV_W, :])
    z = z + _dot(ext(ymp_ref, ym_ref).astype(BF16), wo_ref[FOX_W + RWKV_W:, :])
    x1 = ext(xp_ref, x_ref) + m[2:3, :] * z
    h = _rmsnorm(x1, nw_ref[...]) * (1.0 + m[4:5, :]) + m[3:4, :]
    keep = jnp.where(i % nt == 0, 0.0, 1.0)
    he = jnp.concatenate([h[0:FFN_HALO, :] * keep, h[FFN_HALO:, :]], axis=0).astype(BF16)

    def conv(u, lo, hi):
        cw = cw_ref[:, lo:hi]
        return (cb_ref[:, lo:hi] + cw[0:1, :] * pltpu.roll(u, 2, 0)[FFN_HALO:, :]
                + cw[1:2, :] * pltpu.roll(u, 1, 0)[FFN_HALO:, :] + cw[2:3, :] * u[FFN_HALO:, :])

    chunks = _ffn_chunks(d_ff)
    ups = [(_dot(he, wu_ref[:, lo:hi]), _dot(he, wu_ref[:, d_ff + lo:d_ff + hi])) for lo, hi in chunks]
    y = None
    for (lo, hi), (ug, uv) in zip(chunks, ups):
        g = conv(ug, lo, hi)
        act = g * _sigmoid(g) * conv(uv, d_ff + lo, d_ff + hi)
        part = _dot(act.astype(BF16), wd_ref[lo:hi, :])
        y = part if y is None else y + part
    out = x1[FFN_HALO:, :] + m[5:6, :] * y
    if final:
        out = _rmsnorm(out, nf_ref[...])
    o_ref[...] = out


def _ffn_call(y_fox, y_rwkv, y_moba, xf, mod_l, norm_w, w_out_b, w_up_b, conv_w, conv_b, w_down_b, norm_final,
              layer, seq, final):
    rows, d = xf.shape
    d_ff = w_down_b.shape[1]
    tm = 512
    nt = seq // tm
    hb = tm // FFN_HALO
    kern = functools.partial(_ffn_kernel, nt=nt, d_ff=d_ff, final=final)

    def tile_and_halo(width):
        return [pl.BlockSpec((tm, width), lambda i: (i, 0)),
                pl.BlockSpec((FFN_HALO, width), lambda i: (jnp.maximum(i * hb - 1, 0), 0))]

    def resident(a):
        return pl.BlockSpec((None,) + a.shape[1:], lambda i: (layer, 0, 0), pipeline_mode=pl.Buffered(1))

    return pl.pallas_call(
        kern,
        grid=(rows // tm,),
        in_specs=tile_and_halo(FOX_W) + tile_and_halo(RWKV_W) + tile_and_halo(MOBA_W) + tile_and_halo(d) + [
            pl.BlockSpec((1, 6, d), lambda i: (i // nt, 0, 0)),
            pl.BlockSpec((1, d), lambda i: (0, 0)),
            resident(w_out_b), resident(w_up_b),
            pl.BlockSpec((CONV_W, 2 * d_ff), lambda i: (0, 0)),
            pl.BlockSpec((1, 2 * d_ff), lambda i: (0, 0)),
            resident(w_down_b),
            pl.BlockSpec((1, d), lambda i: (0, 0)),
        ],
        out_specs=pl.BlockSpec((tm, d), lambda i: (i, 0)),
        out_shape=jax.ShapeDtypeStruct((rows, d), F32),
        compiler_params=_cparams(("arbitrary",)),
        name="out_proj_conv_ffn",
    )(y_fox, y_fox, y_rwkv, y_rwkv, y_moba, y_moba, xf, xf, mod_l, norm_w, w_out_b, w_up_b, conv_w, conv_b,
      w_down_b, norm_final)


def kernel(x, c, w_mod, b_mod, norm_mix, w_in, fox_f_bias, rwkv_mu, rwkv_w0, rwkv_w2, rwkv_a0, rwkv_a2,
           rwkv_g2, rwkv_k_k, rwkv_k_a, rwkv_r_k, rwkv_ln_w, rwkv_ln_b, w_out, norm_ffn, w_up, conv_w,
           conv_b, w_down, norm_final):
    batch, seq, d = x.shape
    n_layers = w_mod.shape[0]

    mod = _mod_call(c, w_mod, b_mod).reshape(n_layers, batch, 6, d)
    w_in_p = _win_layout_call(w_in)
    f_bias = jnp.pad(fox_f_bias, ((0, 0), (0, FF_PAD - FOX_HEADS)))
    w_out_b = w_out.astype(BF16)
    w_up_b = w_up.astype(BF16)
    w_down_b = w_down.astype(BF16)
    g2_b = rwkv_g2.astype(BF16)

    xf = x.reshape(batch * seq, d)
    for l in range(n_layers):
        row = lambda a: a[l].reshape(1, -1)
        prm = {
            "w0": row(rwkv_w0), "w2": rwkv_w2[l], "a0": row(rwkv_a0), "a2": rwkv_a2[l], "g2": g2_b[l],
            "k_k": row(rwkv_k_k), "k_a": row(rwkv_k_a), "r_k": row(rwkv_r_k), "ln_w": row(rwkv_ln_w),
            "ln_b": row(rwkv_ln_b),
        }
        p = _inproj_call(xf, mod[l], row(norm_mix), w_in_p, row(rwkv_mu), l, seq)
        y_fox = _fox_call(p, f_bias[l:l + 1], batch, seq)
        y_moba = _moba_call(p, batch, seq)
        y_rwkv = _rwkv_call(p, prm, batch, seq)
        xf = _ffn_call(y_fox, y_rwkv, y_moba, xf, mod[l], row(norm_ffn), w_out_b, w_up_b, conv_w[l],
                       conv_b[l].reshape(1, -1), w_down_b, norm_final.reshape(1, -1), l, seq,
                       final=(l == n_layers - 1))
    return xf.reshape(batch, seq, d)
```

```python
import functools

import jax
import jax.numpy as jnp
import numpy as np
from jax import lax
from jax.experimental import pallas as pl
from jax.experimental.pallas import tpu as pltpu

F32 = jnp.float32
BF16 = jnp.bfloat16

HEAD_DIM = 64
FOX_HEADS = 4
RWKV_HEADS = 8
MOBA_HEADS = 4
FOX_W = FOX_HEADS * HEAD_DIM
RWKV_W = RWKV_HEADS * HEAD_DIM
MOBA_W = MOBA_HEADS * HEAD_DIM
DECAY_LORA = 64
AAA_LORA = 64
GATE_LORA = 128
LORA_W = DECAY_LORA + AAA_LORA + GATE_LORA
MOBA_BLOCK = 256
MOBA_TOPK = 3
CONV_W = 3
NORM_EPS = 1e-6
GN_EPS = 64e-5
LOG2E = float(np.log2(np.e))
ATTN_SCALE = HEAD_DIM ** -0.5 * LOG2E

C_FQ, C_FK, C_FV = 0, 256, 512
C_MQ, C_MK, C_MV = 768, 1024, 1280
C_RR, C_RK, C_RV = 1536, 2048, 2560
C_LORA = 3072
C_FF = 3328
FF_PAD = 128
NP_COLS = C_FF + FF_PAD

RWKV_CHUNK = 64
LANES = 128
SUBLANES = 8
MXU_N = 256
VMEM_LIMIT = 56 * 1024 * 1024


def _cparams(sem):
    return pltpu.CompilerParams(dimension_semantics=sem, vmem_limit_bytes=VMEM_LIMIT)


def _dot(a, b):
    return jnp.dot(a, b, preferred_element_type=F32)


def _dot_nt(a, b):
    return lax.dot_general(a, b, (((1,), (1,)), ((), ())), preferred_element_type=F32)


def _mm(a, b):
    return jnp.dot(a.astype(BF16), b.astype(BF16), preferred_element_type=F32)


def _mm_nt(a, b):
    return _dot_nt(a.astype(BF16), b.astype(BF16))


def _split2(x):
    hi = x.astype(BF16)
    return hi, (x - hi.astype(F32)).astype(BF16)


def _dot_split_lhs(x, w_bf16):
    m = x.shape[0]
    r = _dot(jnp.concatenate(_split2(x), axis=0), w_bf16)
    return r[0:m, :] + r[m:2 * m, :]


def _head_sums(x, ones_blk, exact):
    parts = []
    for c0 in range(0, x.shape[1], MXU_N):
        xc = x[:, c0:c0 + MXU_N]
        parts.append(_dot_split_lhs(xc, ones_blk) if exact else _dot(xc.astype(BF16), ones_blk))
    return jnp.concatenate(parts, axis=1)


def _dot3(a, b):
    m = a.shape[0]
    a_hi, a_lo = _split2(a)
    b_hi, b_lo = _split2(b)
    r = _dot(jnp.concatenate([a_hi, a_lo], axis=0), b_hi)
    return r[0:m, :] + r[m:2 * m, :] + _dot(a_hi, b_lo)


def _dot3_nt(a, b):
    m = a.shape[0]
    a_hi, a_lo = _split2(a)
    b_hi, b_lo = _split2(b)
    r = _dot_nt(jnp.concatenate([a_hi, a_lo], axis=0), b_hi)
    return r[0:m, :] + r[m:2 * m, :] + _dot_nt(a_hi, b_lo)


def _sigmoid(x):
    return 0.5 * jnp.tanh(0.5 * x) + 0.5


def _log_sigmoid(x):
    return jnp.minimum(x, 0.0) - jnp.log1p(jnp.exp(-jnp.abs(x)))


def _rmsnorm(x, w):
    ms = jnp.mean(x * x, axis=-1, keepdims=True)
    return x * lax.rsqrt(ms + NORM_EPS) * w


def _cumsum_rows(x, blk):
    n, w = x.shape
    tri = (lax.broadcasted_iota(jnp.int32, (blk, blk), 1) <= lax.broadcasted_iota(jnp.int32, (blk, blk), 0))
    tri = tri.astype(BF16)
    hi, lo = _split2(x)
    out = []
    for b in range(n // blk):
        rows = slice(b * blk, (b + 1) * blk)
        r = _dot(tri, jnp.concatenate([hi[rows, :], lo[rows, :]], axis=1))
        c = r[:, 0:w] + r[:, w:2 * w]
        if out:
            c = c + out[-1][blk - 1:blk, :]
        out.append(c)
    return jnp.concatenate(out, axis=0)


def _mod_kernel(c_ref, w_ref, b_ref, o_ref):
    c = c_ref[...]
    o_ref[0] = _dot3(c * _sigmoid(c), w_ref[0]) + b_ref[0]


def _mod_call(c, w_mod, b_mod):
    n_layers, d, n = w_mod.shape
    b = c.shape[0]
    tn = 1536
    return pl.pallas_call(
        _mod_kernel,
        grid=(n_layers, n // tn),
        in_specs=[
            pl.BlockSpec((b, d), lambda l, j: (0, 0)),
            pl.BlockSpec((1, d, tn), lambda l, j: (l, 0, j)),
            pl.BlockSpec((1, 1, tn), lambda l, j: (l, 0, j)),
        ],
        out_specs=pl.BlockSpec((1, b, tn), lambda l, j: (l, 0, j)),
        out_shape=jax.ShapeDtypeStruct((n_layers, b, n), F32),
        compiler_params=_cparams(("arbitrary", "arbitrary")),
        name="adaln_mod",
    )(c, w_mod, b_mod.reshape(n_layers, 1, n))


def _win_layout_kernel(w_ref, o_ref):
    fox_cols = 3 * FOX_W + FOX_HEADS
    rwkv_cols = 3 * RWKV_W + LORA_W
    w = w_ref[...]
    o_ref[:, C_FQ:C_MQ] = w[:, 0:3 * FOX_W].astype(BF16)
    o_ref[:, C_MQ:C_RR] = w[:, fox_cols + rwkv_cols:fox_cols + rwkv_cols + 3 * MOBA_W].astype(BF16)
    o_ref[:, C_RR:C_FF] = w[:, fox_cols:fox_cols + rwkv_cols].astype(BF16)
    lane = lax.broadcasted_iota(jnp.int32, (w.shape[0], FF_PAD), 1)
    o_ref[:, C_FF:NP_COLS] = jnp.where(lane < FOX_HEADS, w[:, 3 * FOX_W:3 * FOX_W + FF_PAD], 0.0).astype(BF16)


def _win_layout_call(w_in):
    n_layers, d, n = w_in.shape
    tr = 256
    rows = n_layers * d
    out = pl.pallas_call(
        _win_layout_kernel,
        grid=(rows // tr,),
        in_specs=[pl.BlockSpec((tr, n), lambda i: (i, 0))],
        out_specs=pl.BlockSpec((tr, NP_COLS), lambda i: (i, 0)),
        out_shape=jax.ShapeDtypeStruct((rows, NP_COLS), BF16),
        compiler_params=_cparams(("arbitrary",)),
        name="w_in_layout",
    )(w_in.reshape(rows, n))
    return out.reshape(n_layers, d, NP_COLS)


def _inproj_kernel(x_ref, mod_ref, nw_ref, w_ref, mu_ref, o_ref, prev, *, tm, nt):
    i = pl.program_id(0)
    m = mod_ref[0]
    h = _rmsnorm(x_ref[...], nw_ref[...]) * (1.0 + m[1:2, :]) + m[0:1, :]
    p = _dot(h.astype(BF16), w_ref[...])
    o_ref[:, 0:C_RR] = p[:, 0:C_RR]
    o_ref[:, C_FF:NP_COLS] = p[:, C_FF:NP_COLS]
    feat = p[:, C_RR:C_FF]

    @pl.when(i % nt == 0)
    def _():
        prev[...] = jnp.zeros(prev.shape, F32)

    rolled = pltpu.roll(feat, 1, 0)
    first = lax.broadcasted_iota(jnp.int32, (SUBLANES, C_FF - C_RR), 0) == 0
    top = jnp.where(first, prev[...], rolled[0:SUBLANES, :])
    shifted = jnp.concatenate([top, rolled[SUBLANES:, :]], axis=0)
    prev[...] = feat[tm - 1:tm, :]
    o_ref[:, C_RR:C_FF] = feat + (shifted - feat) * mu_ref[...]


def _inproj_call(xf, mod_l, norm_w, w_in_p, mu_row, layer, seq):
    rows, d = xf.shape
    tm = 512
    nt = seq // tm
    kern = functools.partial(_inproj_kernel, tm=tm, nt=nt)
    return pl.pallas_call(
        kern,
        grid=(rows // tm,),
        in_specs=[
            pl.BlockSpec((tm, d), lambda i: (i, 0)),
            pl.BlockSpec((1, 6, d), lambda i: (i // nt, 0, 0)),
            pl.BlockSpec((1, d), lambda i: (0, 0)),
            pl.BlockSpec((None, d, NP_COLS), lambda i: (layer, 0, 0)),
            pl.BlockSpec((1, C_FF - C_RR), lambda i: (0, 0)),
        ],
        out_specs=pl.BlockSpec((tm, NP_COLS), lambda i: (i, 0)),
        out_shape=jax.ShapeDtypeStruct((rows, NP_COLS), F32),
        scratch_shapes=[pltpu.VMEM((1, C_FF - C_RR), F32)],
        compiler_params=_cparams(("arbitrary",)),
        name="in_proj",
    )(xf, mod_l, norm_w, w_in_p, mu_row)


def _softmax_tiles(s, carry, v_t):
    m_new = [jnp.maximum(c[0], jnp.max(x, axis=0, keepdims=True)) for x, c in zip(s, carry)]
    p = [jnp.exp2(x - m) for x, m in zip(s, m_new)]
    pv = [_dot(v, x.astype(BF16)) for v, x in zip(v_t, p)]
    out = []
    for (m, l, acc), mn, x, y in zip(carry, m_new, p, pv):
        alpha = jnp.exp2(m - mn)
        out.append((mn, alpha * l + jnp.sum(x, axis=0, keepdims=True), alpha * acc + y))
    return out


def _first_tiles(s, v_t):
    m = [jnp.max(x, axis=0, keepdims=True) for x in s]
    p = [jnp.exp2(x - mm) for x, mm in zip(s, m)]
    pv = [_dot(v, x.astype(BF16)) for v, x in zip(v_t, p)]
    return [(mm, jnp.sum(x, axis=0, keepdims=True), y) for mm, x, y in zip(m, p, pv)]


def _flatten(carry):
    return tuple(a for c in carry for a in c)


def _unflatten(flat):
    return [tuple(flat[3 * h:3 * h + 3]) for h in range(len(flat) // 3)]


def _store_heads(o_ref, carry):
    o_ref[...] = jnp.concatenate([acc / l for _, l, acc in carry], axis=0).T.astype(o_ref.dtype)


F_COL = HEAD_DIM
N_SPLIT = 3
FOX_XW = FOX_W + FF_PAD


def _fox_selectors():
    selk = np.zeros((FOX_XW, FOX_HEADS * LANES), np.float32)
    for h in range(FOX_HEADS):
        for c in range(HEAD_DIM):
            selk[h * HEAD_DIM + c, h * LANES + c] = 1.0
    selq = selk.T.copy()
    for h in range(FOX_HEADS):
        for s in range(N_SPLIT):
            selk[FOX_W + FOX_HEADS * s + h, h * LANES + F_COL + s] = 1.0
            selq[h * LANES + F_COL + N_SPLIT + s, FOX_W + FOX_HEADS * s + h] = 1.0
    return jnp.asarray(selk, BF16), jnp.asarray(selq, BF16)


def _fox_kernel(q_ref, k_ref, v_ref, ff_ref, fb_ref, selk_ref, selq_ref, o_ref, ka, v_t, fp, s_buf, *, tq, tk,
                n_kv):
    qi = pl.program_id(1)
    heads = range(FOX_HEADS)

    @pl.when(qi == 0)
    def _():
        lane = lax.broadcasted_iota(jnp.int32, (1, LANES), 1)
        f = _cumsum_rows(_log_sigmoid(ff_ref[...] + fb_ref[...]), MXU_N) * LOG2E
        f = jnp.where(lane < FOX_HEADS, f, 0.0)
        hi = f.astype(BF16).astype(F32)
        mid = (f - hi).astype(BF16).astype(F32)
        lo = (f - hi - mid).astype(BF16).astype(F32)
        fp[...] = (hi + pltpu.roll(mid, FOX_HEADS, 1) + pltpu.roll(lo, 2 * FOX_HEADS, 1)).astype(BF16)
        x = jnp.concatenate([k_ref[...].astype(BF16), fp[...]], axis=1)
        lane4 = lax.broadcasted_iota(jnp.int32, (1, FOX_HEADS * LANES), 1) % LANES
        ones = jnp.where((lane4 >= F_COL + N_SPLIT) & (lane4 < F_COL + 2 * N_SPLIT), 1.0, 0.0)
        ka[...] = (_dot(x, selk_ref[...]) + ones).astype(BF16)
        for j in range(n_kv):
            v_t[j] = v_ref[j * tk:(j + 1) * tk, :].T.astype(BF16)

    q0 = pl.multiple_of(qi * tq, tq)
    xq = jnp.concatenate([(q_ref[...] * ATTN_SCALE).astype(BF16), fp[pl.ds(q0, tq), :]], axis=1)
    sub = lax.broadcasted_iota(jnp.int32, (FOX_HEADS * LANES, tq), 0) % LANES
    neg = jnp.where((sub >= F_COL) & (sub < F_COL + N_SPLIT), -1.0, 0.0)
    q_all = (_dot_nt(selq_ref[...], xq) + neg).astype(BF16)
    q_t = [q_all[h * LANES:(h + 1) * LANES, :] for h in heads]
    lag = (lax.broadcasted_iota(jnp.int32, (tk, tq), 0) - lax.broadcasted_iota(jnp.int32, (tk, tq), 1))
    n_diag = tq // tk

    def logits(j):
        k0 = pl.multiple_of(j * tk, tk)
        return [_dot(ka[pl.ds(k0, tk), h * LANES:(h + 1) * LANES], q_t[h]) for h in heads]

    def values(j):
        vj = v_t[j]
        return [vj[h * HEAD_DIM:(h + 1) * HEAD_DIM, :] for h in heads]

    def put(slot, s):
        for h in heads:
            s_buf[slot * FOX_HEADS + h] = s[h]

    j0 = qi * n_diag
    last = jnp.maximum(j0 - 1, 0)
    s_diag = [logits(j0 + dd) for dd in range(n_diag)]
    put(0, logits(0))
    carry = _first_tiles([jnp.where(lag <= 0, x, -jnp.inf) for x in s_diag[0]], values(j0))
    for dd in range(1, n_diag):
        carry = _softmax_tiles([jnp.where(lag <= -dd * tk, x, -jnp.inf) for x in s_diag[dd]], carry,
                               values(j0 + dd))

    def step(j, slot, carry):
        put(1 - slot, logits(jnp.minimum(j + 1, last)))
        s_cur = [s_buf[slot * FOX_HEADS + h] for h in heads]
        return _softmax_tiles(s_cur, carry, values(j))

    def body(jj, flat):
        carry = _unflatten(flat)
        for slot in range(2):
            carry = step(2 * jj + slot, slot, carry)
        return _flatten(carry)

    assert n_diag % 2 == 0
    _store_heads(o_ref, _unflatten(lax.fori_loop(0, j0 // 2, body, _flatten(carry))))


def _fox_call(p, f_bias_row, batch, seq):
    tq = 512
    tk = 256
    nq = seq // tq
    selk, selq = _fox_selectors()
    kern = functools.partial(_fox_kernel, tq=tq, tk=tk, n_kv=seq // tk)
    return pl.pallas_call(
        kern,
        grid=(batch, nq),
        in_specs=[
            pl.BlockSpec((tq, FOX_W), lambda b, i: (b * nq + i, C_FQ // FOX_W)),
            pl.BlockSpec((seq, FOX_W), lambda b, i: (b, C_FK // FOX_W)),
            pl.BlockSpec((seq, FOX_W), lambda b, i: (b, C_FV // FOX_W)),
            pl.BlockSpec((seq, FF_PAD), lambda b, i: (b, C_FF // FF_PAD)),
            pl.BlockSpec((1, FF_PAD), lambda b, i: (0, 0)),
            pl.BlockSpec(selk.shape, lambda b, i: (0, 0)),
            pl.BlockSpec(selq.shape, lambda b, i: (0, 0)),
        ],
        out_specs=pl.BlockSpec((tq, FOX_W), lambda b, i: (b * nq + i, 0)),
        out_shape=jax.ShapeDtypeStruct((batch * seq, FOX_W), BF16),
        scratch_shapes=[
            pltpu.VMEM((seq, FOX_HEADS * LANES), BF16),
            pltpu.VMEM((seq // tk, FOX_W, tk), BF16),
            pltpu.VMEM((seq, FF_PAD), BF16),
            pltpu.VMEM((2 * FOX_HEADS, tk, tq), F32),
        ],
        compiler_params=_cparams(("arbitrary", "arbitrary")),
        name="fox_attention",
    )(p, p, p, p, f_bias_row, selk, selq)


def _moba_selectors():
    selk = np.zeros((MOBA_W, MOBA_HEADS * LANES), np.float32)
    for h in range(MOBA_HEADS):
        for c in range(HEAD_DIM):
            selk[h * HEAD_DIM + c, h * LANES + c] = 1.0
    return jnp.asarray(selk, BF16), jnp.asarray(selk.T, BF16)


def _moba_kernel(q_ref, k_ref, v_ref, selk_ref, selq_ref, o_ref, ka, v_t, kmean, s_buf, *, blk, n_kb):
    own = pl.program_id(1)
    heads = range(MOBA_HEADS)

    @pl.when(own == 0)
    def _():
        kf = k_ref[...]
        ka[...] = _dot(kf.astype(BF16), selk_ref[...]).astype(BF16)
        means = [jnp.mean(kf[n * blk:(n + 1) * blk, :], axis=0, keepdims=True) for n in range(n_kb)]
        if n_kb < SUBLANES:
            means.append(jnp.zeros((SUBLANES - n_kb, MOBA_W), F32))
        means = jnp.concatenate(means, axis=0)
        head_of_lane = lax.broadcasted_iota(jnp.int32, (SUBLANES, MOBA_W), 1) // HEAD_DIM
        for h in heads:
            kmean[h * SUBLANES:(h + 1) * SUBLANES, :] = jnp.where(head_of_lane == h, means, 0.0)
        for n in range(n_kb):
            v_t[n] = v_ref[n * blk:(n + 1) * blk, :].T.astype(BF16)

    qf = q_ref[...]
    qs = (qf * ATTN_SCALE).astype(BF16)
    q_all = _dot_nt(selq_ref[...], qs).astype(BF16)
    q_t = [q_all[h * LANES:(h + 1) * LANES, :] for h in heads]
    sub = lax.broadcasted_iota(jnp.int32, (SUBLANES, blk), 0)
    past = sub < own
    causal = (lax.broadcasted_iota(jnp.int32, (blk, blk), 0) <= lax.broadcasted_iota(jnp.int32, (blk, blk), 1))

    sel = []
    gates = _dot3_nt(kmean[...], qf)
    for h in heads:
        gate = gates[h * SUBLANES:(h + 1) * SUBLANES, :]
        selm = jnp.zeros((SUBLANES, blk), F32)
        for n in range(n_kb):
            gn = gate[n:n + 1, :]
            beats = past & ((gate > gn) | ((gate == gn) & (sub < n)))
            rank = jnp.sum(beats.astype(F32), axis=0, keepdims=True)
            selm = jnp.where(sub == n, (rank < MOBA_TOPK).astype(F32), selm)
        sel.append(jnp.where(past, selm, 0.0))

    def logits(j):
        k0 = pl.multiple_of(j * blk, blk)
        return [_dot(ka[pl.ds(k0, blk), h * LANES:(h + 1) * LANES], q_t[h]) for h in heads]

    def values(j):
        vj = v_t[j]
        return [vj[h * HEAD_DIM:(h + 1) * HEAD_DIM, :] for h in heads]

    def put(slot, s):
        for h in heads:
            s_buf[slot * MOBA_HEADS + h] = s[h]

    def step(j, slot, carry, live, prefetch):
        if prefetch:
            put(1 - slot, logits(jnp.minimum(j + 1, last)))
        picked = [(jnp.sum(jnp.where(sub == j, sel[h], 0.0), axis=0, keepdims=True) > 0.5) & live for h in heads]
        s = [jnp.where(pk, s_buf[slot * MOBA_HEADS + h], -jnp.inf) for h, pk in zip(heads, picked)]
        return _softmax_tiles(s, carry, values(j))

    last = jnp.maximum(own - 1, 0)
    s_own = logits(own)
    put(0, logits(0))
    carry = _first_tiles([jnp.where(causal, x, -jnp.inf) for x in s_own], values(own))

    def body(jj, flat):
        carry = _unflatten(flat)
        for slot in range(2):
            carry = step(2 * jj + slot, slot, carry, True, True)
        return _flatten(carry)

    carry = _unflatten(lax.fori_loop(0, own // 2, body, _flatten(carry)))
    carry = step(last, 0, carry, own % 2 == 1, False)
    _store_heads(o_ref, carry)


def _moba_call(p, batch, seq):
    blk = MOBA_BLOCK
    n_kb = seq // blk
    assert n_kb <= SUBLANES
    selk, selq = _moba_selectors()
    kern = functools.partial(_moba_kernel, blk=blk, n_kb=n_kb)
    return pl.pallas_call(
        kern,
        grid=(batch, n_kb),
        in_specs=[
            pl.BlockSpec((blk, MOBA_W), lambda b, i: (b * n_kb + i, C_MQ // MOBA_W)),
            pl.BlockSpec((seq, MOBA_W), lambda b, i: (b, C_MK // MOBA_W)),
            pl.BlockSpec((seq, MOBA_W), lambda b, i: (b, C_MV // MOBA_W)),
            pl.BlockSpec(selk.shape, lambda b, i: (0, 0)),
            pl.BlockSpec(selq.shape, lambda b, i: (0, 0)),
        ],
        out_specs=pl.BlockSpec((blk, MOBA_W), lambda b, i: (b * n_kb + i, 0)),
        out_shape=jax.ShapeDtypeStruct((batch * seq, MOBA_W), BF16),
        scratch_shapes=[
            pltpu.VMEM((seq, MOBA_HEADS * LANES), BF16),
            pltpu.VMEM((n_kb, MOBA_W, blk), BF16),
            pltpu.VMEM((MOBA_HEADS * SUBLANES, MOBA_W), F32),
            pltpu.VMEM((2 * MOBA_HEADS, blk, blk), F32),
        ],
        compiler_params=_cparams(("arbitrary", "arbitrary")),
        name="moba_attention",
    )(p, p, p, selk, selq)


DECAY_SCALE = float(np.exp(-0.5))


def _rwkv_kernel(r_ref, k_ref, v_ref, lo_ref, w0, w2, a0, a2, g2, kkw, kaw, rkw, lnw, lnb, o_ref,
                 state, a_t, r_t, k_t, b_t, k_b, b_b, v_s, w_c, y_s, *, tm, sub):
    i = pl.program_id(1)
    ch = RWKV_CHUNK
    n_ch = sub // ch

    @pl.when(i == 0)
    def _():
        state[...] = jnp.zeros(state.shape, F32)

    gi = lax.broadcasted_iota(jnp.int32, (MXU_N, MXU_N), 0) // HEAD_DIM
    gj = lax.broadcasted_iota(jnp.int32, (MXU_N, MXU_N), 1) // HEAD_DIM
    group = (gi == gj).astype(BF16)
    ti = lax.broadcasted_iota(jnp.int32, (ch, ch), 0)
    tj = lax.broadcasted_iota(jnp.int32, (ch, ch), 1)
    strict = tj < ti
    incl = tj <= ti
    si = lax.broadcasted_iota(jnp.int32, (sub, sub), 0)
    sj = lax.broadcasted_iota(jnp.int32, (sub, sub), 1)
    chunk_tri = ((sj <= si) & (si // ch == sj // ch)).astype(BF16)

    def project(r0):
        lo = lo_ref[r0:r0 + sub, :]
        w_lo = lo[:, 0:DECAY_LORA]
        a_lo = lo[:, DECAY_LORA:DECAY_LORA + AAA_LORA]
        g_lo = lo[:, DECAY_LORA + AAA_LORA:LORA_W]
        kk = k_ref[r0:r0 + sub, :] * kkw[...]
        return (_dot3(jnp.tanh(w_lo), w2[...]), _dot3(a_lo, a2[...]), _dot(_sigmoid(g_lo).astype(BF16), g2[...]),
                _head_sums(kk * kk, group, exact=False))

    def prepare(r0, zw, za, gate, ss):
        span = slice(r0, r0 + sub)
        r = r_ref[span, :]
        k = k_ref[span, :]
        v = v_ref[span, :]

        lw = -DECAY_SCALE * _sigmoid(w0[...] + zw)
        eta = _sigmoid(a0[...] + za)

        kk = k * kkw[...] * jnp.minimum(lax.rsqrt(ss), 1e12)
        kp = k * (1.0 + (eta - 1.0) * kaw[...])
        bb = kk * eta

        lw_hi, lw_lo = _split2(lw)
        lc2 = _dot(chunk_tri, jnp.concatenate([lw_hi, lw_lo], axis=1))
        lc = lc2[:, 0:RWKV_W] + lc2[:, RWKV_W:2 * RWKV_W]
        a_t[span, :] = -kk * jnp.exp(lc - lw)
        r_t[span, :] = r * jnp.exp(lc)
        einv = jnp.exp(-lc)
        k_t[span, :] = kp * einv
        b_t[span, :] = bb * einv
        v_s[span, :] = v
        for c in range(n_ch):
            rows = slice(c * ch, (c + 1) * ch)
            dst = slice(r0 + c * ch, r0 + (c + 1) * ch)
            last = lc[(c + 1) * ch - 1:(c + 1) * ch, :]
            e = jnp.exp(last - lc[rows, :])
            k_b[dst, :] = kp[rows, :] * e
            b_b[dst, :] = bb[rows, :] * e
            w_c[r0 // ch + c] = jnp.broadcast_to(jnp.exp(last), (8, RWKV_W))
        return r * kp * rkw[...], v, gate

    def state_free(r0):
        chains = [(c, h) for c in range(n_ch) for h in range(RWKV_HEADS)]

        def tile(ref):
            return [ref[r0 + c * ch:r0 + (c + 1) * ch, h * HEAD_DIM:(h + 1) * HEAD_DIM] for c, h in chains]

        return _rwkv_state_free(tile(a_t), tile(r_t), tile(v_s), tile(k_b), tile(b_b), tile(b_t), tile(k_t),
                                strict, incl)

    def finish(r0, bonus_arg, v, gate):
        span = slice(r0, r0 + sub)
        y = y_s[span, :]
        inv_d = 1.0 / HEAD_DIM
        mean = _head_sums(y, group, exact=False) * inv_d
        d = y - mean
        var = _head_sums(d * d, group, exact=False) * inv_d
        yn = d * lax.rsqrt(var + GN_EPS) * lnw[...] + lnb[...]
        bonus = _head_sums(bonus_arg, group, exact=True) * v
        o_ref[span, :] = ((yn + bonus) * gate).astype(o_ref.dtype)

    starts = list(range(0, tm, sub))
    s_cur = [state[h] for h in range(RWKV_HEADS)]
    prepared = [prepare(r0, *project(r0)) for r0 in starts]
    pending = []
    for r0 in starts:
        stages = state_free(r0)
        while True:
            try:
                next(stages)
            except StopIteration as done:
                free = done.value
                break
            if pending:
                s_cur = pending.pop(0)(s_cur)
        while pending:
            s_cur = pending.pop(0)(s_cur)
        pending = [functools.partial(_rwkv_state_step, c, r0, free, w_c=w_c, y_s=y_s) for c in range(n_ch)]
    while pending:
        s_cur = pending.pop(0)(s_cur)
    for h in range(RWKV_HEADS):
        state[h] = s_cur[h]
    for r0, vals in zip(starts, prepared):
        finish(r0, *vals)


def _rwkv_state_free(at, rt, vv, kbar, bbar, btl, ktl, strict, incl):
    ch = RWKV_CHUNK
    ids = range(len(at))
    pad = jnp.zeros((LANES - ch, HEAD_DIM), BF16)
    m4 = [_dot_nt(jnp.concatenate([at[i], rt[i]], axis=0).astype(BF16),
                  jnp.concatenate([btl[i].astype(BF16), pad, ktl[i].astype(BF16), pad], axis=0)) for i in ids]
    yield
    a_ab = [jnp.where(strict, m4[i][0:ch, 0:ch], 0.0) for i in ids]
    a_ak = [jnp.where(strict, m4[i][0:ch, LANES:LANES + ch], 0.0) for i in ids]
    a_rb = [jnp.where(incl, m4[i][ch:2 * ch, 0:ch], 0.0) for i in ids]
    a_rk = [jnp.where(incl, m4[i][ch:2 * ch, LANES:LANES + ch], 0.0) for i in ids]
    avk = [_mm(jnp.concatenate([a_ak[i], a_rk[i]], axis=0), vv[i]) for i in ids]
    yield
    pw = a_ab
    tx = [jnp.concatenate([at[i], avk[i][0:ch, :]], axis=1) for i in ids]
    span = 1
    while 2 * span < ch:
        x = [_mm(pw[i], jnp.concatenate([tx[i], pw[i]], axis=1)) for i in ids]
        pw = [x[i][:, 2 * ch:3 * ch] for i in ids]
        tx = [tx[i] + x[i][:, 0:2 * ch] for i in ids]
        span *= 2
        yield
    tx = [tx[i] + _mm(pw[i], tx[i]) for i in ids]
    yield
    ry = [_mm(a_rb[i], tx[i]) for i in ids]
    rhat = [rt[i] + ry[i][:, 0:ch] for i in ids]
    yhat = [avk[i][ch:2 * ch, :] + ry[i][:, ch:2 * ch] for i in ids]
    yield
    z = [_mm(tx[i][:, 0:ch].T, bbar[i]) for i in ids]
    yield
    kv = [_mm(jnp.concatenate([vv[i], tx[i][:, ch:2 * ch]], axis=0).T,
              jnp.concatenate([kbar[i], bbar[i]], axis=0)) for i in ids]
    return rhat, yhat, z, kv


def _rwkv_state_step(c, r0, free, s_cur, w_c, y_s):
    ch = RWKV_CHUNK
    rhat, yhat, z, kv = free
    base = c * RWKV_HEADS
    ys = [_mm_nt(rhat[base + h], s_cur[h]) for h in range(RWKV_HEADS)]
    sz = [_mm(s_cur[h], z[base + h]) for h in range(RWKV_HEADS)]
    out = []
    for h in range(RWKV_HEADS):
        hs = slice(h * HEAD_DIM, (h + 1) * HEAD_DIM)
        y_s[r0 + c * ch:r0 + (c + 1) * ch, hs] = ys[h] + yhat[base + h]
        out.append(s_cur[h] * w_c[r0 // ch + c][0:1, hs] + sz[h] + kv[base + h])
    return out


def _rwkv_call(p, prm, batch, seq):
    tm = 512
    sub = 128
    nt = seq // tm
    kern = functools.partial(_rwkv_kernel, tm=tm, sub=sub)

    def rows(width, cstart):
        return pl.BlockSpec((tm, width), lambda b, i: (b * nt + i, cstart // width))

    def full(a):
        return pl.BlockSpec(a.shape, lambda b, i: (0,) * a.ndim)

    params = [prm[n] for n in ("w0", "w2", "a0", "a2", "g2", "k_k", "k_a", "r_k", "ln_w", "ln_b")]
    big = pltpu.VMEM((tm, RWKV_W), F32)
    return pl.pallas_call(
        kern,
        grid=(batch, nt),
        in_specs=[rows(RWKV_W, C_RR), rows(RWKV_W, C_RK), rows(RWKV_W, C_RV), rows(LORA_W, C_LORA)]
        + [full(a) for a in params],
        out_specs=pl.BlockSpec((tm, RWKV_W), lambda b, i: (b * nt + i, 0)),
        out_shape=jax.ShapeDtypeStruct((batch * seq, RWKV_W), BF16),
        scratch_shapes=[
            pltpu.VMEM((RWKV_HEADS, HEAD_DIM, HEAD_DIM), F32),
            big, big, big, big, big, big, big,
            pltpu.VMEM((tm // RWKV_CHUNK, 8, RWKV_W), F32),
            big,
        ],
        compiler_params=_cparams(("arbitrary", "arbitrary")),
        name="rwkv7_mix",
    )(p, p, p, p, *params)


FFN_HALO = 16


MXU_K = 256


def _ffn_chunks(d_ff):
    cut = (d_ff // MXU_K + 1) // 2 * MXU_K
    return ((0, cut), (cut, d_ff))


def _ffn_kernel(yf_ref, yfp_ref, yr_ref, yrp_ref, ym_ref, ymp_ref, x_ref, xp_ref, mod_ref, nw_ref, wo_ref,
                wu_ref, cw_ref, cb_ref, wd_ref, nf_ref, o_ref, *, nt, d_ff, final):
    i = pl.program_id(0)
    m = mod_ref[0]

    def ext(prev_ref, cur_ref):
        return jnp.concatenate([prev_ref[...], cur_ref[...]], axis=0)

    z = _dot(ext(yfp_ref, yf_ref).astype(BF16), wo_ref[0:FOX_W, :])
    z = z + _dot(ext(yrp_ref, yr_ref).astype(BF16), wo_ref[FOX_W:FOX_W + RWKV_W, :])
    z = z + _dot(ext(ymp_ref, ym_ref).astype(BF16), wo_ref[FOX_W + RWKV_W:, :])
    x1 = ext(xp_ref, x_ref) + m[2:3, :] * z
    h = _rmsnorm(x1, nw_ref[...]) * (1.0 + m[4:5, :]) + m[3:4, :]
    keep = jnp.where(i % nt == 0, 0.0, 1.0)
    he = jnp.concatenate([h[0:FFN_HALO, :] * keep, h[FFN_HALO:, :]], axis=0).astype(BF16)

    def conv(u, lo, hi):
        cw = cw_ref[:, lo:hi]
        return (cb_ref[:, lo:hi] + cw[0:1, :] * pltpu.roll(u, 2, 0)[FFN_HALO:, :]
                + cw[1:2, :] * pltpu.roll(u, 1, 0)[FFN_HALO:, :] + cw[2:3, :] * u[FFN_HALO:, :])

    chunks = _ffn_chunks(d_ff)
    ups = [(_dot(he, wu_ref[:, lo:hi]), _dot(he, wu_ref[:, d_ff + lo:d_ff + hi])) for lo, hi in chunks]
    y = None
    for (lo, hi), (ug, uv) in zip(chunks, ups):
        g = conv(ug, lo, hi)
        act = g * _sigmoid(g) * conv(uv, d_ff + lo, d_ff + hi)
        part = _dot(act.astype(BF16), wd_ref[lo:hi, :])
        y = part if y is None else y + part
    out = x1[FFN_HALO:, :] + m[5:6, :] * y
    if final:
        out = _rmsnorm(out, nf_ref[...])
    o_ref[...] = out


def _ffn_call(y_fox, y_rwkv, y_moba, xf, mod_l, norm_w, w_out_b, w_up_b, conv_w, conv_b, w_down_b, norm_final,
              layer, seq, final):
    rows, d = xf.shape
    d_ff = w_down_b.shape[1]
    tm = 512
    nt = seq // tm
    hb = tm // FFN_HALO
    kern = functools.partial(_ffn_kernel, nt=nt, d_ff=d_ff, final=final)

    def tile_and_halo(width):
        return [pl.BlockSpec((tm, width), lambda i: (i, 0)),
                pl.BlockSpec((FFN_HALO, width), lambda i: (jnp.maximum(i * hb - 1, 0), 0))]

    def resident(a):
        return pl.BlockSpec((None,) + a.shape[1:], lambda i: (layer, 0, 0), pipeline_mode=pl.Buffered(1))

    return pl.pallas_call(
        kern,
        grid=(rows // tm,),
        in_specs=tile_and_halo(FOX_W) + tile_and_halo(RWKV_W) + tile_and_halo(MOBA_W) + tile_and_halo(d) + [
            pl.BlockSpec((1, 6, d), lambda i: (i // nt, 0, 0)),
            pl.BlockSpec((1, d), lambda i: (0, 0)),
            resident(w_out_b), resident(w_up_b),
            pl.BlockSpec((CONV_W, 2 * d_ff), lambda i: (0, 0)),
            pl.BlockSpec((1, 2 * d_ff), lambda i: (0, 0)),
            resident(w_down_b),
            pl.BlockSpec((1, d), lambda i: (0, 0)),
        ],
        out_specs=pl.BlockSpec((tm, d), lambda i: (i, 0)),
        out_shape=jax.ShapeDtypeStruct((rows, d), F32),
        compiler_params=_cparams(("arbitrary",)),
        name="out_proj_conv_ffn",
    )(y_fox, y_fox, y_rwkv, y_rwkv, y_moba, y_moba, xf, xf, mod_l, norm_w, w_out_b, w_up_b, conv_w, conv_b,
      w_down_b, norm_final)


def kernel(x, c, w_mod, b_mod, norm_mix, w_in, fox_f_bias, rwkv_mu, rwkv_w0, rwkv_w2, rwkv_a0, rwkv_a2,
           rwkv_g2, rwkv_k_k, rwkv_k_a, rwkv_r_k, rwkv_ln_w, rwkv_ln_b, w_out, norm_ffn, w_up, conv_w,
           conv_b, w_down, norm_final):
    batch, seq, d = x.shape
    n_layers = w_mod.shape[0]

    mod = _mod_call(c, w_mod, b_mod).reshape(n_layers, batch, 6, d)
    w_in_p = _win_layout_call(w_in)
    f_bias = jnp.pad(fox_f_bias, ((0, 0), (0, FF_PAD - FOX_HEADS)))
    w_out_b = w_out.astype(BF16)
    w_up_b = w_up.astype(BF16)
    w_down_b = w_down.astype(BF16)
    g2_b = rwkv_g2.astype(BF16)

    xf = x.reshape(batch * seq, d)
    for l in range(n_layers):
        row = lambda a: a[l].reshape(1, -1)
        prm = {
            "w0": row(rwkv_w0), "w2": rwkv_w2[l], "a0": row(rwkv_a0), "a2": rwkv_a2[l], "g2": g2_b[l],
            "k_k": row(rwkv_k_k), "k_a": row(rwkv_k_a), "r_k": row(rwkv_r_k), "ln_w": row(rwkv_ln_w),
            "ln_b": row(rwkv_ln_b),
        }
        p = _inproj_call(xf, mod[l], row(norm_mix), w_in_p, row(rwkv_mu), l, seq)
        y_fox = _fox_call(p, f_bias[l:l + 1], batch, seq)
        y_moba = _moba_call(p, batch, seq)
        y_rwkv = _rwkv_call(p, prm, batch, seq)
        xf = _ffn_call(y_fox, y_rwkv, y_moba, xf, mod[l], row(norm_ffn), w_out_b, w_up_b, conv_w[l],
                       conv_b[l].reshape(1, -1), w_down_b, norm_final.reshape(1, -1), l, seq,
                       final=(l == n_layers - 1))
    return xf.reshape(batch, seq, d)
```
